```python
import jax
import jax.numpy as jnp
from jax import lax
import numpy as np

D_MODEL = 1024
BATCH = 8
SEQ = 4096
DEPTH = 4

GRID_W = 64
CTX_LEN = 256
HEAD_DIM = 64
NA_HEADS = 4
WIN_H = 8
WIN_W = 16
GQA_Q_HEADS = 8
GQA_KV_HEADS = 2
GQA_GROUP = GQA_Q_HEADS // GQA_KV_HEADS
MLA_HEADS = 4
MLA_Q_RANK = 256
MLA_KV_RANK = 128
MLA_NOPE_DIM = 64
MLA_ROPE_DIM = 32
MLA_V_DIM = 64
ROPE_THETA = 10000.0
Q_BLOCK = 128
N_BRANCHES = 3
FFN_DIM = 2816
N_EXPERTS = 8
TOP_K = 2
EXPERT_DIM = 2816
EXPERT_ROW_BLOCK = 256
NORM_EPS = 1e-6
N_DENSE = (DEPTH + 1) // 2
N_MOE = DEPTH // 2

NA_WIDTH = NA_HEADS * HEAD_DIM
GQA_WIDTH = GQA_Q_HEADS * HEAD_DIM
MLA_WIDTH = MLA_HEADS * MLA_V_DIM
KV_SIZES = (NA_WIDTH, NA_WIDTH, GQA_KV_HEADS * HEAD_DIM, GQA_KV_HEADS * HEAD_DIM, MLA_KV_RANK, MLA_ROPE_DIM)
Q_SIZES = (NA_WIDTH, GQA_WIDTH, MLA_Q_RANK)
KV_WIDTH = sum(KV_SIZES)
Q_WIDTH = sum(Q_SIZES)
GATE_WIDTH = N_BRANCHES * D_MODEL
IN_WIDTH = KV_WIDTH + Q_WIDTH + GATE_WIDTH
NA_SCALE = HEAD_DIM ** -0.5
GQA_SCALE = HEAD_DIM ** -0.5
MLA_SCALE = (MLA_NOPE_DIM + MLA_ROPE_DIM) ** -0.5

kernel_name = 'hybrid_na_gqa_mla_moe_diffusion_trunk'


def split_last(t, sizes):
    return jnp.split(t, [int(s) for s in np.cumsum(sizes)[:-1]], axis=-1)


def rms_norm(x, g):
    xf = x.astype(jnp.float32)
    y = xf * lax.rsqrt(jnp.mean(xf * xf, axis=-1, keepdims=True) + NORM_EPS)
    return (y * g.astype(jnp.float32)).astype(x.dtype)


def modulate(h, shift, scale):
    return h * (1 + scale) + shift


def rope_2d(x, rows, cols):
    half = x.shape[-1] // 2
    n_freq = half // 2
    inv = jnp.power(ROPE_THETA, -jnp.arange(n_freq, dtype=jnp.float32) / n_freq)
    ang = jnp.concatenate([rows[:, None] * inv, cols[:, None] * inv], axis=-1)
    cos = jnp.cos(ang)[None, :, None, :]
    sin = jnp.sin(ang)[None, :, None, :]
    xf = x.astype(jnp.float32)
    x1, x2 = xf[..., :half], xf[..., half:]
    return jnp.concatenate([x1 * cos - x2 * sin, x2 * cos + x1 * sin], axis=-1).astype(x.dtype)


def attend(q, k, v, scale):
    s = jnp.einsum('bqhgd,bkhd->bhgqk', q, k).astype(jnp.float32) * scale
    p = jax.nn.softmax(s, axis=-1).astype(v.dtype)
    return jnp.einsum('bhgqk,bkhd->bqhgd', p, v)


def blocked_attention(q, k_lat, v_lat, k_ctx, v_ctx, scale):
    B, S = q.shape[:2]
    k = jnp.concatenate([k_ctx, k_lat], axis=1)
    v = jnp.concatenate([v_ctx, v_lat], axis=1)
    qb = jnp.moveaxis(q.reshape(B, S // Q_BLOCK, Q_BLOCK, *q.shape[2:]), 1, 0)
    o = lax.map(lambda q_blk: attend(q_blk, k, v, scale), qb)
    return jnp.moveaxis(o, 0, 1).reshape(B, S, -1)


def neighbourhood_attention(q, k, v, k_ctx, v_ctx, rpb, n_rows):
    B, S, H, d = q.shape
    wh = min(WIN_H, n_rows)
    r = jnp.arange(n_rows)
    j = jnp.arange(GRID_W)
    row0 = jnp.clip(r - wh // 2, 0, n_rows - wh)
    key_rows = row0[:, None] + jnp.arange(wh)[None, :]
    col0 = jnp.clip(j - WIN_W // 2, 0, GRID_W - WIN_W)
    in_win = (j[None, :] >= col0[:, None]) & (j[None, :] < col0[:, None] + WIN_W)
    dr = key_rows - r[:, None]
    dc = jnp.clip(j[None, :] - j[:, None], -(WIN_W - 1), WIN_W - 1)
    bias = rpb[:, dr[:, :, None, None] + WIN_H - 1, dc[None, None] + WIN_W - 1]
    bias = bias.transpose(1, 0, 3, 2, 4).astype(jnp.float32)
    qg = q.reshape(B, n_rows, GRID_W, H, d)
    kg = k.reshape(B, n_rows, GRID_W, H, d)[:, key_rows]
    vg = v.reshape(B, n_rows, GRID_W, H, d)[:, key_rows]
    s_nb = jnp.einsum('brqhd,brikhd->brhqik', qg, kg).astype(jnp.float32) * NA_SCALE + bias[None]
    s_nb = jnp.where(in_win[:, None, :], s_nb, -jnp.inf)
    s_ctx = jnp.einsum('brqhd,bchd->brhqc', qg, k_ctx).astype(jnp.float32) * NA_SCALE
    n_nb = wh * GRID_W
    s = jnp.concatenate([s_nb.reshape(B, n_rows, H, GRID_W, n_nb), s_ctx], axis=-1)
    p = jax.nn.softmax(s, axis=-1).astype(v.dtype)
    p_nb = p[..., :n_nb].reshape(B, n_rows, H, GRID_W, wh, GRID_W)
    p_ctx = p[..., n_nb:]
    o = jnp.einsum('brhqik,brikhd->brqhd', p_nb, vg) + jnp.einsum('brhqc,bchd->brqhd', p_ctx, v_ctx)
    return o.reshape(B, S, H * d)


def mixer_kv(p_kv, k_norm, kv_lora_norm, w_ukv, rows, cols):
    B, L = p_kv.shape[:2]
    k_na, v_na, k_g, v_g, c_kv, k_rope = split_last(p_kv, KV_SIZES)
    k_na = k_na.reshape(B, L, NA_HEADS, HEAD_DIM)
    v_na = v_na.reshape(B, L, NA_HEADS, HEAD_DIM)
    k_g = rms_norm(k_g.reshape(B, L, GQA_KV_HEADS, HEAD_DIM), k_norm)
    v_g = v_g.reshape(B, L, GQA_KV_HEADS, HEAD_DIM)
    kv_up = (rms_norm(c_kv, kv_lora_norm) @ w_ukv).reshape(B, L, MLA_HEADS, MLA_NOPE_DIM + MLA_V_DIM)
    k_nope, v_m = kv_up[..., :MLA_NOPE_DIM], kv_up[..., MLA_NOPE_DIM:]
    k_rope = k_rope.reshape(B, L, 1, MLA_ROPE_DIM)
    if rows is not None:
        k_g = rope_2d(k_g, rows, cols)
        k_rope = rope_2d(k_rope, rows, cols)
    k_m = jnp.concatenate([k_nope, jnp.broadcast_to(k_rope, (B, L, MLA_HEADS, MLA_ROPE_DIM))], axis=-1)
    return k_na, v_na, k_g, v_g, k_m, v_m


def mixer_q(p_q, q_norm, q_lora_norm, w_uq, rows, cols):
    B, L = p_q.shape[:2]
    q_na, q_g, c_q = split_last(p_q, Q_SIZES)
    q_na = q_na.reshape(B, L, NA_HEADS, HEAD_DIM)
    q_g = rms_norm(q_g.reshape(B, L, GQA_Q_HEADS, HEAD_DIM), q_norm)
    q_up = (rms_norm(c_q, q_lora_norm) @ w_uq).reshape(B, L, MLA_HEADS, MLA_NOPE_DIM + MLA_ROPE_DIM)
    q_nope, q_rope = q_up[..., :MLA_NOPE_DIM], q_up[..., MLA_NOPE_DIM:]
    if rows is not None:
        q_g = rope_2d(q_g, rows, cols)
        q_rope = rope_2d(q_rope, rows, cols)
    q_g = q_g.reshape(B, L, GQA_KV_HEADS, GQA_GROUP, HEAD_DIM)
    q_m = jnp.concatenate([q_nope, q_rope], axis=-1)
    return q_na, q_g, q_m


def merge_branches(y_na, y_gqa, y_mla, p_gate, w_o_na, w_o_gqa, w_o_mla, w_out):
    g_na, g_gqa, g_mla = jnp.split(jax.nn.sigmoid(p_gate), N_BRANCHES, axis=-1)
    m = g_na * (y_na @ w_o_na) + g_gqa * (y_gqa @ w_o_gqa) + g_mla * (y_mla @ w_o_mla)
    return m @ w_out


def swiglu(h, w_gu, w_dn):
    gate, up = jnp.split(h @ w_gu, 2, axis=-1)
    return (jax.nn.silu(gate) * up) @ w_dn


def moe_swiglu(h, w_router, w_gu, w_dn):
    B, L, D = h.shape
    n = B * L
    n_assign = n * TOP_K
    hf = h.reshape(n, D)
    logits = (hf @ w_router).astype(jnp.float32)
    top_logit, top_e = lax.top_k(logits, TOP_K)
    top_w = jax.nn.softmax(top_logit, axis=-1)
    e_flat = top_e.reshape(-1)
    tok_flat = jnp.repeat(jnp.arange(n, dtype=jnp.int32), TOP_K)
    order = jnp.argsort(e_flat)
    e_sorted = e_flat[order]
    counts = jnp.bincount(e_flat, length=N_EXPERTS)
    padded = (counts + EXPERT_ROW_BLOCK - 1) // EXPERT_ROW_BLOCK * EXPERT_ROW_BLOCK
    pad_end = jnp.cumsum(padded)
    pad_start = pad_end - padded
    raw_start = jnp.cumsum(counts) - counts
    dest = pad_start[e_sorted] + jnp.arange(n_assign) - raw_start[e_sorted]
    n_blocks = -(-(n_assign + N_EXPERTS * (EXPERT_ROW_BLOCK - 1)) // EXPERT_ROW_BLOCK)
    n_rows = n_blocks * EXPERT_ROW_BLOCK
    src_tok = jnp.zeros((n_rows,), jnp.int32).at[dest].set(tok_flat[order])
    row_w = jnp.zeros((n_rows,), jnp.float32).at[dest].set(top_w.reshape(-1)[order])
    block_start = jnp.arange(n_blocks) * EXPERT_ROW_BLOCK
    block_expert = jnp.minimum(jnp.sum(block_start[:, None] >= pad_end[None, :], axis=-1), N_EXPERTS - 1)
    xb = hf[src_tok].reshape(n_blocks, EXPERT_ROW_BLOCK, D)
    yb = lax.map(lambda a: swiglu(a[0], w_gu[a[1]], w_dn[a[1]]), (xb, block_expert))
    yb = yb.reshape(n_rows, D) * row_w[:, None].astype(h.dtype)
    return jax.ops.segment_sum(yb, src_tok, num_segments=n).reshape(B, L, D)


def channel_mixer(t, layer, w_ffn_gu, w_ffn_dn, w_router, w_moe_gu, w_moe_dn):
    j = layer // 2
    if layer % 2 == 0:
        return swiglu(t, w_ffn_gu[j], w_ffn_dn[j])
    return moe_swiglu(t, w_router[j], w_moe_gu[j], w_moe_dn[j])


def setup_inputs(seed: int = 0) -> dict:
    key = jax.random.key(seed)
    ks = jax.random.split(key, 26)
    f32 = jnp.float32
    D = D_MODEL

    def nrm(k, shape, scale):
        return jax.random.normal(k, shape, f32) * scale

    def gain(k, shape):
        return 1.0 + 0.02 * jax.random.normal(k, shape, f32)

    return {
        'x': nrm(ks[0], (BATCH, SEQ, D), 1.0),
        'c': nrm(ks[1], (BATCH, D), 1.0),
        'ctx': nrm(ks[2], (BATCH, CTX_LEN, D), 1.0),
        'c_ctx': nrm(ks[3], (D,), 1.0),
        'w_ada': nrm(ks[4], (DEPTH, D, 6 * D), 0.5 * D ** -0.5),
        'b_ada': nrm(ks[5], (DEPTH, 6 * D), 0.02),
        'norm_mix': gain(ks[6], (DEPTH, D)),
        'norm_ffn': gain(ks[7], (DEPTH, D)),
        'w_in': nrm(ks[8], (DEPTH, D, IN_WIDTH), D ** -0.5),
        'q_norm_gqa': gain(ks[9], (DEPTH, HEAD_DIM)),
        'k_norm_gqa': gain(ks[10], (DEPTH, HEAD_DIM)),
        'q_lora_norm': gain(ks[11], (DEPTH, MLA_Q_RANK)),
        'kv_lora_norm': gain(ks[12], (DEPTH, MLA_KV_RANK)),
        'w_uq': nrm(ks[13], (DEPTH, MLA_Q_RANK, MLA_HEADS * (MLA_NOPE_DIM + MLA_ROPE_DIM)), MLA_Q_RANK ** -0.5),
        'w_ukv': nrm(ks[14], (DEPTH, MLA_KV_RANK, MLA_HEADS * (MLA_NOPE_DIM + MLA_V_DIM)), MLA_KV_RANK ** -0.5),
        'rpb': nrm(ks[15], (DEPTH, NA_HEADS, 2 * WIN_H - 1, 2 * WIN_W - 1), 0.1),
        'w_o_na': nrm(ks[16], (DEPTH, NA_WIDTH, D), NA_WIDTH ** -0.5),
        'w_o_gqa': nrm(ks[17], (DEPTH, GQA_WIDTH, D), GQA_WIDTH ** -0.5),
        'w_o_mla': nrm(ks[18], (DEPTH, MLA_WIDTH, D), MLA_WIDTH ** -0.5),
        'w_out': nrm(ks[19], (DEPTH, D, D), D ** -0.5),
        'w_ffn_gu': nrm(ks[20], (N_DENSE, D, 2 * FFN_DIM), D ** -0.5),
        'w_ffn_dn': nrm(ks[21], (N_DENSE, FFN_DIM, D), FFN_DIM ** -0.5),
        'w_router': nrm(ks[22], (N_MOE, D, N_EXPERTS), D ** -0.5),
        'w_moe_gu': nrm(ks[23], (N_MOE, N_EXPERTS, D, 2 * EXPERT_DIM), D ** -0.5),
        'w_moe_dn': nrm(ks[24], (N_MOE, N_EXPERTS, EXPERT_DIM, D), EXPERT_DIM ** -0.5),
        'norm_final': gain(ks[25], (D,)),
    }


def reference(x, c, ctx, c_ctx, w_ada, b_ada, norm_mix, norm_ffn, w_in, q_norm_gqa, k_norm_gqa,
              q_lora_norm, kv_lora_norm, w_uq, w_ukv, rpb, w_o_na, w_o_gqa, w_o_mla, w_out,
              w_ffn_gu, w_ffn_dn, w_router, w_moe_gu, w_moe_dn, norm_final):
    S = x.shape[1]
    C = ctx.shape[1]
    n_rows = S // GRID_W
    pos = jnp.arange(S)
    rows = (pos // GRID_W).astype(jnp.float32)
    cols = (pos % GRID_W).astype(jnp.float32)
    silu_c = jax.nn.silu(c)
    silu_cc = jax.nn.silu(c_ctx)
    for i in range(DEPTH):
        last = i == DEPTH - 1
        mod = (silu_c @ w_ada[i] + b_ada[i])[:, None, :]
        mod_c = silu_cc @ w_ada[i] + b_ada[i]
        sh1, sc1, g1, sh2, sc2, g2 = jnp.split(mod, 6, axis=-1)
        csh1, csc1, cg1, csh2, csc2, cg2 = jnp.split(mod_c, 6, axis=-1)

        h = modulate(rms_norm(x, norm_mix[i]), sh1, sc1)
        hc = modulate(rms_norm(ctx, norm_mix[i]), csh1, csc1)
        p_kv, p_q, p_gate = split_last(h @ w_in[i], (KV_WIDTH, Q_WIDTH, GATE_WIDTH))
        p_c = hc @ (w_in[i][:, :KV_WIDTH] if last else w_in[i])
        k_na, v_na, k_g, v_g, k_m, v_m = mixer_kv(p_kv, k_norm_gqa[i], kv_lora_norm[i], w_ukv[i], rows, cols)
        k_na_c, v_na_c, k_g_c, v_g_c, k_m_c, v_m_c = mixer_kv(
            p_c[..., :KV_WIDTH], k_norm_gqa[i], kv_lora_norm[i], w_ukv[i], None, None)
        q_na, q_g, q_m = mixer_q(p_q, q_norm_gqa[i], q_lora_norm[i], w_uq[i], rows, cols)
        y_na = neighbourhood_attention(q_na, k_na, v_na, k_na_c, v_na_c, rpb[i], n_rows)
        y_gqa = blocked_attention(q_g, k_g, v_g, k_g_c, v_g_c, GQA_SCALE)
        y_mla = blocked_attention(q_m[:, :, :, None], k_m, v_m, k_m_c, v_m_c, MLA_SCALE)
        x = x + g1 * merge_branches(y_na, y_gqa, y_mla, p_gate, w_o_na[i], w_o_gqa[i], w_o_mla[i], w_out[i])

        if not last:
            _, p_q_c, p_gate_c = split_last(p_c, (KV_WIDTH, Q_WIDTH, GATE_WIDTH))
            q_na_c, q_g_c, q_m_c = mixer_q(p_q_c, q_norm_gqa[i], q_lora_norm[i], w_uq[i], None, None)
            B = ctx.shape[0]
            yc_na = attend(q_na_c[:, :, :, None], k_na_c, v_na_c, NA_SCALE).reshape(B, C, NA_WIDTH)
            yc_gqa = attend(q_g_c, k_g_c, v_g_c, GQA_SCALE).reshape(B, C, GQA_WIDTH)
            yc_mla = attend(q_m_c[:, :, :, None], k_m_c, v_m_c, MLA_SCALE).reshape(B, C, MLA_WIDTH)
            ctx = ctx + cg1 * merge_branches(yc_na, yc_gqa, yc_mla, p_gate_c,
                                             w_o_na[i], w_o_gqa[i], w_o_mla[i], w_out[i])

        h2 = modulate(rms_norm(x, norm_ffn[i]), sh2, sc2)
        if last:
            x = x + g2 * channel_mixer(h2, i, w_ffn_gu, w_ffn_dn, w_router, w_moe_gu, w_moe_dn)
        else:
            hc2 = modulate(rms_norm(ctx, norm_ffn[i]), csh2, csc2)
            f = channel_mixer(jnp.concatenate([hc2, h2], axis=1), i, w_ffn_gu, w_ffn_dn,
                              w_router, w_moe_gu, w_moe_dn)
            ctx = ctx + cg2 * f[:, :C]
            x = x + g2 * f[:, C:]
    return rms_norm(x, norm_final)
```

```python
import functools

import numpy as np
import jax
import jax.numpy as jnp
from jax import lax
from jax.experimental import pallas as pl
from jax.experimental.pallas import tpu as pltpu

GRID_W = 64
HEAD_DIM = 64
NA_HEADS = 4
WIN_H = 8
WIN_W = 16
GQA_Q_HEADS = 8
GQA_KV_HEADS = 2
GQA_GROUP = GQA_Q_HEADS // GQA_KV_HEADS
MLA_HEADS = 4
MLA_Q_RANK = 256
MLA_KV_RANK = 128
MLA_NOPE_DIM = 64
MLA_ROPE_DIM = 32
MLA_V_DIM = 64
ROPE_THETA = 10000.0
TOP_K = 2
NORM_EPS = 1e-6
NA_SCALE = HEAD_DIM ** -0.5
GQA_SCALE = HEAD_DIM ** -0.5
MLA_SCALE = (MLA_NOPE_DIM + MLA_ROPE_DIM) ** -0.5

LANE = 128
TOKEN_TILE = 256
NA_Q_ROWS = 4
NA_K_ROWS = NA_Q_ROWS + WIN_H
KV_CHUNK = 512
MOE_ROW_BLOCK = 512
VMEM_LIMIT = 56 * 1024 * 1024
NEG_BIG = -1e30

F32 = jnp.float32
BF16 = jnp.bfloat16


def _params(sem, vmem=VMEM_LIMIT):
    return pltpu.CompilerParams(dimension_semantics=sem, vmem_limit_bytes=vmem)


def _resident(shape):
    zeros = (0,) * len(shape)
    return pl.BlockSpec(shape, lambda *_: zeros, pipeline_mode=pl.Buffered(1))


def _rms(x, n):
    ss = jnp.sum(x * x, axis=-1, keepdims=True)
    return x * lax.rsqrt(ss * (1.0 / n) + NORM_EPS)


def _dot(a, b):
    return jnp.dot(a, b, preferred_element_type=F32)


def _dot_nt(a, b):
    return lax.dot_general(a, b, (((1,), (1,)), ((), ())), preferred_element_type=F32)


def _silu(x):
    return x * jax.nn.sigmoid(x)


def _ada_kernel(c_ref, w_ref, b_ref, o_ref):
    s = _silu(c_ref[...])
    o_ref[0] = jnp.dot(s, w_ref[0], precision=lax.Precision.HIGHEST,
                       preferred_element_type=F32) + b_ref[0]


def _ada_call(c_rows, w_ada, b_ada):
    depth, d, n = w_ada.shape
    rows = c_rows.shape[0]
    tn = 1536 if n % 1536 == 0 else n
    return pl.pallas_call(
        _ada_kernel,
        out_shape=jax.ShapeDtypeStruct((depth, rows, n), F32),
        grid=(depth, n // tn),
        in_specs=[
            pl.BlockSpec((rows, d), lambda i, j: (0, 0)),
            pl.BlockSpec((1, d, tn), lambda i, j: (i, 0, j)),
            pl.BlockSpec((1, 1, tn), lambda i, j: (i, 0, j)),
        ],
        out_specs=pl.BlockSpec((1, rows, tn), lambda i, j: (i, 0, j)),
        compiler_params=_params(("arbitrary", "arbitrary")),
        name="ada_mod",
    )(c_rows, w_ada, b_ada.reshape(depth, 1, n))


def _rope(xh, cos, sa, sb, shift):
    return xh * cos + pltpu.roll(xh, LANE - shift, 1) * sa + pltpu.roll(xh, shift, 1) * sb


_C_QNA = 0
_C_KNA = _C_QNA + NA_HEADS * HEAD_DIM
_C_VNA = _C_KNA + NA_HEADS * HEAD_DIM
_C_QG = _C_VNA + NA_HEADS * HEAD_DIM
_C_KG = _C_QG + GQA_Q_HEADS * LANE
_C_VG = _C_KG + GQA_KV_HEADS * LANE
_C_CQ = _C_VG + GQA_KV_HEADS * LANE
_C_CKV = _C_CQ + MLA_Q_RANK
_C_KR = _C_CKV + MLA_KV_RANK
_C_GATE = _C_KR + LANE


def _mixin_kernel(x_ref, mod_ref, gmix_ref, w_ref, qn_ref, kn_ref, qln_ref, kvln_ref,
                  wuq_ref, wuk_ref, wuv_ref, rg_ref, rm_ref,
                  qna_ref, kna_ref, vna_ref, qg_ref, kg_ref, vg_ref, qm_ref, km_ref, vm_ref,
                  gate_ref):
    x = x_ref[0]
    d = x.shape[-1]
    tm = x.shape[0]
    mod = mod_ref[0]
    h = _rms(x, d) * gmix_ref[...]
    h = h * (1.0 + mod[1:2]) + mod[0:1]
    hb = h.astype(BF16)

    def proj(lo, width):
        return _dot(hb, w_ref[:, lo:lo + width])

    nw = NA_HEADS * HEAD_DIM
    qna_ref[0] = (proj(_C_QNA, nw) * NA_SCALE).astype(BF16)
    kna_ref[0] = proj(_C_KNA, nw).astype(BF16)
    vna_ref[0] = proj(_C_VNA, nw).astype(BF16)

    ones_pad = (lax.broadcasted_iota(jnp.int32, (tm, LANE), 1) >= HEAD_DIM).astype(F32)
    cos_g, sa_g, sb_g = rg_ref[0], rg_ref[1], rg_ref[2]
    cos_m, sa_m, sb_m = rm_ref[0], rm_ref[1], rm_ref[2]
    g_half = HEAD_DIM // 2
    m_half = MLA_ROPE_DIM // 2

    for hh in range(GQA_Q_HEADS):
        xh = _rms(proj(_C_QG + hh * LANE, LANE), HEAD_DIM) * qn_ref[...]
        xh = _rope(xh, cos_g, sa_g, sb_g, g_half) * GQA_SCALE
        qg_ref[0, :, hh * LANE:(hh + 1) * LANE] = xh.astype(BF16)
    for hh in range(GQA_KV_HEADS):
        xh = _rms(proj(_C_KG + hh * LANE, LANE), HEAD_DIM) * kn_ref[...]
        xh = _rope(xh, cos_g, sa_g, sb_g, g_half)
        kg_ref[0, :, hh * LANE:(hh + 1) * LANE] = xh.astype(BF16)
        vh = proj(_C_VG + hh * LANE, LANE) + ones_pad
        vg_ref[0, :, hh * LANE:(hh + 1) * LANE] = vh.astype(BF16)

    cq = (_rms(proj(_C_CQ, MLA_Q_RANK), MLA_Q_RANK) * qln_ref[...]).astype(BF16)
    q_up = _dot(cq, wuq_ref[...])
    ckv = (_rms(proj(_C_CKV, MLA_KV_RANK), MLA_KV_RANK) * kvln_ref[...]).astype(BF16)
    k_up = _dot(ckv, wuk_ref[...])
    v_up = _dot(ckv, wuv_ref[...])
    k_rope = _rope(proj(_C_KR, LANE), cos_m, sa_m, sb_m, m_half)
    for hh in range(MLA_HEADS):
        sl = slice(hh * LANE, (hh + 1) * LANE)
        qh = _rope(q_up[:, sl], cos_m, sa_m, sb_m, m_half) * MLA_SCALE
        qm_ref[0, :, sl] = qh.astype(BF16)
        km_ref[0, :, sl] = (k_up[:, sl] + k_rope).astype(BF16)
        vm_ref[0, :, sl] = (v_up[:, sl] + ones_pad).astype(BF16)

    for j in range(3):
        gate_ref[0, :, j * d:(j + 1) * d] = jax.nn.sigmoid(proj(_C_GATE + j * d, d)).astype(BF16)


def _mixin_call(xc, mod, gmix, w_cat, qn, kn, qln, kvln, wuq, wuk, wuv, rope_g, rope_m):
    b, t, d = xc.shape
    tm = TOKEN_TILE
    nt = t // tm
    tok = lambda w: pl.BlockSpec((1, tm, w), lambda ti, bi: (bi, ti, 0))
    widths = [NA_HEADS * HEAD_DIM] * 3 + [GQA_Q_HEADS * LANE, GQA_KV_HEADS * LANE,
                                          GQA_KV_HEADS * LANE, MLA_HEADS * LANE,
                                          MLA_HEADS * LANE, MLA_HEADS * LANE, 3 * d]
    return pl.pallas_call(
        _mixin_kernel,
        out_shape=[jax.ShapeDtypeStruct((b, t, w), BF16) for w in widths],
        grid=(nt, b),
        in_specs=[
            tok(d),
            pl.BlockSpec((1, 6, d), lambda ti, bi: (2 * bi + (ti == nt - 1).astype(jnp.int32), 0, 0)),
            _resident((1, d)),
            _resident(w_cat.shape),
            _resident((1, LANE)), _resident((1, LANE)),
            _resident((1, MLA_Q_RANK)), _resident((1, MLA_KV_RANK)),
            _resident(wuq.shape), _resident(wuk.shape), _resident(wuv.shape),
            pl.BlockSpec((3, tm, LANE), lambda ti, bi: (0, ti, 0)),
            pl.BlockSpec((3, tm, LANE), lambda ti, bi: (0, ti, 0)),
        ],
        out_specs=[tok(w) for w in widths],
        compiler_params=_params(("arbitrary", "arbitrary")),
        name="mixer_in",
    )(xc, mod, gmix, w_cat, qn, kn, qln, kvln, wuq, wuk, wuv, rope_g, rope_m)


def _na_kernel(q_ref, k_ref, v_ref, bias_ref, o_ref, *, n_groups, n_rows, s_len, c_len):
    g = pl.program_id(1)
    qn = NA_Q_ROWS * GRID_W
    kn = NA_K_ROWS * GRID_W
    lane = lax.broadcasted_iota(jnp.int32, (qn, LANE), 1)
    low = lane < HEAD_DIM

    def attend_pairs(k_nb, v_nb):
        k_c = k_ref[0, s_len:s_len + c_len, :]
        v_c = v_ref[0, s_len:s_len + c_len, :]
        for pair in range(NA_HEADS // 2):
            sl = slice(pair * LANE, (pair + 1) * LANE)
            qp = q_ref[0, :, sl]
            outs = []
            for sub in range(2):
                head = 2 * pair + sub
                qm = jnp.where(low if sub == 0 else jnp.logical_not(low), qp, jnp.zeros_like(qp))
                s_c = _dot_nt(qm, k_c[:, sl])
                m = jnp.max(s_c, axis=-1, keepdims=True)
                if k_nb is not None:
                    s_nb = _dot_nt(qm, k_nb[:, sl]) + bias_ref[0, head]
                    m = jnp.maximum(m, jnp.max(s_nb, axis=-1, keepdims=True))
                p_c = jnp.exp(s_c - m)
                den = jnp.sum(p_c, axis=-1, keepdims=True)
                o = _dot(p_c.astype(BF16), v_c[:, sl])
                if k_nb is not None:
                    p_nb = jnp.exp(s_nb - m)
                    den = den + jnp.sum(p_nb, axis=-1, keepdims=True)
                    o = o + _dot(p_nb.astype(BF16), v_nb[:, sl])
                outs.append(o / den)
            o_ref[0, :, sl] = jnp.where(low, outs[0], outs[1]).astype(BF16)

    @pl.when(g < n_groups)
    def _():
        first_row = jnp.clip(NA_Q_ROWS * g - WIN_H // 2, 0, n_rows - NA_K_ROWS)
        start = pl.multiple_of(first_row * GRID_W, GRID_W)
        attend_pairs(k_ref[0, pl.ds(start, kn), :], v_ref[0, pl.ds(start, kn), :])

    @pl.when(g == n_groups)
    def _():
        attend_pairs(None, None)


def _na_call(q, k, v, bias, s_len, c_len):
    b, t, w = q.shape
    qn = NA_Q_ROWS * GRID_W
    kn = NA_K_ROWS * GRID_W
    n_rows = s_len // GRID_W
    n_groups = s_len // qn
    assert c_len == qn and n_rows >= NA_K_ROWS

    def variant(g):
        return jnp.where(g == 0, 0, jnp.where(g >= n_groups - 1, 2, 1))

    return pl.pallas_call(
        functools.partial(_na_kernel, n_groups=n_groups, n_rows=n_rows, s_len=s_len, c_len=c_len),
        out_shape=jax.ShapeDtypeStruct((b, t, w), BF16),
        grid=(b, n_groups + 1),
        in_specs=[
            pl.BlockSpec((1, qn, w), lambda bi, g: (bi, g, 0)),
            pl.BlockSpec((1, t, w), lambda bi, g: (bi, 0, 0)),
            pl.BlockSpec((1, t, w), lambda bi, g: (bi, 0, 0)),
            pl.BlockSpec((1, NA_HEADS, qn, kn), lambda bi, g: (variant(g), 0, 0, 0)),
        ],
        out_specs=pl.BlockSpec((1, qn, w), lambda bi, g: (bi, g, 0)),
        compiler_params=_params(("arbitrary", "arbitrary")),
        name="na_attn",
    )(q, k, v, bias)


def _na_bias_tables(rpb, n_rows):
    qi = np.arange(NA_Q_ROWS)[:, None, None, None]
    qc = np.arange(GRID_W)[None, :, None, None]
    kj = np.arange(NA_K_ROWS)[None, None, :, None]
    kc = np.arange(GRID_W)[None, None, None, :]
    tables = []
    for r0, u0 in ((0, 0), (WIN_H // 2, 0), (n_rows - NA_Q_ROWS, n_rows - NA_K_ROWS)):
        qr = r0 + qi
        kr = u0 + kj
        row0 = np.clip(qr - WIN_H // 2, 0, n_rows - WIN_H)
        col0 = np.clip(qc - WIN_W // 2, 0, GRID_W - WIN_W)
        valid = (kr >= row0) & (kr < row0 + WIN_H) & (kc >= col0) & (kc < col0 + WIN_W)
        dr = np.clip(kr - qr, -(WIN_H - 1), WIN_H - 1) + WIN_H - 1
        dc = np.clip(kc - qc, -(WIN_W - 1), WIN_W - 1) + WIN_W - 1
        shape = (NA_Q_ROWS, GRID_W, NA_K_ROWS, GRID_W)
        dr = np.broadcast_to(dr, shape).reshape(NA_Q_ROWS * GRID_W, NA_K_ROWS * GRID_W)
        dc = np.broadcast_to(dc, shape).reshape(NA_Q_ROWS * GRID_W, NA_K_ROWS * GRID_W)
        valid = np.broadcast_to(valid, shape).reshape(NA_Q_ROWS * GRID_W, NA_K_ROWS * GRID_W)
        tables.append(jnp.where(valid[None], rpb[:, dr, dc].astype(F32), NEG_BIG))
    return jnp.stack(tables)


def _flash_kernel(q_ref, k_ref, v_ref, o_ref, *, n_kv, group, tq, s_len, c_len, n_lat_tiles):
    t = pl.program_id(1)
    low = lax.broadcasted_iota(jnp.int32, (tq, LANE), 1) < HEAD_DIM
    n_chunks = s_len // KV_CHUNK

    def run(with_latent):
        outs = []
        for h in range(n_kv):
            sl = slice(h * LANE, (h + 1) * LANE)
            qs = [q_ref[0, :, (h * group + g) * LANE:(h * group + g + 1) * LANE] for g in range(group)]
            q = qs[0] if group == 1 else jnp.concatenate(qs, axis=0)
            s = _dot_nt(q, k_ref[0, s_len:s_len + c_len, sl])
            m = jnp.max(s, axis=-1, keepdims=True)
            acc = _dot(jnp.exp(s - m).astype(BF16), v_ref[0, s_len:s_len + c_len, sl])
            if with_latent:
                def body(i, carry):
                    m_old, acc_old = carry
                    st = pl.multiple_of(i * KV_CHUNK, KV_CHUNK)
                    sc = _dot_nt(q, k_ref[0, pl.ds(st, KV_CHUNK), sl])
                    m_new = jnp.maximum(m_old, jnp.max(sc, axis=-1, keepdims=True))
                    p = jnp.exp(sc - m_new).astype(BF16)
                    acc_new = jnp.exp(m_old - m_new) * acc_old + _dot(p, v_ref[0, pl.ds(st, KV_CHUNK), sl])
                    return m_new, acc_new
                m, acc = lax.fori_loop(0, n_chunks, body, (m, acc))
            o = acc / pltpu.roll(acc, HEAD_DIM, 1)
            for g in range(group):
                outs.append(o[g * tq:(g + 1) * tq])
        for pair in range(len(outs) // 2):
            packed = jnp.where(low, outs[2 * pair], pltpu.roll(outs[2 * pair + 1], HEAD_DIM, 1))
            o_ref[0, :, pair * LANE:(pair + 1) * LANE] = packed.astype(BF16)

    @pl.when(t < n_lat_tiles)
    def _():
        run(True)

    @pl.when(t >= n_lat_tiles)
    def _():
        run(False)


def _flash_call(q, k, v, *, n_kv, group, tq, s_len, c_len, name):
    b, t, _ = q.shape
    n_heads = n_kv * group
    assert s_len % tq == 0 and c_len % tq == 0 and s_len % KV_CHUNK == 0
    return pl.pallas_call(
        functools.partial(_flash_kernel, n_kv=n_kv, group=group, tq=tq, s_len=s_len, c_len=c_len,
                          n_lat_tiles=s_len // tq),
        out_shape=jax.ShapeDtypeStruct((b, t, n_heads * HEAD_DIM), BF16),
        grid=(b, t // tq),
        in_specs=[
            pl.BlockSpec((1, tq, n_heads * LANE), lambda bi, ti: (bi, ti, 0)),
            pl.BlockSpec((1, t, n_kv * LANE), lambda bi, ti: (bi, 0, 0)),
            pl.BlockSpec((1, t, n_kv * LANE), lambda bi, ti: (bi, 0, 0)),
        ],
        out_specs=pl.BlockSpec((1, tq, n_heads * HEAD_DIM), lambda bi, ti: (bi, ti, 0)),
        compiler_params=_params(("arbitrary", "arbitrary")),
        name=name,
    )(q, k, v)


def _merge_kernel(x_ref, mod_ref, yna_ref, yg_ref, ym_ref, gate_ref, wna_ref, wg_ref, wm_ref,
                  wout_ref, o_ref):
    d = x_ref.shape[-1]
    m = gate_ref[0, :, 0:d].astype(F32) * _dot(yna_ref[0], wna_ref[...])
    m = m + gate_ref[0, :, d:2 * d].astype(F32) * _dot(yg_ref[0], wg_ref[...])
    m = m + gate_ref[0, :, 2 * d:3 * d].astype(F32) * _dot(ym_ref[0], wm_ref[...])
    r = _dot(m.astype(BF16), wout_ref[...])
    o_ref[0] = x_ref[0] + mod_ref[0][2:3] * r


def _mod_spec(d, nt):
    return pl.BlockSpec((1, 6, d), lambda bi, ti: (2 * bi + (ti == nt - 1).astype(jnp.int32), 0, 0))


def _merge_call(xc, mod, y_na, y_g, y_m, gate, w_na, w_g, w_m, w_out):
    b, t, d = xc.shape
    tm = TOKEN_TILE
    nt = t // tm
    tok = lambda w: pl.BlockSpec((1, tm, w), lambda bi, ti: (bi, ti, 0))
    return pl.pallas_call(
        _merge_kernel,
        out_shape=jax.ShapeDtypeStruct(xc.shape, F32),
        grid=(b, nt),
        in_specs=[tok(d), _mod_spec(d, nt), tok(y_na.shape[-1]), tok(y_g.shape[-1]),
                  tok(y_m.shape[-1]), tok(3 * d), _resident(w_na.shape), _resident(w_g.shape),
                  _resident(w_m.shape), _resident(w_out.shape)],
        out_specs=tok(d),
        input_output_aliases={0: 0},
        compiler_params=_params(("arbitrary", "arbitrary")),
        name="merge_out",
    )(xc, mod, y_na, y_g, y_m, gate, w_na, w_g, w_m, w_out)


def _norm2(x, mod, gain):
    h = _rms(x, x.shape[-1]) * gain
    return h * (1.0 + mod[4:5]) + mod[3:4]


def _ffn_kernel(x_ref, mod_ref, gn_ref, wg_ref, wu_ref, wd_ref, o_ref):
    x = x_ref[0]
    mod = mod_ref[0]
    hb = _norm2(x, mod, gn_ref[...]).astype(BF16)
    act = (_silu(_dot(hb, wg_ref[...])) * _dot(hb, wu_ref[...])).astype(BF16)
    o_ref[0] = x + mod[5:6] * _dot(act, wd_ref[...])


def _ffn_call(xc, mod, gn, w_gate, w_up, w_dn):
    b, t, d = xc.shape
    tm = TOKEN_TILE
    nt = t // tm
    tok = pl.BlockSpec((1, tm, d), lambda bi, ti: (bi, ti, 0))
    return pl.pallas_call(
        _ffn_kernel,
        out_shape=jax.ShapeDtypeStruct(xc.shape, F32),
        grid=(b, nt),
        in_specs=[tok, _mod_spec(d, nt), _resident((1, d)), _resident(w_gate.shape),
                  _resident(w_up.shape), _resident(w_dn.shape)],
        out_specs=tok,
        input_output_aliases={0: 0},
        compiler_params=_params(("arbitrary", "arbitrary")),
        name="ffn_dense",
    )(xc, mod, gn, w_gate, w_up, w_dn)


def _router_kernel(x_ref, mod_ref, gn_ref, wr_ref, h_ref, idx_ref, wt_ref, cnt_ref, carry_ref,
                   *, n_experts):
    @pl.when((pl.program_id(0) == 0) & (pl.program_id(1) == 0))
    def _():
        carry_ref[...] = jnp.zeros_like(carry_ref)

    x = x_ref[0]
    tm = x.shape[0]
    h = _norm2(x, mod_ref[0], gn_ref[...])
    h_ref[0] = h
    logits = jnp.dot(h, wr_ref[...], precision=lax.Precision.HIGHEST, preferred_element_type=F32)
    lane = lax.broadcasted_iota(jnp.int32, (tm, LANE), 1).astype(F32)
    logits = jnp.where(lane < n_experts, logits, -jnp.inf)
    m1 = jnp.max(logits, axis=-1, keepdims=True)
    i1 = jnp.min(jnp.where(logits == m1, lane, float(LANE)), axis=-1, keepdims=True)
    rest = jnp.where(lane == i1, -jnp.inf, logits)
    m2 = jnp.max(rest, axis=-1, keepdims=True)
    i2 = jnp.min(jnp.where(rest == m2, lane, float(LANE)), axis=-1, keepdims=True)
    e2 = jnp.exp(m2 - m1)
    w1 = 1.0 / (1.0 + e2)
    w2 = e2 / (1.0 + e2)
    hot1 = lane == i1
    hot2 = lane == i2
    hot = jnp.where(hot1 | hot2, 1.0, 0.0)
    rows = lax.broadcasted_iota(jnp.int32, (tm, tm), 0)
    cols = lax.broadcasted_iota(jnp.int32, (tm, tm), 1)
    below = jnp.where(rows > cols, 1.0, 0.0).astype(BF16)
    before = _dot(below, hot.astype(BF16)) + carry_ref[0:1, :]
    r1 = jnp.sum(jnp.where(hot1, before, 0.0), axis=-1, keepdims=True)
    r2 = jnp.sum(jnp.where(hot2, before, 0.0), axis=-1, keepdims=True)
    total = carry_ref[0:1, :] + jnp.sum(hot, axis=0, keepdims=True)
    carry_ref[...] = jnp.broadcast_to(total, carry_ref.shape)
    cnt_ref[...] = jnp.broadcast_to(total, cnt_ref.shape)
    packed = jnp.where(lane == 0, i1, jnp.where(lane == 1, i2, jnp.where(lane == 2, r1, jnp.where(lane == 3, r2, 0.0))))
    idx_ref[0] = packed.astype(jnp.int32)
    wt_ref[0] = jnp.where(lane == 0, w1, jnp.where(lane == 1, w2, 0.0))


def _router_call(xc, mod, gn, w_router_pad, n_experts):
    b, t, d = xc.shape
    tm = TOKEN_TILE
    nt = t // tm
    tok = lambda w: pl.BlockSpec((1, tm, w), lambda bi, ti: (bi, ti, 0))
    return pl.pallas_call(
        functools.partial(_router_kernel, n_experts=n_experts),
        out_shape=[jax.ShapeDtypeStruct((b, t, d), F32),
                   jax.ShapeDtypeStruct((b, t, LANE), jnp.int32),
                   jax.ShapeDtypeStruct((b, t, LANE), F32),
                   jax.ShapeDtypeStruct((8, LANE), F32)],
        grid=(b, nt),
        in_specs=[tok(d), _mod_spec(d, nt), _resident((1, d)), _resident(w_router_pad.shape)],
        out_specs=[tok(d), tok(LANE), tok(LANE), pl.BlockSpec((8, LANE), lambda bi, ti: (0, 0))],
        scratch_shapes=[pltpu.VMEM((8, LANE), F32)],
        compiler_params=_params(("arbitrary", "arbitrary")),
        name="moe_router",
    )(xc, mod, gn, w_router_pad)


def _row_copy(src, src_row, dst, dst_row, sem):
    return pltpu.make_async_copy(src.at[pl.ds(src_row, 1)], dst.at[pl.ds(dst_row, 1)], sem)


def _dispatch_kernel(dest_ref, h_ref, xs_in_ref, xs_ref, sem):
    del xs_in_ref
    tm = dest_ref.shape[-1] // TOP_K
    base = pl.program_id(0) * tm

    def issue(i, carry):
        for k in range(TOP_K):
            _row_copy(h_ref, base + i, xs_ref, dest_ref[0, 0, TOP_K * i + k], sem).start()
        return carry

    lax.fori_loop(0, tm, issue, 0, unroll=8)

    def drain(i, carry):
        for k in range(TOP_K):
            _row_copy(h_ref, 0, xs_ref, 0, sem).wait()
        return carry

    lax.fori_loop(0, tm, drain, 0, unroll=8)


def _dispatch_call(dest_blocks, h_rows, xs_zero):
    n_tiles = dest_blocks.shape[0]
    return pl.pallas_call(
        _dispatch_kernel,
        out_shape=jax.ShapeDtypeStruct(xs_zero.shape, F32),
        grid=(n_tiles,),
        in_specs=[
            pl.BlockSpec((1, 1, dest_blocks.shape[-1]), lambda i: (i, 0, 0), memory_space=pltpu.SMEM),
            pl.BlockSpec(memory_space=pl.ANY),
            pl.BlockSpec(memory_space=pl.ANY),
        ],
        out_specs=pl.BlockSpec(memory_space=pl.ANY),
        scratch_shapes=[pltpu.SemaphoreType.DMA],
        input_output_aliases={2: 0},
        compiler_params=_params(("arbitrary",)),
        name="moe_dispatch",
    )(dest_blocks, h_rows, xs_zero)


def _moe_kernel(be_ref, nu_ref, xs_ref, wg_ref, wu_ref, wd_ref, y_ref):
    i = pl.program_id(0)
    f = pl.program_id(1)

    @pl.when(i < nu_ref[0])
    def _():
        xb = xs_ref[...].astype(BF16)
        act = (_silu(_dot(xb, wg_ref[0])) * _dot(xb, wu_ref[0])).astype(BF16)
        part = _dot(act, wd_ref[0])

        @pl.when(f == 0)
        def _():
            y_ref[...] = part

        @pl.when(f > 0)
        def _():
            y_ref[...] += part

    @pl.when((i >= nu_ref[0]) & (f == 0))
    def _():
        y_ref[...] = jnp.zeros_like(y_ref)


def _moe_call(block_expert, n_used, xs, w_gu, w_dn):
    rows, d = xs.shape
    n_exp, _, two_f = w_gu.shape
    f_dim = two_f // 2
    tf = f_dim // 2 if (f_dim // 2) % LANE == 0 else f_dim
    nf = f_dim // tf
    tmr = MOE_ROW_BLOCK
    grid_spec = pltpu.PrefetchScalarGridSpec(
        num_scalar_prefetch=2,
        grid=(rows // tmr, nf),
        in_specs=[
            pl.BlockSpec((tmr, d), lambda i, f, be, nu: (i, 0)),
            pl.BlockSpec((1, d, tf), lambda i, f, be, nu: (be[i], 0, f)),
            pl.BlockSpec((1, d, tf), lambda i, f, be, nu: (be[i], 0, f + nf)),
            pl.BlockSpec((1, tf, d), lambda i, f, be, nu: (be[i], f, 0)),
        ],
        out_specs=pl.BlockSpec((tmr, d), lambda i, f, be, nu: (i, 0)),
    )
    return pl.pallas_call(
        _moe_kernel,
        out_shape=jax.ShapeDtypeStruct((rows, d), F32),
        grid_spec=grid_spec,
        compiler_params=_params(("arbitrary", "arbitrary")),
        name="moe_experts",
    )(block_expert, n_used, xs, w_gu, w_gu, w_dn)


def _combine_kernel(dest_ref, x_ref, mod_ref, wt_ref, y_ref, o_ref, buf1, buf2, sem):
    tm = x_ref.shape[1]

    def issue(i, carry):
        _row_copy(y_ref, dest_ref[0, 0, TOP_K * i], buf1, i, sem).start()
        _row_copy(y_ref, dest_ref[0, 0, TOP_K * i + 1], buf2, i, sem).start()
        return carry

    lax.fori_loop(0, tm, issue, 0, unroll=8)

    def drain(i, carry):
        _row_copy(y_ref, 0, buf1, 0, sem).wait()
        _row_copy(y_ref, 0, buf2, 0, sem).wait()
        return carry

    lax.fori_loop(0, tm, drain, 0, unroll=8)
    wt = wt_ref[0]
    mix = wt[:, 0:1] * buf1[...] + wt[:, 1:2] * buf2[...]
    o_ref[0] = x_ref[0] + mod_ref[0][5:6] * mix


def _combine_call(dest_blocks, xc, mod, wts, y_rows):
    b, t, d = xc.shape
    tm = TOKEN_TILE
    nt = t // tm
    tok = lambda w: pl.BlockSpec((1, tm, w), lambda bi, ti: (bi, ti, 0))
    return pl.pallas_call(
        _combine_kernel,
        out_shape=jax.ShapeDtypeStruct(xc.shape, F32),
        grid=(b, nt),
        in_specs=[
            pl.BlockSpec((1, 1, dest_blocks.shape[-1]), lambda bi, ti: (bi * nt + ti, 0, 0),
                         memory_space=pltpu.SMEM),
            tok(d), _mod_spec(d, nt), tok(LANE),
            pl.BlockSpec(memory_space=pl.ANY),
        ],
        out_specs=tok(d),
        scratch_shapes=[pltpu.VMEM((tm, d), F32), pltpu.VMEM((tm, d), F32), pltpu.SemaphoreType.DMA],
        input_output_aliases={1: 0},
        compiler_params=_params(("arbitrary", "arbitrary")),
        name="moe_combine",
    )(dest_blocks, xc, mod, wts, y_rows)


def _final_kernel(x_ref, g_ref, o_ref):
    x = x_ref[0]
    o_ref[0] = _rms(x, x.shape[-1]) * g_ref[...]


def _final_call(xc, gain, s_len):
    b, t, d = xc.shape
    tm = TOKEN_TILE
    tok = pl.BlockSpec((1, tm, d), lambda bi, ti: (bi, ti, 0))
    return pl.pallas_call(
        _final_kernel,
        out_shape=jax.ShapeDtypeStruct((b, s_len, d), F32),
        grid=(b, s_len // tm),
        in_specs=[tok, _resident((1, d))],
        out_specs=tok,
        compiler_params=_params(("arbitrary", "arbitrary")),
        name="final_norm",
    )(xc, gain)


def _pad_heads(w, n_heads, width):
    k = w.shape[0]
    w = w.reshape(k, n_heads, width)
    return jnp.pad(w, ((0, 0), (0, 0), (0, LANE - width))).reshape(k, n_heads * LANE)


def _layer_weights(w_in, w_uq, w_ukv):
    d = w_in.shape[0]
    nw = NA_HEADS * HEAD_DIM
    gk = GQA_KV_HEADS * HEAD_DIM
    o = 0
    k_na = w_in[:, o:o + nw]; o += nw
    v_na = w_in[:, o:o + nw]; o += nw
    k_g = w_in[:, o:o + gk]; o += gk
    v_g = w_in[:, o:o + gk]; o += gk
    c_kv = w_in[:, o:o + MLA_KV_RANK]; o += MLA_KV_RANK
    k_r = w_in[:, o:o + MLA_ROPE_DIM]; o += MLA_ROPE_DIM
    q_na = w_in[:, o:o + nw]; o += nw
    q_g = w_in[:, o:o + GQA_Q_HEADS * HEAD_DIM]; o += GQA_Q_HEADS * HEAD_DIM
    c_q = w_in[:, o:o + MLA_Q_RANK]; o += MLA_Q_RANK
    gate = w_in[:, o:]
    k_r_pad = jnp.pad(k_r, ((0, 0), (MLA_NOPE_DIM, LANE - MLA_NOPE_DIM - MLA_ROPE_DIM)))
    w_cat = jnp.concatenate([
        q_na, k_na, v_na, _pad_heads(q_g, GQA_Q_HEADS, HEAD_DIM), _pad_heads(k_g, GQA_KV_HEADS, HEAD_DIM),
        _pad_heads(v_g, GQA_KV_HEADS, HEAD_DIM), c_q, c_kv, k_r_pad, gate], axis=1).astype(BF16)
    assert w_cat.shape[1] == _C_GATE + 3 * d
    wuq = _pad_heads(w_uq, MLA_HEADS, MLA_NOPE_DIM + MLA_ROPE_DIM).astype(BF16)
    kv = w_ukv.reshape(MLA_KV_RANK, MLA_HEADS, MLA_NOPE_DIM + MLA_V_DIM)
    wuk = _pad_heads(kv[:, :, :MLA_NOPE_DIM].reshape(MLA_KV_RANK, -1), MLA_HEADS, MLA_NOPE_DIM).astype(BF16)
    wuv = _pad_heads(kv[:, :, MLA_NOPE_DIM:].reshape(MLA_KV_RANK, -1), MLA_HEADS, MLA_V_DIM).astype(BF16)
    return w_cat, wuq, wuk, wuv


def _rope_tables(s_len, c_len, rot_dim, lane_off):
    half = rot_dim // 2
    n_freq = half // 2
    pos = jnp.arange(s_len)
    rows = (pos // GRID_W).astype(F32)
    cols = (pos % GRID_W).astype(F32)
    inv = jnp.power(ROPE_THETA, -jnp.arange(n_freq, dtype=F32) / n_freq)
    ang = jnp.concatenate([rows[:, None] * inv, cols[:, None] * inv], axis=-1)
    cos, sin = jnp.cos(ang), jnp.sin(ang)
    zeros = jnp.zeros((s_len, half), F32)
    right = LANE - lane_off - rot_dim
    pad = lambda a, b: jnp.pad(jnp.concatenate([a, b], axis=-1), ((0, 0), (lane_off, right)))
    cos_t = pad(cos, cos) + jnp.pad(jnp.ones((s_len, lane_off), F32), ((0, 0), (0, LANE - lane_off)))
    sa_t = pad(-sin, zeros)
    sb_t = pad(zeros, sin)
    ident = jnp.pad(jnp.ones((c_len, lane_off + rot_dim), F32), ((0, 0), (0, right)))
    zc = jnp.zeros((c_len, LANE), F32)
    return jnp.stack([jnp.concatenate([cos_t, ident]), jnp.concatenate([sa_t, zc]),
                      jnp.concatenate([sb_t, zc])])


def _pad_lane(v):
    return jnp.pad(v, (0, LANE - v.shape[0])).reshape(1, LANE)


def kernel(x, c, ctx, c_ctx, w_ada, b_ada, norm_mix, norm_ffn, w_in, q_norm_gqa, k_norm_gqa,
           q_lora_norm, kv_lora_norm, w_uq, w_ukv, rpb, w_o_na, w_o_gqa, w_o_mla, w_out,
           w_ffn_gu, w_ffn_dn, w_router, w_moe_gu, w_moe_dn, norm_final):
    b, s_len, d = x.shape
    c_len = ctx.shape[1]
    t_len = s_len + c_len
    depth = w_ada.shape[0]
    n_rows = s_len // GRID_W
    n_tok = b * t_len
    n_tiles = n_tok // TOKEN_TILE

    c_rows = jnp.zeros((16, d), F32).at[:b].set(c).at[b].set(c_ctx)
    mods = _ada_call(c_rows, w_ada, b_ada)
    rope_g = _rope_tables(s_len, c_len, HEAD_DIM, 0)
    rope_m = _rope_tables(s_len, c_len, MLA_ROPE_DIM, MLA_NOPE_DIM)

    xc = jnp.concatenate([x, ctx], axis=1)
    for i in range(depth):
        lat = mods[i, :b].reshape(b, 1, 6, d)
        cm = jnp.broadcast_to(mods[i, b].reshape(1, 1, 6, d), (b, 1, 6, d))
        mod = jnp.concatenate([lat, cm], axis=1).reshape(2 * b, 6, d)

        w_cat, wuq, wuk, wuv = _layer_weights(w_in[i], w_uq[i], w_ukv[i])
        q_na, k_na, v_na, q_g, k_g, v_g, q_m, k_m, v_m, gate = _mixin_call(
            xc, mod, norm_mix[i].reshape(1, d), w_cat, _pad_lane(q_norm_gqa[i]), _pad_lane(k_norm_gqa[i]),
            q_lora_norm[i].reshape(1, -1), kv_lora_norm[i].reshape(1, -1), wuq, wuk, wuv, rope_g, rope_m)

        y_na = _na_call(q_na, k_na, v_na, _na_bias_tables(rpb[i], n_rows), s_len, c_len)
        y_g = _flash_call(q_g, k_g, v_g, n_kv=GQA_KV_HEADS, group=GQA_GROUP, tq=128,
                          s_len=s_len, c_len=c_len, name="gqa_attn")
        y_m = _flash_call(q_m, k_m, v_m, n_kv=MLA_HEADS, group=1, tq=256,
                          s_len=s_len, c_len=c_len, name="mla_attn")
        xc = _merge_call(xc, mod, y_na, y_g, y_m, gate, w_o_na[i].astype(BF16), w_o_gqa[i].astype(BF16),
                         w_o_mla[i].astype(BF16), w_out[i].astype(BF16))

        j = i // 2
        gn = norm_ffn[i].reshape(1, d)
        if i % 2 == 0:
            f_dim = w_ffn_dn.shape[1]
            xc = _ffn_call(xc, mod, gn, w_ffn_gu[j][:, :f_dim].astype(BF16),
                           w_ffn_gu[j][:, f_dim:].astype(BF16), w_ffn_dn[j].astype(BF16))
        else:
            n_exp = w_router.shape[-1]
            wr = jnp.pad(w_router[j], ((0, 0), (0, LANE - n_exp)))
            h2, idx, wts, cnt = _router_call(xc, mod, gn, wr, n_exp)
            counts = cnt[0, :n_exp].astype(jnp.int32)
            padded = (counts + MOE_ROW_BLOCK - 1) // MOE_ROW_BLOCK * MOE_ROW_BLOCK
            pad_end = jnp.cumsum(padded)
            pad_start = pad_end - padded
            idx = idx.reshape(n_tok, LANE)
            dest = pad_start[idx[:, 0:TOP_K]] + idx[:, TOP_K:2 * TOP_K]
            dest_blocks = dest.reshape(n_tiles, 1, TOP_K * TOKEN_TILE)
            n_blocks = -(-(n_tok * TOP_K + n_exp * (MOE_ROW_BLOCK - 1)) // MOE_ROW_BLOCK)
            starts = jnp.arange(n_blocks) * MOE_ROW_BLOCK
            block_expert = jnp.minimum(jnp.sum(starts[:, None] >= pad_end[None, :], axis=-1),
                                       n_exp - 1).astype(jnp.int32)
            n_used = (pad_end[-1] // MOE_ROW_BLOCK).astype(jnp.int32).reshape(1)
            xs = _dispatch_call(dest_blocks, h2.reshape(n_tok, d),
                                jnp.zeros((n_blocks * MOE_ROW_BLOCK, d), F32))
            y_rows = _moe_call(block_expert, n_used, xs, w_moe_gu[j].astype(BF16), w_moe_dn[j].astype(BF16))
            xc = _combine_call(dest_blocks, xc, mod, wts, y_rows)
    return _final_call(xc, norm_final.reshape(1, d), s_len)
```

```python
import functools

import numpy as np
import jax
import jax.numpy as jnp
from jax import lax
from jax.experimental import pallas as pl
from jax.experimental.pallas import tpu as pltpu

GRID_W = 64
HEAD_DIM = 64
NA_HEADS = 4
WIN_H = 8
WIN_W = 16
GQA_Q_HEADS = 8
GQA_KV_HEADS = 2
GQA_GROUP = GQA_Q_HEADS // GQA_KV_HEADS
MLA_HEADS = 4
MLA_Q_RANK = 256
MLA_KV_RANK = 128
MLA_NOPE_DIM = 64
MLA_ROPE_DIM = 32
MLA_V_DIM = 64
ROPE_THETA = 10000.0
TOP_K = 2
NORM_EPS = 1e-6
NA_SCALE = HEAD_DIM ** -0.5
GQA_SCALE = HEAD_DIM ** -0.5
MLA_SCALE = (MLA_NOPE_DIM + MLA_ROPE_DIM) ** -0.5
LOG2_E = 1.4426950408889634

LANE = 128
TOKEN_TILE = 256
NA_Q_ROWS = 4
NA_K_ROWS = NA_Q_ROWS + WIN_H
KV_CHUNK = 512
MOE_ROW_BLOCK = 512
VMEM_LIMIT = 56 * 1024 * 1024
NEG_BIG = -1e30

F32 = jnp.float32
BF16 = jnp.bfloat16


def _params(sem, vmem=VMEM_LIMIT):
    return pltpu.CompilerParams(dimension_semantics=sem, vmem_limit_bytes=vmem)


def _resident(shape):
    zeros = (0,) * len(shape)
    return pl.BlockSpec(shape, lambda *_: zeros, pipeline_mode=pl.Buffered(1))


def _rms(x, n):
    ss = jnp.sum(x * x, axis=-1, keepdims=True)
    return x * lax.rsqrt(ss * (1.0 / n) + NORM_EPS)


def _dot(a, b):
    return jnp.dot(a, b, preferred_element_type=F32)


def _dot_nt(a, b):
    return lax.dot_general(a, b, (((1,), (1,)), ((), ())), preferred_element_type=F32)


def _silu(x):
    return x * jax.nn.sigmoid(x)


def _ada_kernel(c_ref, w_ref, b_ref, o_ref):
    s = _silu(c_ref[...])
    o_ref[0] = jnp.dot(s, w_ref[0], precision=lax.Precision.HIGHEST,
                       preferred_element_type=F32) + b_ref[0]


def _ada_call(c_rows, w_ada, b_ada):
    depth, d, n = w_ada.shape
    rows = c_rows.shape[0]
    tn = 1536 if n % 1536 == 0 else n
    return pl.pallas_call(
        _ada_kernel,
        out_shape=jax.ShapeDtypeStruct((depth, rows, n), F32),
        grid=(depth, n // tn),
        in_specs=[
            pl.BlockSpec((rows, d), lambda i, j: (0, 0)),
            pl.BlockSpec((1, d, tn), lambda i, j: (i, 0, j)),
            pl.BlockSpec((1, 1, tn), lambda i, j: (i, 0, j)),
        ],
        out_specs=pl.BlockSpec((1, rows, tn), lambda i, j: (i, 0, j)),
        compiler_params=_params(("arbitrary", "arbitrary")),
        name="ada_mod",
    )(c_rows, w_ada, b_ada.reshape(depth, 1, n))


def _rope(xh, cos, sa, sb, shift):
    return xh * cos + pltpu.roll(xh, LANE - shift, 1) * sa + pltpu.roll(xh, shift, 1) * sb


_C_QNA = 0
_C_KNA = _C_QNA + NA_HEADS * HEAD_DIM
_C_VNA = _C_KNA + NA_HEADS * HEAD_DIM
_C_QG = _C_VNA + NA_HEADS * HEAD_DIM
_C_KG = _C_QG + GQA_Q_HEADS * LANE
_C_VG = _C_KG + GQA_KV_HEADS * LANE
_C_CQ = _C_VG + GQA_KV_HEADS * LANE
_C_CKV = _C_CQ + MLA_Q_RANK
_C_KR = _C_CKV + MLA_KV_RANK
_C_GATE = _C_KR + LANE


def _mixin_kernel(x_ref, mod_ref, gmix_ref, w_ref, qn_ref, kn_ref, qln_ref, kvln_ref,
                  wuq_ref, wuk_ref, wuv_ref, rg_ref, rm_ref,
                  qna_ref, kna_ref, vna_ref, qg_ref, kg_ref, vg_ref, qm_ref, km_ref, vm_ref,
                  gate_ref):
    x = x_ref[0]
    d = x.shape[-1]
    tm = x.shape[0]
    mod = mod_ref[0]
    h = _rms(x, d) * gmix_ref[...]
    h = h * (1.0 + mod[1:2]) + mod[0:1]
    hb = h.astype(BF16)

    def proj(lo, width):
        return _dot(hb, w_ref[:, lo:lo + width])

    nw = NA_HEADS * HEAD_DIM
    qna_ref[0] = (proj(_C_QNA, nw) * (NA_SCALE * LOG2_E)).astype(BF16)
    kna_ref[0] = proj(_C_KNA, nw).astype(BF16)
    vna_ref[0] = proj(_C_VNA, nw).astype(BF16)

    ones_pad = (lax.broadcasted_iota(jnp.int32, (tm, LANE), 1) >= HEAD_DIM).astype(F32)
    cos_g, sa_g, sb_g = rg_ref[0], rg_ref[1], rg_ref[2]
    cos_m, sa_m, sb_m = rm_ref[0], rm_ref[1], rm_ref[2]
    g_half = HEAD_DIM // 2
    m_half = MLA_ROPE_DIM // 2

    for hh in range(GQA_Q_HEADS):
        xh = _rms(proj(_C_QG + hh * LANE, LANE), HEAD_DIM) * qn_ref[...]
        xh = _rope(xh, cos_g, sa_g, sb_g, g_half) * (GQA_SCALE * LOG2_E)
        qg_ref[0, :, hh * LANE:(hh + 1) * LANE] = xh.astype(BF16)
    for hh in range(GQA_KV_HEADS):
        xh = _rms(proj(_C_KG + hh * LANE, LANE), HEAD_DIM) * kn_ref[...]
        xh = _rope(xh, cos_g, sa_g, sb_g, g_half)
        kg_ref[0, :, hh * LANE:(hh + 1) * LANE] = xh.astype(BF16)
        vh = proj(_C_VG + hh * LANE, LANE) + ones_pad
        vg_ref[0, :, hh * LANE:(hh + 1) * LANE] = vh.astype(BF16)

    cq = (_rms(proj(_C_CQ, MLA_Q_RANK), MLA_Q_RANK) * qln_ref[...]).astype(BF16)
    q_up = _dot(cq, wuq_ref[...])
    ckv = (_rms(proj(_C_CKV, MLA_KV_RANK), MLA_KV_RANK) * kvln_ref[...]).astype(BF16)
    k_up = _dot(ckv, wuk_ref[...])
    v_up = _dot(ckv, wuv_ref[...])
    k_rope = _rope(proj(_C_KR, LANE), cos_m, sa_m, sb_m, m_half)
    for hh in range(MLA_HEADS):
        sl = slice(hh * LANE, (hh + 1) * LANE)
        qh = _rope(q_up[:, sl], cos_m, sa_m, sb_m, m_half) * (MLA_SCALE * LOG2_E)
        qm_ref[0, :, sl] = qh.astype(BF16)
        km_ref[0, :, sl] = (k_up[:, sl] + k_rope).astype(BF16)
        vm_ref[0, :, sl] = (v_up[:, sl] + ones_pad).astype(BF16)

    for j in range(3):
        gate_ref[0, :, j * d:(j + 1) * d] = jax.nn.sigmoid(proj(_C_GATE + j * d, d)).astype(BF16)


def _mixin_call(xc, mod, gmix, w_cat, qn, kn, qln, kvln, wuq, wuk, wuv, rope_g, rope_m):
    b, t, d = xc.shape
    tm = TOKEN_TILE
    nt = t // tm
    tok = lambda w: pl.BlockSpec((1, tm, w), lambda ti, bi: (bi, ti, 0))
    widths = [NA_HEADS * HEAD_DIM] * 3 + [GQA_Q_HEADS * LANE, GQA_KV_HEADS * LANE,
                                          GQA_KV_HEADS * LANE, MLA_HEADS * LANE,
                                          MLA_HEADS * LANE, MLA_HEADS * LANE, 3 * d]
    return pl.pallas_call(
        _mixin_kernel,
        out_shape=[jax.ShapeDtypeStruct((b, t, w), BF16) for w in widths],
        grid=(nt, b),
        in_specs=[
            tok(d),
            pl.BlockSpec((1, 6, d), lambda ti, bi: (2 * bi + (ti == nt - 1).astype(jnp.int32), 0, 0)),
            _resident((1, d)),
            _resident(w_cat.shape),
            _resident((1, LANE)), _resident((1, LANE)),
            _resident((1, MLA_Q_RANK)), _resident((1, MLA_KV_RANK)),
            _resident(wuq.shape), _resident(wuk.shape), _resident(wuv.shape),
            pl.BlockSpec((3, tm, LANE), lambda ti, bi: (0, ti, 0)),
            pl.BlockSpec((3, tm, LANE), lambda ti, bi: (0, ti, 0)),
        ],
        out_specs=[tok(w) for w in widths],
        compiler_params=_params(("arbitrary", "arbitrary")),
        name="mixer_in",
    )(xc, mod, gmix, w_cat, qn, kn, qln, kvln, wuq, wuk, wuv, rope_g, rope_m)


def _na_kernel(q_ref, k_ref, v_ref, bias_ref, o_ref, *, n_groups, n_rows, s_len, c_len):
    g = pl.program_id(1)
    qn = NA_Q_ROWS * GRID_W
    kn = NA_K_ROWS * GRID_W
    lane = lax.broadcasted_iota(jnp.int32, (qn, LANE), 1)
    low = lane < HEAD_DIM

    def attend_pairs(k_nb, v_nb):
        k_c = k_ref[0, s_len:s_len + c_len, :]
        v_c = v_ref[0, s_len:s_len + c_len, :]
        for pair in range(NA_HEADS // 2):
            sl = slice(pair * LANE, (pair + 1) * LANE)
            qp = q_ref[0, :, sl]
            outs = []
            for sub in range(2):
                head = 2 * pair + sub
                qm = jnp.where(low if sub == 0 else jnp.logical_not(low), qp, jnp.zeros_like(qp))
                s_c = _dot_nt(qm, k_c[:, sl])
                m = jnp.max(s_c, axis=-1, keepdims=True)
                if k_nb is not None:
                    s_nb = _dot_nt(qm, k_nb[:, sl]) + bias_ref[0, head]
                    m = jnp.maximum(m, jnp.max(s_nb, axis=-1, keepdims=True))
                p_c = jnp.exp2(s_c - m)
                den = jnp.sum(p_c, axis=-1, keepdims=True)
                o = _dot(p_c.astype(BF16), v_c[:, sl])
                if k_nb is not None:
                    p_nb = jnp.exp2(s_nb - m)
                    den = den + jnp.sum(p_nb, axis=-1, keepdims=True)
                    o = o + _dot(p_nb.astype(BF16), v_nb[:, sl])
                outs.append(o / den)
            o_ref[0, :, sl] = jnp.where(low, outs[0], outs[1]).astype(BF16)

    @pl.when(g < n_groups)
    def _():
        first_row = jnp.clip(NA_Q_ROWS * g - WIN_H // 2, 0, n_rows - NA_K_ROWS)
        start = pl.multiple_of(first_row * GRID_W, GRID_W)
        attend_pairs(k_ref[0, pl.ds(start, kn), :], v_ref[0, pl.ds(start, kn), :])

    @pl.when(g == n_groups)
    def _():
        attend_pairs(None, None)


def _na_call(q, k, v, bias, s_len, c_len):
    b, t, w = q.shape
    qn = NA_Q_ROWS * GRID_W
    kn = NA_K_ROWS * GRID_W
    n_rows = s_len // GRID_W
    n_groups = s_len // qn
    assert c_len == qn and n_rows >= NA_K_ROWS

    def variant(g):
        return jnp.where(g == 0, 0, jnp.where(g >= n_groups - 1, 2, 1))

    return pl.pallas_call(
        functools.partial(_na_kernel, n_groups=n_groups, n_rows=n_rows, s_len=s_len, c_len=c_len),
        out_shape=jax.ShapeDtypeStruct((b, t, w), BF16),
        grid=(b, n_groups + 1),
        in_specs=[
            pl.BlockSpec((1, qn, w), lambda bi, g: (bi, g, 0)),
            pl.BlockSpec((1, t, w), lambda bi, g: (bi, 0, 0)),
            pl.BlockSpec((1, t, w), lambda bi, g: (bi, 0, 0)),
            pl.BlockSpec((1, NA_HEADS, qn, kn), lambda bi, g: (variant(g), 0, 0, 0)),
        ],
        out_specs=pl.BlockSpec((1, qn, w), lambda bi, g: (bi, g, 0)),
        compiler_params=_params(("arbitrary", "arbitrary")),
        name="na_attn",
    )(q, k, v, bias)


def _na_bias_tables(rpb, n_rows):
    qi = np.arange(NA_Q_ROWS)[:, None]
    kj = np.arange(NA_K_ROWS)[None, :]
    qc = np.arange(GRID_W)[:, None]
    kc = np.arange(GRID_W)[None, :]
    col0 = np.clip(qc - WIN_W // 2, 0, GRID_W - WIN_W)
    col_ok = (kc >= col0) & (kc < col0 + WIN_W)
    dc = np.clip(kc - qc, -(WIN_W - 1), WIN_W - 1) + WIN_W - 1
    col_sel = np.eye(2 * WIN_W - 1, dtype=np.float32)[dc]
    row_sel, valid = [], []
    for r0, u0 in ((0, 0), (WIN_H // 2, 0), (n_rows - NA_Q_ROWS, n_rows - NA_K_ROWS)):
        qr = r0 + qi
        kr = u0 + kj
        row0 = np.clip(qr - WIN_H // 2, 0, n_rows - WIN_H)
        row_ok = (kr >= row0) & (kr < row0 + WIN_H)
        dr = np.clip(kr - qr, -(WIN_H - 1), WIN_H - 1) + WIN_H - 1
        row_sel.append(np.eye(2 * WIN_H - 1, dtype=np.float32)[dr])
        valid.append(row_ok[:, None, :, None] & col_ok[None, :, None, :])
    row_sel = jnp.asarray(np.stack(row_sel))
    valid = np.stack(valid).reshape(3, 1, NA_Q_ROWS * GRID_W, NA_K_ROWS * GRID_W)
    picked = jnp.einsum('vija,lhab,qkb->lvhiqjk', row_sel, rpb.astype(F32) * LOG2_E, jnp.asarray(col_sel),
                        precision=lax.Precision.HIGHEST)
    picked = picked.reshape(rpb.shape[0], 3, NA_HEADS, NA_Q_ROWS * GRID_W, NA_K_ROWS * GRID_W)
    return jnp.where(valid[None], picked, NEG_BIG)


def _flash_kernel(q_ref, k_ref, v_ref, *rest, n_kv, group, tq, s_len, c_len, with_latent):
    o_ref = rest[-1]
    low = lax.broadcasted_iota(jnp.int32, (tq, LANE), 1) < HEAD_DIM
    heads = range(n_kv)
    lanes = [slice(h * LANE, (h + 1) * LANE) for h in heads]
    qs = []
    for h in heads:
        parts = [q_ref[0, :, (h * group + g) * LANE:(h * group + g + 1) * LANE] for g in range(group)]
        qs.append(parts[0] if group == 1 else jnp.concatenate(parts, axis=0))
    carry = []
    for h in heads:
        s = _dot_nt(qs[h], k_ref[0, s_len:s_len + c_len, lanes[h]])
        m = jnp.max(s, axis=-1, keepdims=True)
        carry += [m, _dot(jnp.exp2(s - m).astype(BF16), v_ref[0, s_len:s_len + c_len, lanes[h]])]
    if with_latent:
        def body(i, carry):
            st = pl.multiple_of(i * KV_CHUNK, KV_CHUNK)
            new = []
            for h in heads:
                m_old, acc_old = carry[2 * h], carry[2 * h + 1]
                sc = _dot_nt(qs[h], k_ref[0, pl.ds(st, KV_CHUNK), lanes[h]])
                m_new = jnp.maximum(m_old, jnp.max(sc, axis=-1, keepdims=True))
                p = jnp.exp2(sc - m_new).astype(BF16)
                pv = _dot(p, v_ref[0, pl.ds(st, KV_CHUNK), lanes[h]])
                new += [m_new, jnp.exp2(m_old - m_new) * acc_old + pv]
            return tuple(new)
        carry = lax.fori_loop(0, s_len // KV_CHUNK, body, tuple(carry), unroll=2)
    outs = []
    for h in heads:
        acc = carry[2 * h + 1]
        o = acc / pltpu.roll(acc, HEAD_DIM, 1)
        for g in range(group):
            outs.append(o[g * tq:(g + 1) * tq])
    for pair in range(len(outs) // 2):
        packed = jnp.where(low, outs[2 * pair], pltpu.roll(outs[2 * pair + 1], HEAD_DIM, 1))
        o_ref[0, :, pair * LANE:(pair + 1) * LANE] = packed.astype(BF16)


def _flash_call(q, k, v, *, n_kv, group, tq, s_len, c_len, name):
    b, t, _ = q.shape
    n_heads = n_kv * group
    tq_ctx = min(tq, c_len)
    assert s_len % tq == 0 and c_len % tq_ctx == 0 and s_len % tq_ctx == 0 and s_len % KV_CHUNK == 0
    kv_spec = pl.BlockSpec((1, t, n_kv * LANE), lambda bi, ti: (bi, 0, 0))
    common = dict(n_kv=n_kv, group=group, s_len=s_len, c_len=c_len)
    width = n_heads * HEAD_DIM
    y_lat = pl.pallas_call(
        functools.partial(_flash_kernel, tq=tq, with_latent=True, **common),
        out_shape=jax.ShapeDtypeStruct((b, s_len, width), BF16),
        grid=(b, s_len // tq),
        in_specs=[pl.BlockSpec((1, tq, n_heads * LANE), lambda bi, ti: (bi, ti, 0)), kv_spec, kv_spec],
        out_specs=pl.BlockSpec((1, tq, width), lambda bi, ti: (bi, ti, 0)),
        compiler_params=_params(("arbitrary", "arbitrary")),
        name=name,
    )(q, k, v)
    first = s_len // tq_ctx
    y_ctx = pl.pallas_call(
        functools.partial(_flash_kernel, tq=tq_ctx, with_latent=False, **common),
        out_shape=jax.ShapeDtypeStruct((b, c_len, width), BF16),
        grid=(b, c_len // tq_ctx),
        in_specs=[pl.BlockSpec((1, tq_ctx, n_heads * LANE), lambda bi, ti: (bi, first + ti, 0)), kv_spec, kv_spec],
        out_specs=pl.BlockSpec((1, tq_ctx, width), lambda bi, ti: (bi, ti, 0)),
        compiler_params=_params(("arbitrary", "arbitrary")),
        name=name + "_ctx",
    )(q, k, v)
    return y_lat, y_ctx


def _merge_kernel(x_ref, mod_ref, yna_ref, yg_ref, ygc_ref, ym_ref, ymc_ref, gate_ref, wna_ref, wg_ref,
                  wm_ref, wout_ref, o_ref, *, n_lat_tiles):
    d = x_ref.shape[-1]
    is_ctx = pl.program_id(1) >= n_lat_tiles
    y_g = jnp.where(is_ctx, ygc_ref[0], yg_ref[0])
    y_m = jnp.where(is_ctx, ymc_ref[0], ym_ref[0])
    m = gate_ref[0, :, 0:d].astype(F32) * _dot(yna_ref[0], wna_ref[...])
    m = m + gate_ref[0, :, d:2 * d].astype(F32) * _dot(y_g, wg_ref[...])
    m = m + gate_ref[0, :, 2 * d:3 * d].astype(F32) * _dot(y_m, wm_ref[...])
    r = _dot(m.astype(BF16), wout_ref[...])
    o_ref[0] = x_ref[0] + mod_ref[0][2:3] * r


def _mod_spec(d, nt):
    return pl.BlockSpec((1, 6, d), lambda bi, ti: (2 * bi + (ti == nt - 1).astype(jnp.int32), 0, 0))


def _merge_call(xc, mod, y_na, y_g, y_m, gate, w_na, w_g, w_m, w_out):
    b, t, d = xc.shape
    tm = TOKEN_TILE
    nt = t // tm
    n_lat = y_g[0].shape[1] // tm
    tok = lambda w: pl.BlockSpec((1, tm, w), lambda bi, ti: (bi, ti, 0))
    lat = lambda w: pl.BlockSpec((1, tm, w), lambda bi, ti: (bi, jnp.minimum(ti, n_lat - 1), 0))
    ctx = lambda w: pl.BlockSpec((1, tm, w), lambda bi, ti: (bi, jnp.maximum(ti - n_lat, 0), 0))
    wg_, wm_ = y_g[0].shape[-1], y_m[0].shape[-1]
    return pl.pallas_call(
        functools.partial(_merge_kernel, n_lat_tiles=n_lat),
        out_shape=jax.ShapeDtypeStruct(xc.shape, F32),
        grid=(b, nt),
        in_specs=[tok(d), _mod_spec(d, nt), tok(y_na.shape[-1]), lat(wg_), ctx(wg_), lat(wm_), ctx(wm_),
                  tok(3 * d), _resident(w_na.shape), _resident(w_g.shape),
                  _resident(w_m.shape), _resident(w_out.shape)],
        out_specs=tok(d),
        input_output_aliases={0: 0},
        compiler_params=_params(("arbitrary", "arbitrary")),
        name="merge_out",
    )(xc, mod, y_na, y_g[0], y_g[1], y_m[0], y_m[1], gate, w_na, w_g, w_m, w_out)


def _norm2(x, mod, gain):
    h = _rms(x, x.shape[-1]) * gain
    return h * (1.0 + mod[4:5]) + mod[3:4]


def _ffn_kernel(x_ref, mod_ref, gn_ref, wg_ref, wu_ref, wd_ref, o_ref):
    x = x_ref[0]
    mod = mod_ref[0]
    hb = _norm2(x, mod, gn_ref[...]).astype(BF16)
    act = (_silu(_dot(hb, wg_ref[...])) * _dot(hb, wu_ref[...])).astype(BF16)
    o_ref[0] = x + mod[5:6] * _dot(act, wd_ref[...])


def _ffn_call(xc, mod, gn, w_gate, w_up, w_dn):
    b, t, d = xc.shape
    tm = TOKEN_TILE
    nt = t // tm
    tok = pl.BlockSpec((1, tm, d), lambda bi, ti: (bi, ti, 0))
    return pl.pallas_call(
        _ffn_kernel,
        out_shape=jax.ShapeDtypeStruct(xc.shape, F32),
        grid=(b, nt),
        in_specs=[tok, _mod_spec(d, nt), _resident((1, d)), _resident(w_gate.shape),
                  _resident(w_up.shape), _resident(w_dn.shape)],
        out_specs=tok,
        input_output_aliases={0: 0},
        compiler_params=_params(("arbitrary", "arbitrary")),
        name="ffn_dense",
    )(xc, mod, gn, w_gate, w_up, w_dn)


def _router_kernel(x_ref, mod_ref, gn_ref, wr_ref, idx_ref, wt_ref, cnt_ref, carry_ref,
                   *, n_experts):
    @pl.when((pl.program_id(0) == 0) & (pl.program_id(1) == 0))
    def _():
        carry_ref[...] = jnp.zeros_like(carry_ref)

    x = x_ref[0]
    tm = x.shape[0]
    h = _norm2(x, mod_ref[0], gn_ref[...])
    logits = jnp.dot(h, wr_ref[...], precision=lax.Precision.HIGHEST, preferred_element_type=F32)
    lane = lax.broadcasted_iota(jnp.int32, (tm, LANE), 1).astype(F32)
    logits = jnp.where(lane < n_experts, logits, -jnp.inf)
    m1 = jnp.max(logits, axis=-1, keepdims=True)
    i1 = jnp.min(jnp.where(logits == m1, lane, float(LANE)), axis=-1, keepdims=True)
    rest = jnp.where(lane == i1, -jnp.inf, logits)
    m2 = jnp.max(rest, axis=-1, keepdims=True)
    i2 = jnp.min(jnp.where(rest == m2, lane, float(LANE)), axis=-1, keepdims=True)
    e2 = jnp.exp(m2 - m1)
    w1 = 1.0 / (1.0 + e2)
    w2 = e2 / (1.0 + e2)
    hot1 = lane == i1
    hot2 = lane == i2
    hot = jnp.where(hot1 | hot2, 1.0, 0.0)
    rows = lax.broadcasted_iota(jnp.int32, (tm, tm), 0)
    cols = lax.broadcasted_iota(jnp.int32, (tm, tm), 1)
    below = jnp.where(rows > cols, 1.0, 0.0).astype(BF16)
    before = _dot(below, hot.astype(BF16)) + carry_ref[0:1, :]
    r1 = jnp.sum(jnp.where(hot1, before, 0.0), axis=-1, keepdims=True)
    r2 = jnp.sum(jnp.where(hot2, before, 0.0), axis=-1, keepdims=True)
    total = carry_ref[0:1, :] + jnp.sum(hot, axis=0, keepdims=True)
    carry_ref[...] = jnp.broadcast_to(total, carry_ref.shape)
    cnt_ref[...] = jnp.broadcast_to(total, cnt_ref.shape)
    packed = jnp.where(lane == 0, i1, jnp.where(lane == 1, i2, jnp.where(lane == 2, r1, jnp.where(lane == 3, r2, 0.0))))
    idx_ref[0] = packed.astype(jnp.int32)
    wt_ref[0] = jnp.where(lane == 0, w1, jnp.where(lane == 1, w2, 0.0))


def _router_call(xc, mod, gn, w_router_pad, n_experts):
    b, t, d = xc.shape
    tm = TOKEN_TILE
    nt = t // tm
    tok = lambda w: pl.BlockSpec((1, tm, w), lambda bi, ti: (bi, ti, 0))
    return pl.pallas_call(
        functools.partial(_router_kernel, n_experts=n_experts),
        out_shape=[jax.ShapeDtypeStruct((b, t, LANE), jnp.int32),
                   jax.ShapeDtypeStruct((b, t, LANE), F32),
                   jax.ShapeDtypeStruct((8, LANE), F32)],
        grid=(b, nt),
        in_specs=[tok(d), _mod_spec(d, nt), _resident((1, d)), _resident(w_router_pad.shape)],
        out_specs=[tok(LANE), tok(LANE), pl.BlockSpec((8, LANE), lambda bi, ti: (0, 0))],
        scratch_shapes=[pltpu.VMEM((8, LANE), F32)],
        compiler_params=_params(("arbitrary", "arbitrary")),
        name="moe_router",
    )(xc, mod, gn, w_router_pad)


def _row_copy(src, src_row, dst, dst_row, sem):
    return pltpu.make_async_copy(src.at[pl.ds(src_row, 1)], dst.at[pl.ds(dst_row, 1)], sem)


def _dispatch_kernel(dest_ref, x_ref, mod_ref, gn_ref, xs_in_ref, xs_ref, h_buf, sem):
    del xs_in_ref
    tm = x_ref.shape[1]
    h_buf[...] = _norm2(x_ref[0], mod_ref[0], gn_ref[...])

    def issue(i, carry):
        for k in range(TOP_K):
            _row_copy(h_buf, i, xs_ref, dest_ref[0, 0, TOP_K * i + k], sem).start()
        return carry

    lax.fori_loop(0, tm, issue, 0, unroll=8)

    def drain(i, carry):
        for k in range(TOP_K):
            _row_copy(h_buf, 0, xs_ref, 0, sem).wait()
        return carry

    lax.fori_loop(0, tm, drain, 0, unroll=8)


def _dispatch_call(dest_blocks, xc, mod, gn, xs_zero):
    b, t, d = xc.shape
    tm = TOKEN_TILE
    nt = t // tm
    return pl.pallas_call(
        _dispatch_kernel,
        out_shape=jax.ShapeDtypeStruct(xs_zero.shape, F32),
        grid=(b, nt),
        in_specs=[
            pl.BlockSpec((1, 1, dest_blocks.shape[-1]), lambda bi, ti: (bi * nt + ti, 0, 0),
                         memory_space=pltpu.SMEM),
            pl.BlockSpec((1, tm, d), lambda bi, ti: (bi, ti, 0)),
            _mod_spec(d, nt), _resident((1, d)),
            pl.BlockSpec(memory_space=pl.ANY),
        ],
        out_specs=pl.BlockSpec(memory_space=pl.ANY),
        scratch_shapes=[pltpu.VMEM((tm, d), F32), pltpu.SemaphoreType.DMA],
        input_output_aliases={4: 0},
        compiler_params=_params(("arbitrary", "arbitrary")),
        name="moe_dispatch",
    )(dest_blocks, xc, mod, gn, xs_zero)


def _moe_kernel(be_ref, nu_ref, xs_ref, wg_ref, wu_ref, wd_ref, y_ref):
    i = pl.program_id(0)
    f = pl.program_id(1)

    @pl.when(i < nu_ref[0])
    def _():
        xb = xs_ref[...].astype(BF16)
        act = (_silu(_dot(xb, wg_ref[0])) * _dot(xb, wu_ref[0])).astype(BF16)
        part = _dot(act, wd_ref[0])

        @pl.when(f == 0)
        def _():
            y_ref[...] = part

        @pl.when(f > 0)
        def _():
            y_ref[...] += part

    @pl.when((i >= nu_ref[0]) & (f == 0))
    def _():
        y_ref[...] = jnp.zeros_like(y_ref)


def _moe_call(block_expert, n_used, xs, w_gu, w_dn):
    rows, d = xs.shape
    n_exp, _, two_f = w_gu.shape
    f_dim = two_f // 2
    tf = f_dim // 2 if (f_dim // 2) % LANE == 0 else f_dim
    nf = f_dim // tf
    tmr = MOE_ROW_BLOCK
    grid_spec = pltpu.PrefetchScalarGridSpec(
        num_scalar_prefetch=2,
        grid=(rows // tmr, nf),
        in_specs=[
            pl.BlockSpec((tmr, d), lambda i, f, be, nu: (i, 0)),
            pl.BlockSpec((1, d, tf), lambda i, f, be, nu: (be[i], 0, f)),
            pl.BlockSpec((1, d, tf), lambda i, f, be, nu: (be[i], 0, f + nf)),
            pl.BlockSpec((1, tf, d), lambda i, f, be, nu: (be[i], f, 0)),
        ],
        out_specs=pl.BlockSpec((tmr, d), lambda i, f, be, nu: (i, 0)),
    )
    return pl.pallas_call(
        _moe_kernel,
        out_shape=jax.ShapeDtypeStruct((rows, d), F32),
        grid_spec=grid_spec,
        compiler_params=_params(("arbitrary", "arbitrary")),
        name="moe_experts",
    )(block_expert, n_used, xs, w_gu, w_gu, w_dn)


def _combine_kernel(dest_ref, x_ref, mod_ref, wt_ref, y_ref, o_ref, buf1, buf2, sem):
    tm = x_ref.shape[1]

    def issue(i, carry):
        _row_copy(y_ref, dest_ref[0, 0, TOP_K * i], buf1, i, sem).start()
        _row_copy(y_ref, dest_ref[0, 0, TOP_K * i + 1], buf2, i, sem).start()
        return carry

    lax.fori_loop(0, tm, issue, 0, unroll=8)

    def drain(i, carry):
        _row_copy(y_ref, 0, buf1, 0, sem).wait()
        _row_copy(y_ref, 0, buf2, 0, sem).wait()
        return carry

    lax.fori_loop(0, tm, drain, 0, unroll=8)
    wt = wt_ref[0]
    mix = wt[:, 0:1] * buf1[...] + wt[:, 1:2] * buf2[...]
    o_ref[0] = x_ref[0] + mod_ref[0][5:6] * mix


def _combine_call(dest_blocks, xc, mod, wts, y_rows):
    b, t, d = xc.shape
    tm = TOKEN_TILE
    nt = t // tm
    tok = lambda w: pl.BlockSpec((1, tm, w), lambda bi, ti: (bi, ti, 0))
    return pl.pallas_call(
        _combine_kernel,
        out_shape=jax.ShapeDtypeStruct(xc.shape, F32),
        grid=(b, nt),
        in_specs=[
            pl.BlockSpec((1, 1, dest_blocks.shape[-1]), lambda bi, ti: (bi * nt + ti, 0, 0),
                         memory_space=pltpu.SMEM),
            tok(d), _mod_spec(d, nt), tok(LANE),
            pl.BlockSpec(memory_space=pl.ANY),
        ],
        out_specs=tok(d),
        scratch_shapes=[pltpu.VMEM((tm, d), F32), pltpu.VMEM((tm, d), F32), pltpu.SemaphoreType.DMA],
        input_output_aliases={1: 0},
        compiler_params=_params(("arbitrary", "arbitrary")),
        name="moe_combine",
    )(dest_blocks, xc, mod, wts, y_rows)


def _final_kernel(x_ref, g_ref, o_ref):
    x = x_ref[0]
    o_ref[0] = _rms(x, x.shape[-1]) * g_ref[...]


def _final_call(xc, gain, s_len):
    b, t, d = xc.shape
    tm = TOKEN_TILE
    tok = pl.BlockSpec((1, tm, d), lambda bi, ti: (bi, ti, 0))
    return pl.pallas_call(
        _final_kernel,
        out_shape=jax.ShapeDtypeStruct((b, s_len, d), F32),
        grid=(b, s_len // tm),
        in_specs=[tok, _resident((1, d))],
        out_specs=tok,
        compiler_params=_params(("arbitrary", "arbitrary")),
        name="final_norm",
    )(xc, gain)


def _pad_heads(w, n_heads, width):
    k = w.shape[0]
    w = w.reshape(k, n_heads, width)
    return jnp.pad(w, ((0, 0), (0, 0), (0, LANE - width))).reshape(k, n_heads * LANE)


def _layer_weights(w_in, w_uq, w_ukv):
    d = w_in.shape[0]
    nw = NA_HEADS * HEAD_DIM
    gk = GQA_KV_HEADS * HEAD_DIM
    o = 0
    k_na = w_in[:, o:o + nw]; o += nw
    v_na = w_in[:, o:o + nw]; o += nw
    k_g = w_in[:, o:o + gk]; o += gk
    v_g = w_in[:, o:o + gk]; o += gk
    c_kv = w_in[:, o:o + MLA_KV_RANK]; o += MLA_KV_RANK
    k_r = w_in[:, o:o + MLA_ROPE_DIM]; o += MLA_ROPE_DIM
    q_na = w_in[:, o:o + nw]; o += nw
    q_g = w_in[:, o:o + GQA_Q_HEADS * HEAD_DIM]; o += GQA_Q_HEADS * HEAD_DIM
    c_q = w_in[:, o:o + MLA_Q_RANK]; o += MLA_Q_RANK
    gate = w_in[:, o:]
    k_r_pad = jnp.pad(k_r, ((0, 0), (MLA_NOPE_DIM, LANE - MLA_NOPE_DIM - MLA_ROPE_DIM)))
    w_cat = jnp.concatenate([
        q_na, k_na, v_na, _pad_heads(q_g, GQA_Q_HEADS, HEAD_DIM), _pad_heads(k_g, GQA_KV_HEADS, HEAD_DIM),
        _pad_heads(v_g, GQA_KV_HEADS, HEAD_DIM), c_q, c_kv, k_r_pad, gate], axis=1).astype(BF16)
    assert w_cat.shape[1] == _C_GATE + 3 * d
    wuq = _pad_heads(w_uq, MLA_HEADS, MLA_NOPE_DIM + MLA_ROPE_DIM).astype(BF16)
    kv = w_ukv.reshape(MLA_KV_RANK, MLA_HEADS, MLA_NOPE_DIM + MLA_V_DIM)
    wuk = _pad_heads(kv[:, :, :MLA_NOPE_DIM].reshape(MLA_KV_RANK, -1), MLA_HEADS, MLA_NOPE_DIM).astype(BF16)
    wuv = _pad_heads(kv[:, :, MLA_NOPE_DIM:].reshape(MLA_KV_RANK, -1), MLA_HEADS, MLA_V_DIM).astype(BF16)
    return w_cat, wuq, wuk, wuv


def _rope_tables(s_len, c_len, rot_dim, lane_off):
    half = rot_dim // 2
    n_freq = half // 2
    pos = jnp.arange(s_len)
    rows = (pos // GRID_W).astype(F32)
    cols = (pos % GRID_W).astype(F32)
    inv = jnp.power(ROPE_THETA, -jnp.arange(n_freq, dtype=F32) / n_freq)
    ang = jnp.concatenate([rows[:, None] * inv, cols[:, None] * inv], axis=-1)
    cos, sin = jnp.cos(ang), jnp.sin(ang)
    zeros = jnp.zeros((s_len, half), F32)
    right = LANE - lane_off - rot_dim
    pad = lambda a, b: jnp.pad(jnp.concatenate([a, b], axis=-1), ((0, 0), (lane_off, right)))
    cos_t = pad(cos, cos) + jnp.pad(jnp.ones((s_len, lane_off), F32), ((0, 0), (0, LANE - lane_off)))
    sa_t = pad(-sin, zeros)
    sb_t = pad(zeros, sin)
    ident = jnp.pad(jnp.ones((c_len, lane_off + rot_dim), F32), ((0, 0), (0, right)))
    zc = jnp.zeros((c_len, LANE), F32)
    return jnp.stack([jnp.concatenate([cos_t, ident]), jnp.concatenate([sa_t, zc]),
                      jnp.concatenate([sb_t, zc])])


def _pad_lane(v):
    return jnp.pad(v, (0, LANE - v.shape[0])).reshape(1, LANE)


def kernel(x, c, ctx, c_ctx, w_ada, b_ada, norm_mix, norm_ffn, w_in, q_norm_gqa, k_norm_gqa,
           q_lora_norm, kv_lora_norm, w_uq, w_ukv, rpb, w_o_na, w_o_gqa, w_o_mla, w_out,
           w_ffn_gu, w_ffn_dn, w_router, w_moe_gu, w_moe_dn, norm_final):
    b, s_len, d = x.shape
    c_len = ctx.shape[1]
    t_len = s_len + c_len
    depth = w_ada.shape[0]
    n_rows = s_len // GRID_W
    n_tok = b * t_len
    n_tiles = n_tok // TOKEN_TILE

    c_rows = jnp.zeros((16, d), F32).at[:b].set(c).at[b].set(c_ctx)
    mods = _ada_call(c_rows, w_ada, b_ada)
    rope_g = _rope_tables(s_len, c_len, HEAD_DIM, 0)
    rope_m = _rope_tables(s_len, c_len, MLA_ROPE_DIM, MLA_NOPE_DIM)
    na_bias = _na_bias_tables(rpb, n_rows)

    xc = jnp.concatenate([x, ctx], axis=1)
    for i in range(depth):
        lat = mods[i, :b].reshape(b, 1, 6, d)
        cm = jnp.broadcast_to(mods[i, b].reshape(1, 1, 6, d), (b, 1, 6, d))
        mod = jnp.concatenate([lat, cm], axis=1).reshape(2 * b, 6, d)

        w_cat, wuq, wuk, wuv = _layer_weights(w_in[i], w_uq[i], w_ukv[i])
        q_na, k_na, v_na, q_g, k_g, v_g, q_m, k_m, v_m, gate = _mixin_call(
            xc, mod, norm_mix[i].reshape(1, d), w_cat, _pad_lane(q_norm_gqa[i]), _pad_lane(k_norm_gqa[i]),
            q_lora_norm[i].reshape(1, -1), kv_lora_norm[i].reshape(1, -1), wuq, wuk, wuv, rope_g, rope_m)

        y_na = _na_call(q_na, k_na, v_na, na_bias[i], s_len, c_len)
        y_g = _flash_call(q_g, k_g, v_g, n_kv=GQA_KV_HEADS, group=GQA_GROUP, tq=256,
                          s_len=s_len, c_len=c_len, name="gqa_attn")
        y_m = _flash_call(q_m, k_m, v_m, n_kv=MLA_HEADS, group=1, tq=512,
                          s_len=s_len, c_len=c_len, name="mla_attn")
        xc = _merge_call(xc, mod, y_na, y_g, y_m, gate, w_o_na[i].astype(BF16), w_o_gqa[i].astype(BF16),
                         w_o_mla[i].astype(BF16), w_out[i].astype(BF16))

        j = i // 2
        gn = norm_ffn[i].reshape(1, d)
        if i % 2 == 0:
            f_dim = w_ffn_dn.shape[1]
            xc = _ffn_call(xc, mod, gn, w_ffn_gu[j][:, :f_dim].astype(BF16),
                           w_ffn_gu[j][:, f_dim:].astype(BF16), w_ffn_dn[j].astype(BF16))
        else:
            n_exp = w_router.shape[-1]
            wr = jnp.pad(w_router[j], ((0, 0), (0, LANE - n_exp)))
            idx, wts, cnt = _router_call(xc, mod, gn, wr, n_exp)
            counts = cnt[0, :n_exp].astype(jnp.int32)
            padded = (counts + MOE_ROW_BLOCK - 1) // MOE_ROW_BLOCK * MOE_ROW_BLOCK
            pad_end = jnp.cumsum(padded)
            pad_start = pad_end - padded
            idx = idx.reshape(n_tok, LANE)
            dest = pad_start[idx[:, 0:TOP_K]] + idx[:, TOP_K:2 * TOP_K]
            dest_blocks = dest.reshape(n_tiles, 1, TOP_K * TOKEN_TILE)
            n_blocks = -(-(n_tok * TOP_K + n_exp * (MOE_ROW_BLOCK - 1)) // MOE_ROW_BLOCK)
            starts = jnp.arange(n_blocks) * MOE_ROW_BLOCK
            block_expert = jnp.minimum(jnp.sum(starts[:, None] >= pad_end[None, :], axis=-1),
                                       n_exp - 1).astype(jnp.int32)
            n_used = (pad_end[-1] // MOE_ROW_BLOCK).astype(jnp.int32).reshape(1)
            xs = _dispatch_call(dest_blocks, xc, mod, gn, jnp.zeros((n_blocks * MOE_ROW_BLOCK, d), F32))
            y_rows = _moe_call(block_expert, n_used, xs, w_moe_gu[j].astype(BF16), w_moe_dn[j].astype(BF16))
            xc = _combine_call(dest_blocks, xc, mod, wts, y_rows)
    return _final_call(xc, norm_final.reshape(1, d), s_len)
```

```python
import functools

import numpy as np
import jax
import jax.numpy as jnp
from jax import lax
from jax.experimental import pallas as pl
from jax.experimental.pallas import tpu as pltpu

GRID_W = 64
HEAD_DIM = 64
NA_HEADS = 4
WIN_H = 8
WIN_W = 16
GQA_Q_HEADS = 8
GQA_KV_HEADS = 2
GQA_GROUP = GQA_Q_HEADS // GQA_KV_HEADS
MLA_HEADS = 4
MLA_Q_RANK = 256
MLA_KV_RANK = 128
MLA_NOPE_DIM = 64
MLA_ROPE_DIM = 32
MLA_V_DIM = 64
ROPE_THETA = 10000.0
TOP_K = 2
NORM_EPS = 1e-6
NA_SCALE = HEAD_DIM ** -0.5
GQA_SCALE = HEAD_DIM ** -0.5
MLA_SCALE = (MLA_NOPE_DIM + MLA_ROPE_DIM) ** -0.5
LOG2_E = 1.4426950408889634

LANE = 128
TOKEN_TILE = 256
NA_Q_ROWS = 4
NA_K_ROWS = NA_Q_ROWS + WIN_H
KV_CHUNK = 512
MOE_ROW_BLOCK = 512
VMEM_LIMIT = 56 * 1024 * 1024
NEG_BIG = -1e30
SAFE_LOG2_RANGE = 80.0
NORM_SLACK = 1.02
_N_QG, _N_KG = 0, GQA_Q_HEADS
_N_QM, _N_KM = _N_KG + GQA_KV_HEADS, _N_KG + GQA_KV_HEADS + MLA_HEADS

F32 = jnp.float32
BF16 = jnp.bfloat16


def _params(sem, vmem=VMEM_LIMIT):
    return pltpu.CompilerParams(dimension_semantics=sem, vmem_limit_bytes=vmem)


def _resident(shape):
    zeros = (0,) * len(shape)
    return pl.BlockSpec(shape, lambda *_: zeros, pipeline_mode=pl.Buffered(1))


def _rms(x, n):
    ss = jnp.sum(x * x, axis=-1, keepdims=True)
    return x * lax.rsqrt(ss * (1.0 / n) + NORM_EPS)


def _dot(a, b):
    return jnp.dot(a, b, preferred_element_type=F32)


def _dot_nt(a, b):
    return lax.dot_general(a, b, (((1,), (1,)), ((), ())), preferred_element_type=F32)


def _silu(x):
    return x * jax.nn.sigmoid(x)


def _ada_kernel(c_ref, w_ref, b_ref, o_ref):
    s = _silu(c_ref[...])
    o_ref[0] = jnp.dot(s, w_ref[0], precision=lax.Precision.HIGHEST,
                       preferred_element_type=F32) + b_ref[0]


def _ada_call(c_rows, w_ada, b_ada):
    depth, d, n = w_ada.shape
    rows = c_rows.shape[0]
    tn = 1536 if n % 1536 == 0 else n
    return pl.pallas_call(
        _ada_kernel,
        out_shape=jax.ShapeDtypeStruct((depth, rows, n), F32),
        grid=(depth, n // tn),
        in_specs=[
            pl.BlockSpec((rows, d), lambda i, j: (0, 0)),
            pl.BlockSpec((1, d, tn), lambda i, j: (i, 0, j)),
            pl.BlockSpec((1, 1, tn), lambda i, j: (i, 0, j)),
        ],
        out_specs=pl.BlockSpec((1, rows, tn), lambda i, j: (i, 0, j)),
        compiler_params=_params(("arbitrary", "arbitrary")),
        name="ada_mod",
    )(c_rows, w_ada, b_ada.reshape(depth, 1, n))


def _rope(xh, cos, sa, sb, shift):
    return xh * cos + pltpu.roll(xh, LANE - shift, 1) * sa + pltpu.roll(xh, shift, 1) * sb


_W_NA = 3 * NA_HEADS * HEAD_DIM
_W_QG = GQA_Q_HEADS * HEAD_DIM
_W_KVG = 2 * GQA_KV_HEADS * LANE
_W_LORA = MLA_Q_RANK + MLA_KV_RANK + LANE
_C_NA = 0
_C_QG = _C_NA + _W_NA
_C_KVG = _C_QG + _W_QG
_C_LORA = _C_KVG + _W_KVG
_C_GATE = _C_LORA + _W_LORA


def _mixin_kernel(x_ref, mod_ref, gmix_ref, w_ref, qn_ref, kn_ref, qln_ref, kvln_ref,
                  wuq_ref, wukv_ref, rg_ref, rm_ref,
                  qna_ref, kna_ref, vna_ref, qg_ref, kg_ref, vg_ref, qm_ref, km_ref, vm_ref,
                  gate_ref, nrm_ref):
    x = x_ref[0]
    d = x.shape[-1]
    tm = x.shape[0]
    mod = mod_ref[0]
    h = _rms(x, d) * gmix_ref[...]
    h = h * (1.0 + mod[1:2]) + mod[0:1]
    hb = h.astype(BF16)

    def proj(lo, width):
        return _dot(hb, w_ref[:, lo:lo + width])

    def block(a, j):
        return a[:, j * LANE:(j + 1) * LANE]

    nw = NA_HEADS * HEAD_DIM
    p_na = proj(_C_NA, _W_NA)
    qna_ref[0] = (p_na[:, 0:nw] * (NA_SCALE * LOG2_E)).astype(BF16)
    kna_ref[0] = p_na[:, nw:2 * nw].astype(BF16)
    vna_ref[0] = p_na[:, 2 * nw:3 * nw].astype(BF16)

    lane = lax.broadcasted_iota(jnp.int32, (tm, LANE), 1)
    low = lane < HEAD_DIM
    ones_pad = jnp.where(low, 0.0, 1.0)
    norms = []

    def unit_at(pos):
        return jnp.where(lane == pos, 1.0, 0.0)

    cos_g, sa_g, sb_g = rg_ref[0], rg_ref[1], rg_ref[2]
    cos_m, sa_m, sb_m = rm_ref[0], rm_ref[1], rm_ref[2]
    g_half = HEAD_DIM // 2
    m_half = MLA_ROPE_DIM // 2

    p_qg = proj(_C_QG, _W_QG)
    for j in range(GQA_Q_HEADS // 2):
        xp = block(p_qg, j)
        sq = xp * xp
        ss_lo = jnp.sum(jnp.where(low, sq, 0.0), axis=-1, keepdims=True)
        ss_hi = jnp.sum(jnp.where(low, 0.0, sq), axis=-1, keepdims=True)
        inv = jnp.where(low, lax.rsqrt(ss_lo * (1.0 / HEAD_DIM) + NORM_EPS),
                        lax.rsqrt(ss_hi * (1.0 / HEAD_DIM) + NORM_EPS))
        xp = _rope(xp * inv * qn_ref[...], cos_g, sa_g, sb_g, g_half) * (GQA_SCALE * LOG2_E)
        qg_ref[0, :, j * LANE:(j + 1) * LANE] = xp.astype(BF16)
        sq = xp * xp
        norms.append((_N_QG + 2 * j, jnp.sum(jnp.where(low, sq, 0.0), axis=-1, keepdims=True)))
        norms.append((_N_QG + 2 * j + 1, jnp.sum(jnp.where(low, 0.0, sq), axis=-1, keepdims=True)))
    p_kvg = proj(_C_KVG, _W_KVG)
    for hh in range(GQA_KV_HEADS):
        xh = _rope(_rms(block(p_kvg, hh), HEAD_DIM) * kn_ref[...], cos_g, sa_g, sb_g, g_half)
        norms.append((_N_KG + hh, jnp.sum(xh * xh, axis=-1, keepdims=True)))
        kg_ref[0, :, hh * LANE:(hh + 1) * LANE] = (xh + unit_at(HEAD_DIM)).astype(BF16)
        vh = block(p_kvg, GQA_KV_HEADS + hh) + ones_pad
        vg_ref[0, :, hh * LANE:(hh + 1) * LANE] = vh.astype(BF16)

    p_lora = proj(_C_LORA, _W_LORA)
    cq = (_rms(p_lora[:, 0:MLA_Q_RANK], MLA_Q_RANK) * qln_ref[...]).astype(BF16)
    q_up = _dot(cq, wuq_ref[...])
    ckv = p_lora[:, MLA_Q_RANK:MLA_Q_RANK + MLA_KV_RANK]
    ckv = (_rms(ckv, MLA_KV_RANK) * kvln_ref[...]).astype(BF16)
    kv_up = _dot(ckv, wukv_ref[...])
    k_rope = _rope(p_lora[:, MLA_Q_RANK + MLA_KV_RANK:], cos_m, sa_m, sb_m, m_half)
    for hh in range(MLA_HEADS):
        sl = slice(hh * LANE, (hh + 1) * LANE)
        qh = _rope(block(q_up, hh), cos_m, sa_m, sb_m, m_half) * (MLA_SCALE * LOG2_E)
        qm_ref[0, :, sl] = qh.astype(BF16)
        kh = block(kv_up, hh) + k_rope
        norms.append((_N_QM + hh, jnp.sum(qh * qh, axis=-1, keepdims=True)))
        norms.append((_N_KM + hh, jnp.sum(kh * kh, axis=-1, keepdims=True)))
        km_ref[0, :, sl] = (kh + unit_at(MLA_NOPE_DIM + MLA_ROPE_DIM)).astype(BF16)
        vm_ref[0, :, sl] = (block(kv_up, MLA_HEADS + hh) + ones_pad).astype(BF16)

    packed = jnp.zeros((tm, LANE), F32)
    for pos, val in norms:
        packed = jnp.where(lane == pos, val, packed)
    nrm_ref[0] = packed

    for j in range(3):
        gate_ref[0, :, j * d:(j + 1) * d] = jax.nn.sigmoid(proj(_C_GATE + j * d, d)).astype(BF16)


def _mixin_call(xc, mod, gmix, w_cat, qn, kn, qln, kvln, wuq, wukv, rope_g, rope_m):
    b, t, d = xc.shape
    tm = TOKEN_TILE
    nt = t // tm
    tok = lambda w: pl.BlockSpec((1, tm, w), lambda ti, bi: (bi, ti, 0))
    widths = [NA_HEADS * HEAD_DIM] * 3 + [GQA_Q_HEADS * HEAD_DIM, GQA_KV_HEADS * LANE,
                                          GQA_KV_HEADS * LANE, MLA_HEADS * LANE,
                                          MLA_HEADS * LANE, MLA_HEADS * LANE, 3 * d]
    return pl.pallas_call(
        _mixin_kernel,
        out_shape=[jax.ShapeDtypeStruct((b, t, w), BF16) for w in widths]
        + [jax.ShapeDtypeStruct((b, t, LANE), F32)],
        grid=(nt, b),
        in_specs=[
            tok(d),
            pl.BlockSpec((1, 6, d), lambda ti, bi: (2 * bi + (ti == nt - 1).astype(jnp.int32), 0, 0)),
            _resident((1, d)),
            _resident(w_cat.shape),
            _resident((1, LANE)), _resident((1, LANE)),
            _resident((1, MLA_Q_RANK)), _resident((1, MLA_KV_RANK)),
            _resident(wuq.shape), _resident(wukv.shape),
            pl.BlockSpec((3, tm, LANE), lambda ti, bi: (0, ti, 0)),
            pl.BlockSpec((3, tm, LANE), lambda ti, bi: (0, ti, 0)),
        ],
        out_specs=[tok(w) for w in widths] + [tok(LANE)],
        compiler_params=_params(("arbitrary", "arbitrary")),
        name="mixer_in",
    )(xc, mod, gmix, w_cat, qn, kn, qln, kvln, wuq, wukv, rope_g, rope_m)


def _na_kernel(q_ref, k_ref, v_ref, bias_ref, o_ref, *, n_groups, n_rows, s_len, c_len):
    g = pl.program_id(1)
    qn = NA_Q_ROWS * GRID_W
    kn = NA_K_ROWS * GRID_W
    lane = lax.broadcasted_iota(jnp.int32, (qn, LANE), 1)
    low = lane < HEAD_DIM

    def attend_pairs(k_nb, v_nb):
        k_c = k_ref[0, s_len:s_len + c_len, :]
        v_c = v_ref[0, s_len:s_len + c_len, :]
        for pair in range(NA_HEADS // 2):
            sl = slice(pair * LANE, (pair + 1) * LANE)
            qp = q_ref[0, :, sl]
            outs = []
            for sub in range(2):
                head = 2 * pair + sub
                qm = jnp.where(low if sub == 0 else jnp.logical_not(low), qp, jnp.zeros_like(qp))
                s_c = _dot_nt(qm, k_c[:, sl])
                m = jnp.max(s_c, axis=-1, keepdims=True)
                if k_nb is not None:
                    s_nb = _dot_nt(qm, k_nb[:, sl]) + bias_ref[0, head]
                    m = jnp.maximum(m, jnp.max(s_nb, axis=-1, keepdims=True))
                p_c = jnp.exp2(s_c - m)
                den = jnp.sum(p_c, axis=-1, keepdims=True)
                o = _dot(p_c.astype(BF16), v_c[:, sl])
                if k_nb is not None:
                    p_nb = jnp.exp2(s_nb - m)
                    den = den + jnp.sum(p_nb, axis=-1, keepdims=True)
                    o = o + _dot(p_nb.astype(BF16), v_nb[:, sl])
                outs.append(o / den)
            o_ref[0, :, sl] = jnp.where(low, outs[0], outs[1]).astype(BF16)

    @pl.when(g < n_groups)
    def _():
        first_row = jnp.clip(NA_Q_ROWS * g - WIN_H // 2, 0, n_rows - NA_K_ROWS)
        start = pl.multiple_of(first_row * GRID_W, GRID_W)
        attend_pairs(k_ref[0, pl.ds(start, kn), :], v_ref[0, pl.ds(start, kn), :])

    @pl.when(g == n_groups)
    def _():
        attend_pairs(None, None)


def _na_call(q, k, v, bias, s_len, c_len):
    b, t, w = q.shape
    qn = NA_Q_ROWS * GRID_W
    kn = NA_K_ROWS * GRID_W
    n_rows = s_len // GRID_W
    n_groups = s_len // qn
    assert c_len == qn and n_rows >= NA_K_ROWS

    def variant(g):
        return jnp.where(g == 0, 0, jnp.where(g >= n_groups - 1, 2, 1))

    return pl.pallas_call(
        functools.partial(_na_kernel, n_groups=n_groups, n_rows=n_rows, s_len=s_len, c_len=c_len),
        out_shape=jax.ShapeDtypeStruct((b, t, w), BF16),
        grid=(b, n_groups + 1),
        in_specs=[
            pl.BlockSpec((1, qn, w), lambda bi, g: (bi, g, 0)),
            pl.BlockSpec((1, t, w), lambda bi, g: (bi, 0, 0)),
            pl.BlockSpec((1, t, w), lambda bi, g: (bi, 0, 0)),
            pl.BlockSpec((1, NA_HEADS, qn, kn), lambda bi, g: (variant(g), 0, 0, 0)),
        ],
        out_specs=pl.BlockSpec((1, qn, w), lambda bi, g: (bi, g, 0)),
        compiler_params=_params(("arbitrary", "arbitrary")),
        name="na_attn",
    )(q, k, v, bias)


def _na_bias_tables(rpb, n_rows):
    qi = np.arange(NA_Q_ROWS)[:, None]
    kj = np.arange(NA_K_ROWS)[None, :]
    qc = np.arange(GRID_W)[:, None]
    kc = np.arange(GRID_W)[None, :]
    col0 = np.clip(qc - WIN_W // 2, 0, GRID_W - WIN_W)
    col_ok = (kc >= col0) & (kc < col0 + WIN_W)
    dc = np.clip(kc - qc, -(WIN_W - 1), WIN_W - 1) + WIN_W - 1
    col_sel = np.eye(2 * WIN_W - 1, dtype=np.float32)[dc]
    row_sel, valid = [], []
    for r0, u0 in ((0, 0), (WIN_H // 2, 0), (n_rows - NA_Q_ROWS, n_rows - NA_K_ROWS)):
        qr = r0 + qi
        kr = u0 + kj
        row0 = np.clip(qr - WIN_H // 2, 0, n_rows - WIN_H)
        row_ok = (kr >= row0) & (kr < row0 + WIN_H)
        dr = np.clip(kr - qr, -(WIN_H - 1), WIN_H - 1) + WIN_H - 1
        row_sel.append(np.eye(2 * WIN_H - 1, dtype=np.float32)[dr])
        valid.append(row_ok[:, None, :, None] & col_ok[None, :, None, :])
    row_sel = jnp.asarray(np.stack(row_sel))
    valid = np.stack(valid).reshape(3, 1, NA_Q_ROWS * GRID_W, NA_K_ROWS * GRID_W)
    picked = jnp.einsum('vija,lhab,qkb->lvhiqjk', row_sel, rpb.astype(F32) * LOG2_E, jnp.asarray(col_sel),
                        precision=lax.Precision.HIGHEST)
    picked = picked.reshape(rpb.shape[0], 3, NA_HEADS, NA_Q_ROWS * GRID_W, NA_K_ROWS * GRID_W)
    return jnp.where(valid[None], picked, NEG_BIG)


def _row_max(s):
    parts = [s[:, j * LANE:(j + 1) * LANE] for j in range(s.shape[1] // LANE)]
    while len(parts) > 1:
        parts = [jnp.maximum(parts[j], parts[j + 1]) for j in range(0, len(parts) - 1, 2)] + parts[len(parts) & ~1:]
    return jnp.max(parts[0], axis=-1, keepdims=True)


def _flash_kernel(q_ref, k_ref, v_ref, u_ref, o_ref, *, n_kv, group, tq, q_width, s_len, c_len,
                  with_latent, online):
    low = lax.broadcasted_iota(jnp.int32, (tq, LANE), 1) < HEAD_DIM
    heads = range(n_kv)
    lanes = [slice(h * LANE, (h + 1) * LANE) for h in heads]

    def load_q(head):
        if q_width == LANE:
            qh = q_ref[0, :, head * LANE:(head + 1) * LANE]
        else:
            pair = q_ref[0, :, (head // 2) * LANE:(head // 2 + 1) * LANE]
            if head % 2:
                pair = pltpu.roll(pair.astype(F32), HEAD_DIM, 1).astype(BF16)
            qh = jnp.where(low, pair, jnp.zeros_like(pair))
        return qh if online else qh + u_ref[0, head:head + 1, :]

    qs = []
    for h in heads:
        parts = [load_q(h * group + g) for g in range(group)]
        qs.append(parts[0] if group == 1 else jnp.concatenate(parts, axis=0))

    if online:
        carry = []
        for h in heads:
            s = _dot_nt(qs[h], k_ref[0, s_len:s_len + c_len, lanes[h]])
            m = _row_max(s)
            carry += [m, _dot(jnp.exp2(s - m).astype(BF16), v_ref[0, s_len:s_len + c_len, lanes[h]])]
        if with_latent:
            def body(i, carry):
                st = pl.multiple_of(i * KV_CHUNK, KV_CHUNK)
                new = []
                for h in heads:
                    m_old, acc_old = carry[2 * h], carry[2 * h + 1]
                    sc = _dot_nt(qs[h], k_ref[0, pl.ds(st, KV_CHUNK), lanes[h]])
                    m_new = jnp.maximum(m_old, _row_max(sc))
                    p = jnp.exp2(sc - m_new).astype(BF16)
                    pv = _dot(p, v_ref[0, pl.ds(st, KV_CHUNK), lanes[h]])
                    new += [m_new, jnp.exp2(m_old - m_new) * acc_old + pv]
                return tuple(new)
            carry = lax.fori_loop(0, s_len // KV_CHUNK, body, tuple(carry), unroll=2)
        accs = [carry[2 * h + 1] for h in heads]
    else:
        chunks = [(s_len, c_len)]
        if with_latent:
            chunks += [(i * KV_CHUNK, KV_CHUNK) for i in range(s_len // KV_CHUNK)]
        accs = [None] * n_kv
        for st, size in chunks:
            for h in heads:
                p = jnp.exp2(_dot_nt(qs[h], k_ref[0, st:st + size, lanes[h]])).astype(BF16)
                pv = _dot(p, v_ref[0, st:st + size, lanes[h]])
                accs[h] = pv if accs[h] is None else accs[h] + pv
    outs = []
    for h in heads:
        acc = accs[h]
        o = acc / pltpu.roll(acc, HEAD_DIM, 1)
        for g in range(group):
            outs.append(o[g * tq:(g + 1) * tq])
    for pair in range(len(outs) // 2):
        packed = jnp.where(low, outs[2 * pair], pltpu.roll(outs[2 * pair + 1], HEAD_DIM, 1))
        o_ref[0, :, pair * LANE:(pair + 1) * LANE] = packed.astype(BF16)


def _flash_call(q, k, v, shift, *, n_kv, group, tq, q_width, s_len, c_len, online, name):
    b, t, _ = q.shape
    n_heads = n_kv * group
    tq_ctx = min(tq, c_len)
    assert s_len % tq == 0 and c_len % tq_ctx == 0 and s_len % tq_ctx == 0 and s_len % KV_CHUNK == 0
    kv_spec = pl.BlockSpec((1, t, n_kv * LANE), lambda bi, ti: (bi, 0, 0))
    u_spec = pl.BlockSpec((1, n_heads, LANE), lambda bi, ti: (bi, 0, 0))
    common = dict(n_kv=n_kv, group=group, q_width=q_width, s_len=s_len, c_len=c_len, online=online)
    width = n_heads * HEAD_DIM
    tag = "_online" if online else ""
    y_lat = pl.pallas_call(
        functools.partial(_flash_kernel, tq=tq, with_latent=True, **common),
        out_shape=jax.ShapeDtypeStruct((b, s_len, width), BF16),
        grid=(b, s_len // tq),
        in_specs=[pl.BlockSpec((1, tq, n_heads * q_width), lambda bi, ti: (bi, ti, 0)), kv_spec, kv_spec, u_spec],
        out_specs=pl.BlockSpec((1, tq, width), lambda bi, ti: (bi, ti, 0)),
        compiler_params=_params(("arbitrary", "arbitrary")),
        name=name + tag,
    )(q, k, v, shift)
    first = s_len // tq_ctx
    y_ctx = pl.pallas_call(
        functools.partial(_flash_kernel, tq=tq_ctx, with_latent=False, **common),
        out_shape=jax.ShapeDtypeStruct((b, c_len, width), BF16),
        grid=(b, c_len // tq_ctx),
        in_specs=[pl.BlockSpec((1, tq_ctx, n_heads * q_width), lambda bi, ti: (bi, first + ti, 0)), kv_spec, kv_spec,
                  u_spec],
        out_specs=pl.BlockSpec((1, tq_ctx, width), lambda bi, ti: (bi, ti, 0)),
        compiler_params=_params(("arbitrary", "arbitrary")),
        name=name + "_ctx" + tag,
    )(q, k, v, shift)
    return y_lat, y_ctx


def _bounded_attention(q, k, v, q_norm2, k_norm2, unit_lane, **kw):
    n_kv, group = kw["n_kv"], kw["group"]
    k_norm2 = jnp.repeat(k_norm2, group, axis=1)
    bound = jnp.sqrt(q_norm2 * k_norm2) * NORM_SLACK
    lane = jnp.arange(LANE) == unit_lane
    shift = jnp.where(lane[None, None, :], -bound[:, :, None], 0.0).astype(BF16)
    safe = jnp.max(bound) * 2.0 < SAFE_LOG2_RANGE
    return lax.cond(safe,
                    lambda ops: _flash_call(*ops, online=False, **kw),
                    lambda ops: _flash_call(*ops, online=True, **kw),
                    (q, k, v, shift))


def _merge_kernel(x_ref, mod_ref, yna_ref, yg_ref, ygc_ref, ym_ref, ymc_ref, gate_ref, wna_ref, wg_ref,
                  wm_ref, wout_ref, o_ref, *, n_lat_tiles):
    d = x_ref.shape[-1]
    is_ctx = pl.program_id(1) >= n_lat_tiles
    y_g = jnp.where(is_ctx, ygc_ref[0], yg_ref[0])
    y_m = jnp.where(is_ctx, ymc_ref[0], ym_ref[0])
    m = gate_ref[0, :, 0:d].astype(F32) * _dot(yna_ref[0], wna_ref[...])
    m = m + gate_ref[0, :, d:2 * d].astype(F32) * _dot(y_g, wg_ref[...])
    m = m + gate_ref[0, :, 2 * d:3 * d].astype(F32) * _dot(y_m, wm_ref[...])
    r = _dot(m.astype(BF16), wout_ref[...])
    o_ref[0] = x_ref[0] + mod_ref[0][2:3] * r


def _mod_spec(d, nt):
    return pl.BlockSpec((1, 6, d), lambda bi, ti: (2 * bi + (ti == nt - 1).astype(jnp.int32), 0, 0))


def _merge_call(xc, mod, y_na, y_g, y_m, gate, w_na, w_g, w_m, w_out):
    b, t, d = xc.shape
    tm = TOKEN_TILE
    nt = t // tm
    n_lat = y_g[0].shape[1] // tm
    tok = lambda w: pl.BlockSpec((1, tm, w), lambda bi, ti: (bi, ti, 0))
    lat = lambda w: pl.BlockSpec((1, tm, w), lambda bi, ti: (bi, jnp.minimum(ti, n_lat - 1), 0))
    ctx = lambda w: pl.BlockSpec((1, tm, w), lambda bi, ti: (bi, jnp.maximum(ti - n_lat, 0), 0))
    wg_, wm_ = y_g[0].shape[-1], y_m[0].shape[-1]
    return pl.pallas_call(
        functools.partial(_merge_kernel, n_lat_tiles=n_lat),
        out_shape=jax.ShapeDtypeStruct(xc.shape, F32),
        grid=(b, nt),
        in_specs=[tok(d), _mod_spec(d, nt), tok(y_na.shape[-1]), lat(wg_), ctx(wg_), lat(wm_), ctx(wm_),
                  tok(3 * d), _resident(w_na.shape), _resident(w_g.shape),
                  _resident(w_m.shape), _resident(w_out.shape)],
        out_specs=tok(d),
        input_output_aliases={0: 0},
        compiler_params=_params(("arbitrary", "arbitrary")),
        name="merge_out",
    )(xc, mod, y_na, y_g[0], y_g[1], y_m[0], y_m[1], gate, w_na, w_g, w_m, w_out)


def _norm2(x, mod, gain):
    h = _rms(x, x.shape[-1]) * gain
    return h * (1.0 + mod[4:5]) + mod[3:4]


def _ffn_kernel(x_ref, mod_ref, gn_ref, wg_ref, wu_ref, wd_ref, o_ref):
    x = x_ref[0]
    mod = mod_ref[0]
    hb = _norm2(x, mod, gn_ref[...]).astype(BF16)
    act = (_silu(_dot(hb, wg_ref[...])) * _dot(hb, wu_ref[...])).astype(BF16)
    o_ref[0] = x + mod[5:6] * _dot(act, wd_ref[...])


def _ffn_call(xc, mod, gn, w_gate, w_up, w_dn):
    b, t, d = xc.shape
    tm = TOKEN_TILE
    nt = t // tm
    tok = pl.BlockSpec((1, tm, d), lambda bi, ti: (bi, ti, 0))
    return pl.pallas_call(
        _ffn_kernel,
        out_shape=jax.ShapeDtypeStruct(xc.shape, F32),
        grid=(b, nt),
        in_specs=[tok, _mod_spec(d, nt), _resident((1, d)), _resident(w_gate.shape),
                  _resident(w_up.shape), _resident(w_dn.shape)],
        out_specs=tok,
        input_output_aliases={0: 0},
        compiler_params=_params(("arbitrary", "arbitrary")),
        name="ffn_dense",
    )(xc, mod, gn, w_gate, w_up, w_dn)


def _router_kernel(x_ref, mod_ref, gn_ref, wr_ref, idx_ref, wt_ref, cnt_ref, carry_ref,
                   *, n_experts):
    @pl.when((pl.program_id(0) == 0) & (pl.program_id(1) == 0))
    def _():
        carry_ref[...] = jnp.zeros_like(carry_ref)

    x = x_ref[0]
    tm = x.shape[0]
    h = _norm2(x, mod_ref[0], gn_ref[...])
    logits = jnp.dot(h, wr_ref[...], precision=lax.Precision.HIGHEST, preferred_element_type=F32)
    lane = lax.broadcasted_iota(jnp.int32, (tm, LANE), 1).astype(F32)
    logits = jnp.where(lane < n_experts, logits, -jnp.inf)
    m1 = jnp.max(logits, axis=-1, keepdims=True)
    i1 = jnp.min(jnp.where(logits == m1, lane, float(LANE)), axis=-1, keepdims=True)
    rest = jnp.where(lane == i1, -jnp.inf, logits)
    m2 = jnp.max(rest, axis=-1, keepdims=True)
    i2 = jnp.min(jnp.where(rest == m2, lane, float(LANE)), axis=-1, keepdims=True)
    e2 = jnp.exp(m2 - m1)
    w1 = 1.0 / (1.0 + e2)
    w2 = e2 / (1.0 + e2)
    hot1 = lane == i1
    hot2 = lane == i2
    hot = jnp.where(hot1 | hot2, 1.0, 0.0)
    rows = lax.broadcasted_iota(jnp.int32, (tm, tm), 0)
    cols = lax.broadcasted_iota(jnp.int32, (tm, tm), 1)
    below = jnp.where(rows > cols, 1.0, 0.0).astype(BF16)
    before = _dot(below, hot.astype(BF16)) + carry_ref[0:1, :]
    r1 = jnp.sum(jnp.where(hot1, before, 0.0), axis=-1, keepdims=True)
    r2 = jnp.sum(jnp.where(hot2, before, 0.0), axis=-1, keepdims=True)
    total = carry_ref[0:1, :] + jnp.sum(hot, axis=0, keepdims=True)
    carry_ref[...] = jnp.broadcast_to(total, carry_ref.shape)
    cnt_ref[...] = jnp.broadcast_to(total, cnt_ref.shape)
    packed = jnp.where(lane == 0, i1, jnp.where(lane == 1, i2, jnp.where(lane == 2, r1, jnp.where(lane == 3, r2, 0.0))))
    idx_ref[0] = packed.astype(jnp.int32)
    wt_ref[0] = jnp.where(lane == 0, w1, jnp.where(lane == 1, w2, 0.0))


def _router_call(xc, mod, gn, w_router_pad, n_experts):
    b, t, d = xc.shape
    tm = TOKEN_TILE
    nt = t // tm
    tok = lambda w: pl.BlockSpec((1, tm, w), lambda bi, ti: (bi, ti, 0))
    return pl.pallas_call(
        functools.partial(_router_kernel, n_experts=n_experts),
        out_shape=[jax.ShapeDtypeStruct((b, t, LANE), jnp.int32),
                   jax.ShapeDtypeStruct((b, t, LANE), F32),
                   jax.ShapeDtypeStruct((8, LANE), F32)],
        grid=(b, nt),
        in_specs=[tok(d), _mod_spec(d, nt), _resident((1, d)), _resident(w_router_pad.shape)],
        out_specs=[tok(LANE), tok(LANE), pl.BlockSpec((8, LANE), lambda bi, ti: (0, 0))],
        scratch_shapes=[pltpu.VMEM((8, LANE), F32)],
        compiler_params=_params(("arbitrary", "arbitrary")),
        name="moe_router",
    )(xc, mod, gn, w_router_pad)


def _row_copy(src, src_row, dst, dst_row, sem):
    return pltpu.make_async_copy(src.at[pl.ds(src_row, 1)], dst.at[pl.ds(dst_row, 1)], sem)


def _dispatch_kernel(dest_ref, x_ref, mod_ref, gn_ref, xs_in_ref, xs_ref, h_buf, sem):
    del xs_in_ref
    tm = x_ref.shape[1]
    h_buf[...] = _norm2(x_ref[0], mod_ref[0], gn_ref[...])

    def issue(i, carry):
        for k in range(TOP_K):
            _row_copy(h_buf, i, xs_ref, dest_ref[0, 0, TOP_K * i + k], sem).start()
        return carry

    lax.fori_loop(0, tm, issue, 0, unroll=8)

    def drain(i, carry):
        for k in range(TOP_K):
            _row_copy(h_buf, 0, xs_ref, 0, sem).wait()
        return carry

    lax.fori_loop(0, tm, drain, 0, unroll=8)


def _dispatch_call(dest_blocks, xc, mod, gn, xs_zero):
    b, t, d = xc.shape
    tm = TOKEN_TILE
    nt = t // tm
    return pl.pallas_call(
        _dispatch_kernel,
        out_shape=jax.ShapeDtypeStruct(xs_zero.shape, F32),
        grid=(b, nt),
        in_specs=[
            pl.BlockSpec((1, 1, dest_blocks.shape[-1]), lambda bi, ti: (bi * nt + ti, 0, 0),
                         memory_space=pltpu.SMEM),
            pl.BlockSpec((1, tm, d), lambda bi, ti: (bi, ti, 0)),
            _mod_spec(d, nt), _resident((1, d)),
            pl.BlockSpec(memory_space=pl.ANY),
        ],
        out_specs=pl.BlockSpec(memory_space=pl.ANY),
        scratch_shapes=[pltpu.VMEM((tm, d), F32), pltpu.SemaphoreType.DMA],
        input_output_aliases={4: 0},
        compiler_params=_params(("arbitrary", "arbitrary")),
        name="moe_dispatch",
    )(dest_blocks, xc, mod, gn, xs_zero)


def _moe_kernel(be_ref, nu_ref, xs_ref, wg_ref, wu_ref, wd_ref, y_ref):
    i = pl.program_id(0)
    f = pl.program_id(1)

    @pl.when(i < nu_ref[0])
    def _():
        xb = xs_ref[...].astype(BF16)
        act = (_silu(_dot(xb, wg_ref[0])) * _dot(xb, wu_ref[0])).astype(BF16)
        part = _dot(act, wd_ref[0])

        @pl.when(f == 0)
        def _():
            y_ref[...] = part

        @pl.when(f > 0)
        def _():
            y_ref[...] += part

    @pl.when((i >= nu_ref[0]) & (f == 0))
    def _():
        y_ref[...] = jnp.zeros_like(y_ref)


def _moe_call(block_expert, n_used, xs, w_gu, w_dn):
    rows, d = xs.shape
    n_exp, _, two_f = w_gu.shape
    f_dim = two_f // 2
    tf = f_dim // 2 if (f_dim // 2) % LANE == 0 else f_dim
    nf = f_dim // tf
    tmr = MOE_ROW_BLOCK
    grid_spec = pltpu.PrefetchScalarGridSpec(
        num_scalar_prefetch=2,
        grid=(rows // tmr, nf),
        in_specs=[
            pl.BlockSpec((tmr, d), lambda i, f, be, nu: (i, 0)),
            pl.BlockSpec((1, d, tf), lambda i, f, be, nu: (be[i], 0, f)),
            pl.BlockSpec((1, d, tf), lambda i, f, be, nu: (be[i], 0, f + nf)),
            pl.BlockSpec((1, tf, d), lambda i, f, be, nu: (be[i], f, 0)),
        ],
        out_specs=pl.BlockSpec((tmr, d), lambda i, f, be, nu: (i, 0)),
    )
    return pl.pallas_call(
        _moe_kernel,
        out_shape=jax.ShapeDtypeStruct((rows, d), F32),
        grid_spec=grid_spec,
        compiler_params=_params(("arbitrary", "arbitrary")),
        name="moe_experts",
    )(block_expert, n_used, xs, w_gu, w_gu, w_dn)


def _combine_kernel(dest_ref, x_ref, mod_ref, wt_ref, y_ref, o_ref, buf1, buf2, sem):
    tm = x_ref.shape[1]

    def issue(i, carry):
        _row_copy(y_ref, dest_ref[0, 0, TOP_K * i], buf1, i, sem).start()
        _row_copy(y_ref, dest_ref[0, 0, TOP_K * i + 1], buf2, i, sem).start()
        return carry

    lax.fori_loop(0, tm, issue, 0, unroll=8)

    def drain(i, carry):
        _row_copy(y_ref, 0, buf1, 0, sem).wait()
        _row_copy(y_ref, 0, buf2, 0, sem).wait()
        return carry

    lax.fori_loop(0, tm, drain, 0, unroll=8)
    wt = wt_ref[0]
    mix = wt[:, 0:1] * buf1[...] + wt[:, 1:2] * buf2[...]
    o_ref[0] = x_ref[0] + mod_ref[0][5:6] * mix


def _combine_call(dest_blocks, xc, mod, wts, y_rows):
    b, t, d = xc.shape
    tm = TOKEN_TILE
    nt = t // tm
    tok = lambda w: pl.BlockSpec((1, tm, w), lambda bi, ti: (bi, ti, 0))
    return pl.pallas_call(
        _combine_kernel,
        out_shape=jax.ShapeDtypeStruct(xc.shape, F32),
        grid=(b, nt),
        in_specs=[
            pl.BlockSpec((1, 1, dest_blocks.shape[-1]), lambda bi, ti: (bi * nt + ti, 0, 0),
                         memory_space=pltpu.SMEM),
            tok(d), _mod_spec(d, nt), tok(LANE),
            pl.BlockSpec(memory_space=pl.ANY),
        ],
        out_specs=tok(d),
        scratch_shapes=[pltpu.VMEM((tm, d), F32), pltpu.VMEM((tm, d), F32), pltpu.SemaphoreType.DMA],
        input_output_aliases={1: 0},
        compiler_params=_params(("arbitrary", "arbitrary")),
        name="moe_combine",
    )(dest_blocks, xc, mod, wts, y_rows)


def _final_kernel(x_ref, g_ref, o_ref):
    x = x_ref[0]
    o_ref[0] = _rms(x, x.shape[-1]) * g_ref[...]


def _final_call(xc, gain, s_len):
    b, t, d = xc.shape
    tm = TOKEN_TILE
    tok = pl.BlockSpec((1, tm, d), lambda bi, ti: (bi, ti, 0))
    return pl.pallas_call(
        _final_kernel,
        out_shape=jax.ShapeDtypeStruct((b, s_len, d), F32),
        grid=(b, s_len // tm),
        in_specs=[tok, _resident((1, d))],
        out_specs=tok,
        compiler_params=_params(("arbitrary", "arbitrary")),
        name="final_norm",
    )(xc, gain)


def _pad_heads(w, n_heads, width):
    k = w.shape[0]
    w = w.reshape(k, n_heads, width)
    return jnp.pad(w, ((0, 0), (0, 0), (0, LANE - width))).reshape(k, n_heads * LANE)


def _layer_weights(w_in, w_uq, w_ukv):
    d = w_in.shape[0]
    nw = NA_HEADS * HEAD_DIM
    gk = GQA_KV_HEADS * HEAD_DIM
    o = 0
    k_na = w_in[:, o:o + nw]; o += nw
    v_na = w_in[:, o:o + nw]; o += nw
    k_g = w_in[:, o:o + gk]; o += gk
    v_g = w_in[:, o:o + gk]; o += gk
    c_kv = w_in[:, o:o + MLA_KV_RANK]; o += MLA_KV_RANK
    k_r = w_in[:, o:o + MLA_ROPE_DIM]; o += MLA_ROPE_DIM
    q_na = w_in[:, o:o + nw]; o += nw
    q_g = w_in[:, o:o + GQA_Q_HEADS * HEAD_DIM]; o += GQA_Q_HEADS * HEAD_DIM
    c_q = w_in[:, o:o + MLA_Q_RANK]; o += MLA_Q_RANK
    gate = w_in[:, o:]
    k_r_pad = jnp.pad(k_r, ((0, 0), (MLA_NOPE_DIM, LANE - MLA_NOPE_DIM - MLA_ROPE_DIM)))
    w_cat = jnp.concatenate([
        q_na, k_na, v_na, q_g, _pad_heads(k_g, GQA_KV_HEADS, HEAD_DIM), _pad_heads(v_g, GQA_KV_HEADS, HEAD_DIM), c_q, c_kv, k_r_pad, gate],
        axis=1).astype(BF16)
    assert w_cat.shape[1] == _C_GATE + 3 * d
    wuq = _pad_heads(w_uq, MLA_HEADS, MLA_NOPE_DIM + MLA_ROPE_DIM).astype(BF16)
    kv = w_ukv.reshape(MLA_KV_RANK, MLA_HEADS, MLA_NOPE_DIM + MLA_V_DIM)
    wuk = _pad_heads(kv[:, :, :MLA_NOPE_DIM].reshape(MLA_KV_RANK, -1), MLA_HEADS, MLA_NOPE_DIM)
    wuv = _pad_heads(kv[:, :, MLA_NOPE_DIM:].reshape(MLA_KV_RANK, -1), MLA_HEADS, MLA_V_DIM)
    return w_cat, wuq, jnp.concatenate([wuk, wuv], axis=1).astype(BF16)


def _rope_tables(s_len, c_len, rot_dim, lane_off):
    half = rot_dim // 2
    n_freq = half // 2
    pos = jnp.arange(s_len)
    rows = (pos // GRID_W).astype(F32)
    cols = (pos % GRID_W).astype(F32)
    inv = jnp.power(ROPE_THETA, -jnp.arange(n_freq, dtype=F32) / n_freq)
    ang = jnp.concatenate([rows[:, None] * inv, cols[:, None] * inv], axis=-1)
    cos, sin = jnp.cos(ang), jnp.sin(ang)
    zeros = jnp.zeros((s_len, half), F32)
    right = LANE - lane_off - rot_dim
    pad = lambda a, b: jnp.pad(jnp.concatenate([a, b], axis=-1), ((0, 0), (lane_off, right)))
    cos_t = pad(cos, cos) + jnp.pad(jnp.ones((s_len, lane_off), F32), ((0, 0), (0, LANE - lane_off)))
    sa_t = pad(-sin, zeros)
    sb_t = pad(zeros, sin)
    ident = jnp.pad(jnp.ones((c_len, lane_off + rot_dim), F32), ((0, 0), (0, right)))
    zc = jnp.zeros((c_len, LANE), F32)
    tables = jnp.stack([jnp.concatenate([cos_t, ident]), jnp.concatenate([sa_t, zc]),
                        jnp.concatenate([sb_t, zc])])
    if lane_off == 0 and LANE % rot_dim == 0:
        tables = jnp.tile(tables[..., :rot_dim], (1, 1, LANE // rot_dim))
    return tables


def _tile_lane(v):
    return jnp.tile(v, LANE // v.shape[0]).reshape(1, LANE)


def kernel(x, c, ctx, c_ctx, w_ada, b_ada, norm_mix, norm_ffn, w_in, q_norm_gqa, k_norm_gqa,
           q_lora_norm, kv_lora_norm, w_uq, w_ukv, rpb, w_o_na, w_o_gqa, w_o_mla, w_out,
           w_ffn_gu, w_ffn_dn, w_router, w_moe_gu, w_moe_dn, norm_final):
    b, s_len, d = x.shape
    c_len = ctx.shape[1]
    t_len = s_len + c_len
    depth = w_ada.shape[0]
    n_rows = s_len // GRID_W
    n_tok = b * t_len
    n_tiles = n_tok // TOKEN_TILE

    c_rows = jnp.zeros((16, d), F32).at[:b].set(c).at[b].set(c_ctx)
    mods = _ada_call(c_rows, w_ada, b_ada)
    rope_g = _rope_tables(s_len, c_len, HEAD_DIM, 0)
    rope_m = _rope_tables(s_len, c_len, MLA_ROPE_DIM, MLA_NOPE_DIM)
    na_bias = _na_bias_tables(rpb, n_rows)

    xc = jnp.concatenate([x, ctx], axis=1)
    for i in range(depth):
        lat = mods[i, :b].reshape(b, 1, 6, d)
        cm = jnp.broadcast_to(mods[i, b].reshape(1, 1, 6, d), (b, 1, 6, d))
        mod = jnp.concatenate([lat, cm], axis=1).reshape(2 * b, 6, d)

        w_cat, wuq, wukv = _layer_weights(w_in[i], w_uq[i], w_ukv[i])
        q_na, k_na, v_na, q_g, k_g, v_g, q_m, k_m, v_m, gate, norms = _mixin_call(
            xc, mod, norm_mix[i].reshape(1, d), w_cat, _tile_lane(q_norm_gqa[i]), _tile_lane(k_norm_gqa[i]),
            q_lora_norm[i].reshape(1, -1), kv_lora_norm[i].reshape(1, -1), wuq, wukv, rope_g, rope_m)

        y_na = _na_call(q_na, k_na, v_na, na_bias[i], s_len, c_len)
        nmax = jnp.max(norms, axis=1)
        y_g = _bounded_attention(
            q_g, k_g, v_g, nmax[:, _N_QG:_N_QG + GQA_Q_HEADS], nmax[:, _N_KG:_N_KG + GQA_KV_HEADS], HEAD_DIM,
            n_kv=GQA_KV_HEADS, group=GQA_GROUP, tq=256, q_width=HEAD_DIM, s_len=s_len, c_len=c_len,
            name="gqa_attn")
        y_m = _bounded_attention(
            q_m, k_m, v_m, nmax[:, _N_QM:_N_QM + MLA_HEADS], nmax[:, _N_KM:_N_KM + MLA_HEADS],
            MLA_NOPE_DIM + MLA_ROPE_DIM, n_kv=MLA_HEADS, group=1, tq=512, q_width=LANE, s_len=s_len,
            c_len=c_len, name="mla_attn")
        xc = _merge_call(xc, mod, y_na, y_g, y_m, gate, w_o_na[i].astype(BF16), w_o_gqa[i].astype(BF16),
                         w_o_mla[i].astype(BF16), w_out[i].astype(BF16))

        j = i // 2
        gn = norm_ffn[i].reshape(1, d)
        if i % 2 == 0:
            f_dim = w_ffn_dn.shape[1]
            xc = _ffn_call(xc, mod, gn, w_ffn_gu[j][:, :f_dim].astype(BF16),
                           w_ffn_gu[j][:, f_dim:].astype(BF16), w_ffn_dn[j].astype(BF16))
        else:
            n_exp = w_router.shape[-1]
            wr = jnp.pad(w_router[j], ((0, 0), (0, LANE - n_exp)))
            idx, wts, cnt = _router_call(xc, mod, gn, wr, n_exp)
            counts = cnt[0, :n_exp].astype(jnp.int32)
            padded = (counts + MOE_ROW_BLOCK - 1) // MOE_ROW_BLOCK * MOE_ROW_BLOCK
            pad_end = jnp.cumsum(padded)
            pad_start = pad_end - padded
            idx = idx.reshape(n_tok, LANE)
            dest = pad_start[idx[:, 0:TOP_K]] + idx[:, TOP_K:2 * TOP_K]
            dest_blocks = dest.reshape(n_tiles, 1, TOP_K * TOKEN_TILE)
            n_blocks = -(-(n_tok * TOP_K + n_exp * (MOE_ROW_BLOCK - 1)) // MOE_ROW_BLOCK)
            starts = jnp.arange(n_blocks) * MOE_ROW_BLOCK
            block_expert = jnp.minimum(jnp.sum(starts[:, None] >= pad_end[None, :], axis=-1),
                                       n_exp - 1).astype(jnp.int32)
            n_used = (pad_end[-1] // MOE_ROW_BLOCK).astype(jnp.int32).reshape(1)
            xs = _dispatch_call(dest_blocks, xc, mod, gn, jnp.zeros((n_blocks * MOE_ROW_BLOCK, d), F32))
            y_rows = _moe_call(block_expert, n_used, xs, w_moe_gu[j].astype(BF16), w_moe_dn[j].astype(BF16))
            xc = _combine_call(dest_blocks, xc, mod, wts, y_rows)
    return _final_call(xc, norm_final.reshape(1, d), s_len)
```

```python
import functools

import numpy as np
import jax
import jax.numpy as jnp
from jax import lax
from jax.experimental import pallas as pl
from jax.experimental.pallas import tpu as pltpu

GRID_W = 64
HEAD_DIM = 64
NA_HEADS = 4
WIN_H = 8
WIN_W = 16
GQA_Q_HEADS = 8
GQA_KV_HEADS = 2
GQA_GROUP = GQA_Q_HEADS // GQA_KV_HEADS
MLA_HEADS = 4
MLA_Q_RANK = 256
MLA_KV_RANK = 128
MLA_NOPE_DIM = 64
MLA_ROPE_DIM = 32
MLA_V_DIM = 64
ROPE_THETA = 10000.0
TOP_K = 2
NORM_EPS = 1e-6
NA_SCALE = HEAD_DIM ** -0.5
GQA_SCALE = HEAD_DIM ** -0.5
MLA_SCALE = (MLA_NOPE_DIM + MLA_ROPE_DIM) ** -0.5
LOG2_E = 1.4426950408889634

LANE = 128
TOKEN_TILE = 256
NA_Q_ROWS = 4
NA_K_ROWS = NA_Q_ROWS + WIN_H
KV_CHUNK = 512
MOE_ROW_BLOCK = 512
VMEM_LIMIT = 56 * 1024 * 1024
NEG_BIG = -1e30
SAFE_LOG2_RANGE = 80.0
NORM_SLACK = 1.02
_N_QG, _N_KG = 0, GQA_Q_HEADS
_N_QM, _N_KM = _N_KG + GQA_KV_HEADS, _N_KG + GQA_KV_HEADS + MLA_HEADS

F32 = jnp.float32
BF16 = jnp.bfloat16


def _params(sem, vmem=VMEM_LIMIT):
    return pltpu.CompilerParams(dimension_semantics=sem, vmem_limit_bytes=vmem)


def _resident(shape):
    zeros = (0,) * len(shape)
    return pl.BlockSpec(shape, lambda *_: zeros, pipeline_mode=pl.Buffered(1))


def _rms(x, n):
    ss = jnp.sum(x * x, axis=-1, keepdims=True)
    return x * lax.rsqrt(ss * (1.0 / n) + NORM_EPS)


def _dot(a, b):
    return jnp.dot(a, b, preferred_element_type=F32)


def _dot_nt(a, b):
    return lax.dot_general(a, b, (((1,), (1,)), ((), ())), preferred_element_type=F32)


def _silu(x):
    return x * jax.nn.sigmoid(x)


def _ada_kernel(c_ref, w_ref, b_ref, o_ref):
    s = _silu(c_ref[...])
    o_ref[0] = jnp.dot(s, w_ref[0], precision=lax.Precision.HIGHEST,
                       preferred_element_type=F32) + b_ref[0]


def _ada_call(c_rows, w_ada, b_ada):
    depth, d, n = w_ada.shape
    rows = c_rows.shape[0]
    tn = 1536 if n % 1536 == 0 else n
    return pl.pallas_call(
        _ada_kernel,
        out_shape=jax.ShapeDtypeStruct((depth, rows, n), F32),
        grid=(depth, n // tn),
        in_specs=[
            pl.BlockSpec((rows, d), lambda i, j: (0, 0)),
            pl.BlockSpec((1, d, tn), lambda i, j: (i, 0, j)),
            pl.BlockSpec((1, 1, tn), lambda i, j: (i, 0, j)),
        ],
        out_specs=pl.BlockSpec((1, rows, tn), lambda i, j: (i, 0, j)),
        compiler_params=_params(("arbitrary", "arbitrary")),
        name="ada_mod",
    )(c_rows, w_ada, b_ada.reshape(depth, 1, n))


def _rope(xh, cos, sa, sb, shift):
    return xh * cos + pltpu.roll(xh, LANE - shift, 1) * sa + pltpu.roll(xh, shift, 1) * sb


_W_NA = 3 * NA_HEADS * HEAD_DIM
_W_QG = GQA_Q_HEADS * HEAD_DIM
_W_KVG = 2 * GQA_KV_HEADS * LANE
_W_LORA = MLA_Q_RANK + MLA_KV_RANK + LANE
_C_NA = 0
_C_QG = _C_NA + _W_NA
_C_KVG = _C_QG + _W_QG
_C_LORA = _C_KVG + _W_KVG
_C_GATE = _C_LORA + _W_LORA


def _mixin_kernel(x_ref, mod_ref, gmix_ref, w_ref, qn_ref, kn_ref, qln_ref, kvln_ref,
                  wuq_ref, wukv_ref, rg_ref, rm_ref,
                  qna_ref, kna_ref, vna_ref, qg_ref, kg_ref, vg_ref, qm_ref, km_ref, vm_ref,
                  gate_ref, nrm_ref):
    x = x_ref[0]
    d = x.shape[-1]
    tm = x.shape[0]
    mod = mod_ref[0]
    h = _rms(x, d) * gmix_ref[...]
    h = h * (1.0 + mod[1:2]) + mod[0:1]
    hb = h.astype(BF16)

    def proj(lo, width):
        return _dot(hb, w_ref[:, lo:lo + width])

    def block(a, j):
        return a[:, j * LANE:(j + 1) * LANE]

    nw = NA_HEADS * HEAD_DIM
    p_na = proj(_C_NA, _W_NA)
    qna_ref[0] = (p_na[:, 0:nw] * (NA_SCALE * LOG2_E)).astype(BF16)
    kna_ref[0] = p_na[:, nw:2 * nw].astype(BF16)
    vna_ref[0] = p_na[:, 2 * nw:3 * nw].astype(BF16)

    lane = lax.broadcasted_iota(jnp.int32, (tm, LANE), 1)
    low = lane < HEAD_DIM
    ones_pad = jnp.where(low, 0.0, 1.0)
    norms = []

    def unit_at(pos):
        return jnp.where(lane == pos, 1.0, 0.0)

    cos_g, sa_g, sb_g = rg_ref[0], rg_ref[1], rg_ref[2]
    cos_m, sa_m, sb_m = rm_ref[0], rm_ref[1], rm_ref[2]
    g_half = HEAD_DIM // 2
    m_half = MLA_ROPE_DIM // 2

    p_qg = proj(_C_QG, _W_QG)
    for j in range(GQA_Q_HEADS // 2):
        xp = block(p_qg, j)
        sq = xp * xp
        ss_lo = jnp.sum(jnp.where(low, sq, 0.0), axis=-1, keepdims=True)
        ss_hi = jnp.sum(jnp.where(low, 0.0, sq), axis=-1, keepdims=True)
        inv = jnp.where(low, lax.rsqrt(ss_lo * (1.0 / HEAD_DIM) + NORM_EPS),
                        lax.rsqrt(ss_hi * (1.0 / HEAD_DIM) + NORM_EPS))
        xp = _rope(xp * inv * qn_ref[...], cos_g, sa_g, sb_g, g_half) * (GQA_SCALE * LOG2_E)
        qg_ref[0, :, j * LANE:(j + 1) * LANE] = xp.astype(BF16)
        sq = xp * xp
        norms.append((_N_QG + 2 * j, jnp.sum(jnp.where(low, sq, 0.0), axis=-1, keepdims=True)))
        norms.append((_N_QG + 2 * j + 1, jnp.sum(jnp.where(low, 0.0, sq), axis=-1, keepdims=True)))
    p_kvg = proj(_C_KVG, _W_KVG)
    for hh in range(GQA_KV_HEADS):
        xh = _rope(_rms(block(p_kvg, hh), HEAD_DIM) * kn_ref[...], cos_g, sa_g, sb_g, g_half)
        norms.append((_N_KG + hh, jnp.sum(xh * xh, axis=-1, keepdims=True)))
        kg_ref[0, :, hh * LANE:(hh + 1) * LANE] = (xh + unit_at(HEAD_DIM)).astype(BF16)
        vh = block(p_kvg, GQA_KV_HEADS + hh) + ones_pad
        vg_ref[0, :, hh * LANE:(hh + 1) * LANE] = vh.astype(BF16)

    p_lora = proj(_C_LORA, _W_LORA)
    cq = (_rms(p_lora[:, 0:MLA_Q_RANK], MLA_Q_RANK) * qln_ref[...]).astype(BF16)
    q_up = _dot(cq, wuq_ref[...])
    ckv = p_lora[:, MLA_Q_RANK:MLA_Q_RANK + MLA_KV_RANK]
    ckv = (_rms(ckv, MLA_KV_RANK) * kvln_ref[...]).astype(BF16)
    kv_up = _dot(ckv, wukv_ref[...])
    k_rope = _rope(p_lora[:, MLA_Q_RANK + MLA_KV_RANK:], cos_m, sa_m, sb_m, m_half)
    for hh in range(MLA_HEADS):
        sl = slice(hh * LANE, (hh + 1) * LANE)
        qh = _rope(block(q_up, hh), cos_m, sa_m, sb_m, m_half) * (MLA_SCALE * LOG2_E)
        qm_ref[0, :, sl] = qh.astype(BF16)
        kh = block(kv_up, hh) + k_rope
        norms.append((_N_QM + hh, jnp.sum(qh * qh, axis=-1, keepdims=True)))
        norms.append((_N_KM + hh, jnp.sum(kh * kh, axis=-1, keepdims=True)))
        km_ref[0, :, sl] = (kh + unit_at(MLA_NOPE_DIM + MLA_ROPE_DIM)).astype(BF16)
        vm_ref[0, :, sl] = (block(kv_up, MLA_HEADS + hh) + ones_pad).astype(BF16)

    packed = jnp.zeros((tm, LANE), F32)
    for pos, val in norms:
        packed = jnp.where(lane == pos, val, packed)
    nrm_ref[0] = packed

    for j in range(3):
        gate_ref[0, :, j * d:(j + 1) * d] = jax.nn.sigmoid(proj(_C_GATE + j * d, d)).astype(BF16)


def _mixin_call(xc, mod, gmix, w_cat, qn, kn, qln, kvln, wuq, wukv, rope_g, rope_m):
    b, t, d = xc.shape
    tm = TOKEN_TILE
    nt = t // tm
    tok = lambda w: pl.BlockSpec((1, tm, w), lambda ti, bi: (bi, ti, 0))
    widths = [NA_HEADS * HEAD_DIM] * 3 + [GQA_Q_HEADS * HEAD_DIM, GQA_KV_HEADS * LANE,
                                          GQA_KV_HEADS * LANE, MLA_HEADS * LANE,
                                          MLA_HEADS * LANE, MLA_HEADS * LANE, 3 * d]
    return pl.pallas_call(
        _mixin_kernel,
        out_shape=[jax.ShapeDtypeStruct((b, t, w), BF16) for w in widths]
        + [jax.ShapeDtypeStruct((b, t, LANE), F32)],
        grid=(nt, b),
        in_specs=[
            tok(d),
            pl.BlockSpec((1, 6, d), lambda ti, bi: (2 * bi + (ti == nt - 1).astype(jnp.int32), 0, 0)),
            _resident((1, d)),
            _resident(w_cat.shape),
            _resident((1, LANE)), _resident((1, LANE)),
            _resident((1, MLA_Q_RANK)), _resident((1, MLA_KV_RANK)),
            _resident(wuq.shape), _resident(wukv.shape),
            pl.BlockSpec((3, tm, LANE), lambda ti, bi: (0, ti, 0)),
            pl.BlockSpec((3, tm, LANE), lambda ti, bi: (0, ti, 0)),
        ],
        out_specs=[tok(w) for w in widths] + [tok(LANE)],
        compiler_params=_params(("arbitrary", "arbitrary")),
        name="mixer_in",
    )(xc, mod, gmix, w_cat, qn, kn, qln, kvln, wuq, wukv, rope_g, rope_m)


def _na_kernel(q_ref, k_ref, v_ref, bias_ref, o_ref, *, n_groups, n_rows, s_len, c_len):
    g = pl.program_id(1)
    qn = NA_Q_ROWS * GRID_W
    kn = NA_K_ROWS * GRID_W
    lane = lax.broadcasted_iota(jnp.int32, (qn, LANE), 1)
    low = lane < HEAD_DIM

    def attend_pairs(k_nb, v_nb):
        k_c = k_ref[0, s_len:s_len + c_len, :]
        v_c = v_ref[0, s_len:s_len + c_len, :]
        for pair in range(NA_HEADS // 2):
            sl = slice(pair * LANE, (pair + 1) * LANE)
            qp = q_ref[0, :, sl]
            outs = []
            for sub in range(2):
                head = 2 * pair + sub
                qm = jnp.where(low if sub == 0 else jnp.logical_not(low), qp, jnp.zeros_like(qp))
                s_c = _dot_nt(qm, k_c[:, sl])
                m = jnp.max(s_c, axis=-1, keepdims=True)
                if k_nb is not None:
                    s_nb = _dot_nt(qm, k_nb[:, sl]) + bias_ref[0, head]
                    m = jnp.maximum(m, jnp.max(s_nb, axis=-1, keepdims=True))
                p_c = jnp.exp2(s_c - m)
                den = jnp.sum(p_c, axis=-1, keepdims=True)
                o = _dot(p_c.astype(BF16), v_c[:, sl])
                if k_nb is not None:
                    p_nb = jnp.exp2(s_nb - m)
                    den = den + jnp.sum(p_nb, axis=-1, keepdims=True)
                    o = o + _dot(p_nb.astype(BF16), v_nb[:, sl])
                outs.append(o / den)
            o_ref[0, :, sl] = jnp.where(low, outs[0], outs[1]).astype(BF16)

    @pl.when(g < n_groups)
    def _():
        first_row = jnp.clip(NA_Q_ROWS * g - WIN_H // 2, 0, n_rows - NA_K_ROWS)
        start = pl.multiple_of(first_row * GRID_W, GRID_W)
        attend_pairs(k_ref[0, pl.ds(start, kn), :], v_ref[0, pl.ds(start, kn), :])

    @pl.when(g == n_groups)
    def _():
        attend_pairs(None, None)


def _na_call(q, k, v, bias, s_len, c_len):
    b, t, w = q.shape
    qn = NA_Q_ROWS * GRID_W
    kn = NA_K_ROWS * GRID_W
    n_rows = s_len // GRID_W
    n_groups = s_len // qn
    assert c_len == qn and n_rows >= NA_K_ROWS

    def variant(g):
        return jnp.where(g == 0, 0, jnp.where(g >= n_groups - 1, 2, 1))

    return pl.pallas_call(
        functools.partial(_na_kernel, n_groups=n_groups, n_rows=n_rows, s_len=s_len, c_len=c_len),
        out_shape=jax.ShapeDtypeStruct((b, t, w), BF16),
        grid=(b, n_groups + 1),
        in_specs=[
            pl.BlockSpec((1, qn, w), lambda bi, g: (bi, g, 0)),
            pl.BlockSpec((1, t, w), lambda bi, g: (bi, 0, 0)),
            pl.BlockSpec((1, t, w), lambda bi, g: (bi, 0, 0)),
            pl.BlockSpec((1, NA_HEADS, qn, kn), lambda bi, g: (variant(g), 0, 0, 0)),
        ],
        out_specs=pl.BlockSpec((1, qn, w), lambda bi, g: (bi, g, 0)),
        compiler_params=_params(("arbitrary", "arbitrary")),
        name="na_attn",
    )(q, k, v, bias)


def _na_bias_tables(rpb, n_rows):
    qi = np.arange(NA_Q_ROWS)[:, None]
    kj = np.arange(NA_K_ROWS)[None, :]
    qc = np.arange(GRID_W)[:, None]
    kc = np.arange(GRID_W)[None, :]
    col0 = np.clip(qc - WIN_W // 2, 0, GRID_W - WIN_W)
    col_ok = (kc >= col0) & (kc < col0 + WIN_W)
    dc = np.clip(kc - qc, -(WIN_W - 1), WIN_W - 1) + WIN_W - 1
    col_sel = np.eye(2 * WIN_W - 1, dtype=np.float32)[dc]
    row_sel, valid = [], []
    for r0, u0 in ((0, 0), (WIN_H // 2, 0), (n_rows - NA_Q_ROWS, n_rows - NA_K_ROWS)):
        qr = r0 + qi
        kr = u0 + kj
        row0 = np.clip(qr - WIN_H // 2, 0, n_rows - WIN_H)
        row_ok = (kr >= row0) & (kr < row0 + WIN_H)
        dr = np.clip(kr - qr, -(WIN_H - 1), WIN_H - 1) + WIN_H - 1
        row_sel.append(np.eye(2 * WIN_H - 1, dtype=np.float32)[dr])
        valid.append(row_ok[:, None, :, None] & col_ok[None, :, None, :])
    row_sel = jnp.asarray(np.stack(row_sel))
    valid = np.stack(valid).reshape(3, 1, NA_Q_ROWS * GRID_W, NA_K_ROWS * GRID_W)
    picked = jnp.einsum('vija,lhab,qkb->lvhiqjk', row_sel, rpb.astype(F32) * LOG2_E, jnp.asarray(col_sel),
                        precision=lax.Precision.HIGHEST)
    picked = picked.reshape(rpb.shape[0], 3, NA_HEADS, NA_Q_ROWS * GRID_W, NA_K_ROWS * GRID_W)
    return jnp.where(valid[None], picked, NEG_BIG)


def _row_max(s):
    parts = [s[:, j * LANE:(j + 1) * LANE] for j in range(s.shape[1] // LANE)]
    while len(parts) > 1:
        parts = [jnp.maximum(parts[j], parts[j + 1]) for j in range(0, len(parts) - 1, 2)] + parts[len(parts) & ~1:]
    return jnp.max(parts[0], axis=-1, keepdims=True)


def _flash_kernel(q_ref, k_ref, v_ref, u_ref, o_ref, *, n_kv, group, tq, q_width, s_len, c_len,
                  with_latent, online):
    low = lax.broadcasted_iota(jnp.int32, (tq, LANE), 1) < HEAD_DIM
    heads = range(n_kv)
    lanes = [slice(h * LANE, (h + 1) * LANE) for h in heads]

    def load_q(head):
        if q_width == LANE:
            qh = q_ref[0, :, head * LANE:(head + 1) * LANE]
        else:
            pair = q_ref[0, :, (head // 2) * LANE:(head // 2 + 1) * LANE]
            if head % 2:
                pair = pltpu.roll(pair.astype(F32), HEAD_DIM, 1).astype(BF16)
            qh = jnp.where(low, pair, jnp.zeros_like(pair))
        return qh if online else qh + u_ref[0, head:head + 1, :]

    qs = []
    for h in heads:
        parts = [load_q(h * group + g) for g in range(group)]
        qs.append(parts[0] if group == 1 else jnp.concatenate(parts, axis=0))

    if online:
        carry = []
        for h in heads:
            s = _dot_nt(qs[h], k_ref[0, s_len:s_len + c_len, lanes[h]])
            m = _row_max(s)
            carry += [m, _dot(jnp.exp2(s - m).astype(BF16), v_ref[0, s_len:s_len + c_len, lanes[h]])]
        if with_latent:
            def body(i, carry):
                st = pl.multiple_of(i * KV_CHUNK, KV_CHUNK)
                new = []
                for h in heads:
                    m_old, acc_old = carry[2 * h], carry[2 * h + 1]
                    sc = _dot_nt(qs[h], k_ref[0, pl.ds(st, KV_CHUNK), lanes[h]])
                    m_new = jnp.maximum(m_old, _row_max(sc))
                    p = jnp.exp2(sc - m_new).astype(BF16)
                    pv = _dot(p, v_ref[0, pl.ds(st, KV_CHUNK), lanes[h]])
                    new += [m_new, jnp.exp2(m_old - m_new) * acc_old + pv]
                return tuple(new)
            carry = lax.fori_loop(0, s_len // KV_CHUNK, body, tuple(carry), unroll=2)
        accs = [carry[2 * h + 1] for h in heads]
    else:
        chunks = [(s_len, c_len)]
        if with_latent:
            chunks += [(i * KV_CHUNK, KV_CHUNK) for i in range(s_len // KV_CHUNK)]
        accs = [None] * n_kv
        for st, size in chunks:
            for h in heads:
                p = jnp.exp2(_dot_nt(qs[h], k_ref[0, st:st + size, lanes[h]])).astype(BF16)
                pv = _dot(p, v_ref[0, st:st + size, lanes[h]])
                accs[h] = pv if accs[h] is None else accs[h] + pv
    outs = []
    for h in heads:
        acc = accs[h]
        o = acc / pltpu.roll(acc, HEAD_DIM, 1)
        for g in range(group):
            outs.append(o[g * tq:(g + 1) * tq])
    for pair in range(len(outs) // 2):
        packed = jnp.where(low, outs[2 * pair], pltpu.roll(outs[2 * pair + 1], HEAD_DIM, 1))
        o_ref[0, :, pair * LANE:(pair + 1) * LANE] = packed.astype(BF16)


def _flash_call(q, k, v, shift, *, n_kv, group, tq, q_width, s_len, c_len, online, name):
    b, t, _ = q.shape
    n_heads = n_kv * group
    tq_ctx = min(tq, c_len)
    assert s_len % tq == 0 and c_len % tq_ctx == 0 and s_len % tq_ctx == 0 and s_len % KV_CHUNK == 0
    kv_spec = pl.BlockSpec((1, t, n_kv * LANE), lambda bi, ti: (bi, 0, 0))
    u_spec = pl.BlockSpec((1, n_heads, LANE), lambda bi, ti: (bi, 0, 0))
    common = dict(n_kv=n_kv, group=group, q_width=q_width, s_len=s_len, c_len=c_len, online=online)
    width = n_heads * HEAD_DIM
    tag = "_online" if online else ""
    y_lat = pl.pallas_call(
        functools.partial(_flash_kernel, tq=tq, with_latent=True, **common),
        out_shape=jax.ShapeDtypeStruct((b, s_len, width), BF16),
        grid=(b, s_len // tq),
        in_specs=[pl.BlockSpec((1, tq, n_heads * q_width), lambda bi, ti: (bi, ti, 0)), kv_spec, kv_spec, u_spec],
        out_specs=pl.BlockSpec((1, tq, width), lambda bi, ti: (bi, ti, 0)),
        compiler_params=_params(("arbitrary", "arbitrary")),
        name=name + tag,
    )(q, k, v, shift)
    first = s_len // tq_ctx
    y_ctx = pl.pallas_call(
        functools.partial(_flash_kernel, tq=tq_ctx, with_latent=False, **common),
        out_shape=jax.ShapeDtypeStruct((b, c_len, width), BF16),
        grid=(b, c_len // tq_ctx),
        in_specs=[pl.BlockSpec((1, tq_ctx, n_heads * q_width), lambda bi, ti: (bi, first + ti, 0)), kv_spec, kv_spec,
                  u_spec],
        out_specs=pl.BlockSpec((1, tq_ctx, width), lambda bi, ti: (bi, ti, 0)),
        compiler_params=_params(("arbitrary", "arbitrary")),
        name=name + "_ctx" + tag,
    )(q, k, v, shift)
    return y_lat, y_ctx


def _bounded_attention(q, k, v, q_norm2, k_norm2, unit_lane, **kw):
    n_kv, group = kw["n_kv"], kw["group"]
    k_norm2 = jnp.repeat(k_norm2, group, axis=1)
    bound = jnp.sqrt(q_norm2 * k_norm2) * NORM_SLACK
    lane = jnp.arange(LANE) == unit_lane
    shift = jnp.where(lane[None, None, :], -bound[:, :, None], 0.0).astype(BF16)
    safe = jnp.max(bound) * 2.0 < SAFE_LOG2_RANGE
    return lax.cond(safe,
                    lambda ops: _flash_call(*ops, online=False, **kw),
                    lambda ops: _flash_call(*ops, online=True, **kw),
                    (q, k, v, shift))


def _mod_spec(d, nt):
    return pl.BlockSpec((1, 6, d), lambda bi, ti: (2 * bi + (ti == nt - 1).astype(jnp.int32), 0, 0))


def _merged_residual(x_ref, mod_ref, yna_ref, yg_ref, ygc_ref, ym_ref, ymc_ref, gate_ref, wna_ref, wg_ref,
                     wm_ref, wout_ref, n_lat_tiles):
    d = x_ref.shape[-1]
    is_ctx = pl.program_id(1) >= n_lat_tiles
    y_g = jnp.where(is_ctx, ygc_ref[0], yg_ref[0])
    y_m = jnp.where(is_ctx, ymc_ref[0], ym_ref[0])
    m = gate_ref[0, :, 0:d].astype(F32) * _dot(yna_ref[0], wna_ref[...])
    m = m + gate_ref[0, :, d:2 * d].astype(F32) * _dot(y_g, wg_ref[...])
    m = m + gate_ref[0, :, 2 * d:3 * d].astype(F32) * _dot(y_m, wm_ref[...])
    r = _dot(m.astype(BF16), wout_ref[...])
    return x_ref[0] + mod_ref[0][2:3] * r


def _norm2(x, mod, gain):
    h = _rms(x, x.shape[-1]) * gain
    return h * (1.0 + mod[4:5]) + mod[3:4]


def _merge_ffn_kernel(*refs, n_lat_tiles):
    mix_refs, (gn_ref, wg_ref, wu_ref, wd_ref, o_ref) = refs[:12], refs[12:]
    x = _merged_residual(*mix_refs, n_lat_tiles)
    mod = mix_refs[1][0]
    hb = _norm2(x, mod, gn_ref[...]).astype(BF16)
    act = (_silu(_dot(hb, wg_ref[...])) * _dot(hb, wu_ref[...])).astype(BF16)
    o_ref[0] = x + mod[5:6] * _dot(act, wd_ref[...])


def _merge_router_kernel(*refs, n_lat_tiles, n_experts):
    mix_refs, (gn_ref, wr_ref, o_ref, idx_ref, wt_ref, cnt_ref, carry_ref) = refs[:12], refs[12:]

    @pl.when((pl.program_id(0) == 0) & (pl.program_id(1) == 0))
    def _():
        carry_ref[...] = jnp.zeros_like(carry_ref)

    x = _merged_residual(*mix_refs, n_lat_tiles)
    o_ref[0] = x
    tm = x.shape[0]
    h = _norm2(x, mix_refs[1][0], gn_ref[...])
    logits = jnp.dot(h, wr_ref[...], precision=lax.Precision.HIGHEST, preferred_element_type=F32)
    lane = lax.broadcasted_iota(jnp.int32, (tm, LANE), 1).astype(F32)
    logits = jnp.where(lane < n_experts, logits, -jnp.inf)
    m1 = jnp.max(logits, axis=-1, keepdims=True)
    i1 = jnp.min(jnp.where(logits == m1, lane, float(LANE)), axis=-1, keepdims=True)
    rest = jnp.where(lane == i1, -jnp.inf, logits)
    m2 = jnp.max(rest, axis=-1, keepdims=True)
    i2 = jnp.min(jnp.where(rest == m2, lane, float(LANE)), axis=-1, keepdims=True)
    e2 = jnp.exp(m2 - m1)
    w1 = 1.0 / (1.0 + e2)
    w2 = e2 / (1.0 + e2)
    hot1 = lane == i1
    hot2 = lane == i2
    hot = jnp.where(hot1 | hot2, 1.0, 0.0)
    rows = lax.broadcasted_iota(jnp.int32, (tm, tm), 0)
    cols = lax.broadcasted_iota(jnp.int32, (tm, tm), 1)
    below = jnp.where(rows > cols, 1.0, 0.0).astype(BF16)
    before = _dot(below, hot.astype(BF16)) + carry_ref[0:1, :]
    r1 = jnp.sum(jnp.where(hot1, before, 0.0), axis=-1, keepdims=True)
    r2 = jnp.sum(jnp.where(hot2, before, 0.0), axis=-1, keepdims=True)
    total = carry_ref[0:1, :] + jnp.sum(hot, axis=0, keepdims=True)
    carry_ref[...] = jnp.broadcast_to(total, carry_ref.shape)
    cnt_ref[...] = jnp.broadcast_to(total, cnt_ref.shape)
    packed = jnp.where(lane == 0, i1, jnp.where(lane == 1, i2, jnp.where(lane == 2, r1, jnp.where(lane == 3, r2, 0.0))))
    idx_ref[0] = packed.astype(jnp.int32)
    wt_ref[0] = jnp.where(lane == 0, w1, jnp.where(lane == 1, w2, 0.0))


def _merge_call(xc, mod, y_na, y_g, y_m, gate, w_na, w_g, w_m, w_out, gn, *, ffn=None, router=None):
    b, t, d = xc.shape
    tm = TOKEN_TILE
    nt = t // tm
    n_lat = y_g[0].shape[1] // tm
    tok = lambda w: pl.BlockSpec((1, tm, w), lambda bi, ti: (bi, ti, 0))
    lat = lambda w: pl.BlockSpec((1, tm, w), lambda bi, ti: (bi, jnp.minimum(ti, n_lat - 1), 0))
    ctx = lambda w: pl.BlockSpec((1, tm, w), lambda bi, ti: (bi, jnp.maximum(ti - n_lat, 0), 0))
    wg_, wm_ = y_g[0].shape[-1], y_m[0].shape[-1]
    mix_specs = [tok(d), _mod_spec(d, nt), tok(y_na.shape[-1]), lat(wg_), ctx(wg_), lat(wm_), ctx(wm_),
                 tok(3 * d), _resident(w_na.shape), _resident(w_g.shape), _resident(w_m.shape),
                 _resident(w_out.shape), _resident((1, d))]
    mix_args = (xc, mod, y_na, y_g[0], y_g[1], y_m[0], y_m[1], gate, w_na, w_g, w_m, w_out, gn)
    x_shape = jax.ShapeDtypeStruct(xc.shape, F32)
    if ffn is not None:
        return pl.pallas_call(
            functools.partial(_merge_ffn_kernel, n_lat_tiles=n_lat),
            out_shape=x_shape,
            grid=(b, nt),
            in_specs=mix_specs + [_resident(w.shape) for w in ffn],
            out_specs=tok(d),
            input_output_aliases={0: 0},
            compiler_params=_params(("arbitrary", "arbitrary")),
            name="merge_ffn",
        )(*mix_args, *ffn)
    w_router_pad, n_experts = router
    return pl.pallas_call(
        functools.partial(_merge_router_kernel, n_lat_tiles=n_lat, n_experts=n_experts),
        out_shape=[x_shape, jax.ShapeDtypeStruct((b, t, LANE), jnp.int32),
                   jax.ShapeDtypeStruct((b, t, LANE), F32), jax.ShapeDtypeStruct((8, LANE), F32)],
        grid=(b, nt),
        in_specs=mix_specs + [_resident(w_router_pad.shape)],
        out_specs=[tok(d), tok(LANE), tok(LANE), pl.BlockSpec((8, LANE), lambda bi, ti: (0, 0))],
        scratch_shapes=[pltpu.VMEM((8, LANE), F32)],
        input_output_aliases={0: 0},
        compiler_params=_params(("arbitrary", "arbitrary")),
        name="merge_router",
    )(*mix_args, w_router_pad)


def _row_copy(src, src_row, dst, dst_row, sem):
    return pltpu.make_async_copy(src.at[pl.ds(src_row, 1)], dst.at[pl.ds(dst_row, 1)], sem)


def _dispatch_kernel(dest_ref, x_ref, mod_ref, gn_ref, xs_in_ref, xs_ref, h_buf, sem):
    del xs_in_ref
    tm = x_ref.shape[1]
    h_buf[...] = _norm2(x_ref[0], mod_ref[0], gn_ref[...])

    def issue(i, carry):
        for k in range(TOP_K):
            _row_copy(h_buf, i, xs_ref, dest_ref[0, 0, TOP_K * i + k], sem).start()
        return carry

    lax.fori_loop(0, tm, issue, 0, unroll=8)

    def drain(i, carry):
        for k in range(TOP_K):
            _row_copy(h_buf, 0, xs_ref, 0, sem).wait()
        return carry

    lax.fori_loop(0, tm, drain, 0, unroll=8)


def _dispatch_call(dest_blocks, xc, mod, gn, xs_zero):
    b, t, d = xc.shape
    tm = TOKEN_TILE
    nt = t // tm
    return pl.pallas_call(
        _dispatch_kernel,
        out_shape=jax.ShapeDtypeStruct(xs_zero.shape, F32),
        grid=(b, nt),
        in_specs=[
            pl.BlockSpec((1, 1, dest_blocks.shape[-1]), lambda bi, ti: (bi * nt + ti, 0, 0),
                         memory_space=pltpu.SMEM),
            pl.BlockSpec((1, tm, d), lambda bi, ti: (bi, ti, 0)),
            _mod_spec(d, nt), _resident((1, d)),
            pl.BlockSpec(memory_space=pl.ANY),
        ],
        out_specs=pl.BlockSpec(memory_space=pl.ANY),
        scratch_shapes=[pltpu.VMEM((tm, d), F32), pltpu.SemaphoreType.DMA],
        input_output_aliases={4: 0},
        compiler_params=_params(("arbitrary", "arbitrary")),
        name="moe_dispatch",
    )(dest_blocks, xc, mod, gn, xs_zero)


def _moe_kernel(be_ref, nu_ref, xs_ref, wgu_ref, wd_ref, y_ref, *, f_chunk):
    f_dim = wd_ref.shape[1]

    @pl.when(pl.program_id(0) < nu_ref[0])
    def _():
        xb = xs_ref[...].astype(BF16)
        acc = None
        for c in range(0, f_dim, f_chunk):
            gate = _dot(xb, wgu_ref[0, :, c:c + f_chunk])
            up = _dot(xb, wgu_ref[0, :, f_dim + c:f_dim + c + f_chunk])
            part = _dot((_silu(gate) * up).astype(BF16), wd_ref[0, c:c + f_chunk, :])
            acc = part if acc is None else acc + part
        y_ref[...] = acc

    @pl.when(pl.program_id(0) >= nu_ref[0])
    def _():
        y_ref[...] = jnp.zeros_like(y_ref)


def _moe_call(block_expert, n_used, xs, w_gu, w_dn):
    rows, d = xs.shape
    n_exp, _, two_f = w_gu.shape
    f_dim = two_f // 2
    f_chunk = 2 * LANE if f_dim % (2 * LANE) == 0 else f_dim
    tmr = MOE_ROW_BLOCK
    grid_spec = pltpu.PrefetchScalarGridSpec(
        num_scalar_prefetch=2,
        grid=(rows // tmr,),
        in_specs=[
            pl.BlockSpec((tmr, d), lambda i, be, nu: (i, 0)),
            pl.BlockSpec((1, d, two_f), lambda i, be, nu: (be[i], 0, 0)),
            pl.BlockSpec((1, f_dim, d), lambda i, be, nu: (be[i], 0, 0)),
        ],
        out_specs=pl.BlockSpec((tmr, d), lambda i, be, nu: (i, 0)),
    )
    return pl.pallas_call(
        functools.partial(_moe_kernel, f_chunk=f_chunk),
        out_shape=jax.ShapeDtypeStruct((rows, d), F32),
        grid_spec=grid_spec,
        compiler_params=_params(("arbitrary",)),
        name="moe_experts",
    )(block_expert, n_used, xs, w_gu, w_dn)


def _combine_kernel(dest_ref, x_ref, mod_ref, wt_ref, y_ref, o_ref, buf1, buf2, sem):
    tm = x_ref.shape[1]

    def issue(i, carry):
        _row_copy(y_ref, dest_ref[0, 0, TOP_K * i], buf1, i, sem).start()
        _row_copy(y_ref, dest_ref[0, 0, TOP_K * i + 1], buf2, i, sem).start()
        return carry

    lax.fori_loop(0, tm, issue, 0, unroll=8)

    def drain(i, carry):
        _row_copy(y_ref, 0, buf1, 0, sem).wait()
        _row_copy(y_ref, 0, buf2, 0, sem).wait()
        return carry

    lax.fori_loop(0, tm, drain, 0, unroll=8)
    wt = wt_ref[0]
    mix = wt[:, 0:1] * buf1[...] + wt[:, 1:2] * buf2[...]
    o_ref[0] = x_ref[0] + mod_ref[0][5:6] * mix


def _combine_call(dest_blocks, xc, mod, wts, y_rows):
    b, t, d = xc.shape
    tm = TOKEN_TILE
    nt = t // tm
    tok = lambda w: pl.BlockSpec((1, tm, w), lambda bi, ti: (bi, ti, 0))
    return pl.pallas_call(
        _combine_kernel,
        out_shape=jax.ShapeDtypeStruct(xc.shape, F32),
        grid=(b, nt),
        in_specs=[
            pl.BlockSpec((1, 1, dest_blocks.shape[-1]), lambda bi, ti: (bi * nt + ti, 0, 0),
                         memory_space=pltpu.SMEM),
            tok(d), _mod_spec(d, nt), tok(LANE),
            pl.BlockSpec(memory_space=pl.ANY),
        ],
        out_specs=tok(d),
        scratch_shapes=[pltpu.VMEM((tm, d), F32), pltpu.VMEM((tm, d), F32), pltpu.SemaphoreType.DMA],
        input_output_aliases={1: 0},
        compiler_params=_params(("arbitrary", "arbitrary")),
        name="moe_combine",
    )(dest_blocks, xc, mod, wts, y_rows)


def _final_kernel(x_ref, g_ref, o_ref):
    x = x_ref[0]
    o_ref[0] = _rms(x, x.shape[-1]) * g_ref[...]


def _final_call(xc, gain, s_len):
    b, t, d = xc.shape
    tm = TOKEN_TILE
    tok = pl.BlockSpec((1, tm, d), lambda bi, ti: (bi, ti, 0))
    return pl.pallas_call(
        _final_kernel,
        out_shape=jax.ShapeDtypeStruct((b, s_len, d), F32),
        grid=(b, s_len // tm),
        in_specs=[tok, _resident((1, d))],
        out_specs=tok,
        compiler_params=_params(("arbitrary", "arbitrary")),
        name="final_norm",
    )(xc, gain)


def _pad_heads(w, n_heads, width):
    k = w.shape[0]
    w = w.reshape(k, n_heads, width)
    return jnp.pad(w, ((0, 0), (0, 0), (0, LANE - width))).reshape(k, n_heads * LANE)


def _layer_weights(w_in, w_uq, w_ukv):
    d = w_in.shape[0]
    nw = NA_HEADS * HEAD_DIM
    gk = GQA_KV_HEADS * HEAD_DIM
    o = 0
    k_na = w_in[:, o:o + nw]; o += nw
    v_na = w_in[:, o:o + nw]; o += nw
    k_g = w_in[:, o:o + gk]; o += gk
    v_g = w_in[:, o:o + gk]; o += gk
    c_kv = w_in[:, o:o + MLA_KV_RANK]; o += MLA_KV_RANK
    k_r = w_in[:, o:o + MLA_ROPE_DIM]; o += MLA_ROPE_DIM
    q_na = w_in[:, o:o + nw]; o += nw
    q_g = w_in[:, o:o + GQA_Q_HEADS * HEAD_DIM]; o += GQA_Q_HEADS * HEAD_DIM
    c_q = w_in[:, o:o + MLA_Q_RANK]; o += MLA_Q_RANK
    gate = w_in[:, o:]
    k_r_pad = jnp.pad(k_r, ((0, 0), (MLA_NOPE_DIM, LANE - MLA_NOPE_DIM - MLA_ROPE_DIM)))
    w_cat = jnp.concatenate([
        q_na, k_na, v_na, q_g, _pad_heads(k_g, GQA_KV_HEADS, HEAD_DIM), _pad_heads(v_g, GQA_KV_HEADS, HEAD_DIM), c_q, c_kv, k_r_pad, gate],
        axis=1).astype(BF16)
    assert w_cat.shape[1] == _C_GATE + 3 * d
    wuq = _pad_heads(w_uq, MLA_HEADS, MLA_NOPE_DIM + MLA_ROPE_DIM).astype(BF16)
    kv = w_ukv.reshape(MLA_KV_RANK, MLA_HEADS, MLA_NOPE_DIM + MLA_V_DIM)
    wuk = _pad_heads(kv[:, :, :MLA_NOPE_DIM].reshape(MLA_KV_RANK, -1), MLA_HEADS, MLA_NOPE_DIM)
    wuv = _pad_heads(kv[:, :, MLA_NOPE_DIM:].reshape(MLA_KV_RANK, -1), MLA_HEADS, MLA_V_DIM)
    return w_cat, wuq, jnp.concatenate([wuk, wuv], axis=1).astype(BF16)


def _rope_tables(s_len, c_len, rot_dim, lane_off):
    half = rot_dim // 2
    n_freq = half // 2
    pos = jnp.arange(s_len)
    rows = (pos // GRID_W).astype(F32)
    cols = (pos % GRID_W).astype(F32)
    inv = jnp.power(ROPE_THETA, -jnp.arange(n_freq, dtype=F32) / n_freq)
    ang = jnp.concatenate([rows[:, None] * inv, cols[:, None] * inv], axis=-1)
    cos, sin = jnp.cos(ang), jnp.sin(ang)
    zeros = jnp.zeros((s_len, half), F32)
    right = LANE - lane_off - rot_dim
    pad = lambda a, b: jnp.pad(jnp.concatenate([a, b], axis=-1), ((0, 0), (lane_off, right)))
    cos_t = pad(cos, cos) + jnp.pad(jnp.ones((s_len, lane_off), F32), ((0, 0), (0, LANE - lane_off)))
    sa_t = pad(-sin, zeros)
    sb_t = pad(zeros, sin)
    ident = jnp.pad(jnp.ones((c_len, lane_off + rot_dim), F32), ((0, 0), (0, right)))
    zc = jnp.zeros((c_len, LANE), F32)
    tables = jnp.stack([jnp.concatenate([cos_t, ident]), jnp.concatenate([sa_t, zc]),
                        jnp.concatenate([sb_t, zc])])
    if lane_off == 0 and LANE % rot_dim == 0:
        tables = jnp.tile(tables[..., :rot_dim], (1, 1, LANE // rot_dim))
    return tables


def _tile_lane(v):
    return jnp.tile(v, LANE // v.shape[0]).reshape(1, LANE)


def kernel(x, c, ctx, c_ctx, w_ada, b_ada, norm_mix, norm_ffn, w_in, q_norm_gqa, k_norm_gqa,
           q_lora_norm, kv_lora_norm, w_uq, w_ukv, rpb, w_o_na, w_o_gqa, w_o_mla, w_out,
           w_ffn_gu, w_ffn_dn, w_router, w_moe_gu, w_moe_dn, norm_final):
    b, s_len, d = x.shape
    c_len = ctx.shape[1]
    t_len = s_len + c_len
    depth = w_ada.shape[0]
    n_rows = s_len // GRID_W
    n_tok = b * t_len
    n_tiles = n_tok // TOKEN_TILE

    c_rows = jnp.zeros((16, d), F32).at[:b].set(c).at[b].set(c_ctx)
    mods = _ada_call(c_rows, w_ada, b_ada)
    rope_g = _rope_tables(s_len, c_len, HEAD_DIM, 0)
    rope_m = _rope_tables(s_len, c_len, MLA_ROPE_DIM, MLA_NOPE_DIM)
    na_bias = _na_bias_tables(rpb, n_rows)

    xc = jnp.concatenate([x, ctx], axis=1)
    for i in range(depth):
        lat = mods[i, :b].reshape(b, 1, 6, d)
        cm = jnp.broadcast_to(mods[i, b].reshape(1, 1, 6, d), (b, 1, 6, d))
        mod = jnp.concatenate([lat, cm], axis=1).reshape(2 * b, 6, d)

        w_cat, wuq, wukv = _layer_weights(w_in[i], w_uq[i], w_ukv[i])
        q_na, k_na, v_na, q_g, k_g, v_g, q_m, k_m, v_m, gate, norms = _mixin_call(
            xc, mod, norm_mix[i].reshape(1, d), w_cat, _tile_lane(q_norm_gqa[i]), _tile_lane(k_norm_gqa[i]),
            q_lora_norm[i].reshape(1, -1), kv_lora_norm[i].reshape(1, -1), wuq, wukv, rope_g, rope_m)

        y_na = _na_call(q_na, k_na, v_na, na_bias[i], s_len, c_len)
        nmax = jnp.max(norms, axis=1)
        y_g = _bounded_attention(
            q_g, k_g, v_g, nmax[:, _N_QG:_N_QG + GQA_Q_HEADS], nmax[:, _N_KG:_N_KG + GQA_KV_HEADS], HEAD_DIM,
            n_kv=GQA_KV_HEADS, group=GQA_GROUP, tq=256, q_width=HEAD_DIM, s_len=s_len, c_len=c_len,
            name="gqa_attn")
        y_m = _bounded_attention(
            q_m, k_m, v_m, nmax[:, _N_QM:_N_QM + MLA_HEADS], nmax[:, _N_KM:_N_KM + MLA_HEADS],
            MLA_NOPE_DIM + MLA_ROPE_DIM, n_kv=MLA_HEADS, group=1, tq=512, q_width=LANE, s_len=s_len,
            c_len=c_len, name="mla_attn")
        j = i // 2
        gn = norm_ffn[i].reshape(1, d)
        merge_args = (xc, mod, y_na, y_g, y_m, gate, w_o_na[i].astype(BF16), w_o_gqa[i].astype(BF16),
                      w_o_mla[i].astype(BF16), w_out[i].astype(BF16), gn)
        if i % 2 == 0:
            f_dim = w_ffn_dn.shape[1]
            xc = _merge_call(*merge_args, ffn=(w_ffn_gu[j][:, :f_dim].astype(BF16),
                                               w_ffn_gu[j][:, f_dim:].astype(BF16), w_ffn_dn[j].astype(BF16)))
        else:
            n_exp = w_router.shape[-1]
            wr = jnp.pad(w_router[j], ((0, 0), (0, LANE - n_exp)))
            xc, idx, wts, cnt = _merge_call(*merge_args, router=(wr, n_exp))
            counts = cnt[0, :n_exp].astype(jnp.int32)
            padded = (counts + MOE_ROW_BLOCK - 1) // MOE_ROW_BLOCK * MOE_ROW_BLOCK
            pad_end = jnp.cumsum(padded)
            pad_start = pad_end - padded
            idx = idx.reshape(n_tok, LANE)
            dest = pad_start[idx[:, 0:TOP_K]] + idx[:, TOP_K:2 * TOP_K]
            dest_blocks = dest.reshape(n_tiles, 1, TOP_K * TOKEN_TILE)
            n_blocks = -(-(n_tok * TOP_K + n_exp * (MOE_ROW_BLOCK - 1)) // MOE_ROW_BLOCK)
            starts = jnp.arange(n_blocks) * MOE_ROW_BLOCK
            block_expert = jnp.minimum(jnp.sum(starts[:, None] >= pad_end[None, :], axis=-1),
                                       n_exp - 1).astype(jnp.int32)
            n_used = (pad_end[-1] // MOE_ROW_BLOCK).astype(jnp.int32).reshape(1)
            xs = _dispatch_call(dest_blocks, xc, mod, gn, jnp.zeros((n_blocks * MOE_ROW_BLOCK, d), F32))
            y_rows = _moe_call(block_expert, n_used, xs, w_moe_gu[j].astype(BF16), w_moe_dn[j].astype(BF16))
            xc = _combine_call(dest_blocks, xc, mod, wts, y_rows)
    return _final_call(xc, norm_final.reshape(1, d), s_len)
```

```python
import functools

import numpy as np
import jax
import jax.numpy as jnp
from jax import lax
from jax.experimental import pallas as pl
from jax.experimental.pallas import tpu as pltpu

GRID_W = 64
HEAD_DIM = 64
NA_HEADS = 4
WIN_H = 8
WIN_W = 16
GQA_Q_HEADS = 8
GQA_KV_HEADS = 2
GQA_GROUP = GQA_Q_HEADS // GQA_KV_HEADS
MLA_HEADS = 4
MLA_Q_RANK = 256
MLA_KV_RANK = 128
MLA_NOPE_DIM = 64
MLA_ROPE_DIM = 32
MLA_V_DIM = 64
ROPE_THETA = 10000.0
TOP_K = 2
NORM_EPS = 1e-6
NA_SCALE = HEAD_DIM ** -0.5
GQA_SCALE = HEAD_DIM ** -0.5
MLA_SCALE = (MLA_NOPE_DIM + MLA_ROPE_DIM) ** -0.5
LOG2_E = 1.4426950408889634

LANE = 128
TOKEN_TILE = 256
NA_Q_ROWS = 4
NA_K_ROWS = NA_Q_ROWS + WIN_H
KV_CHUNK = 512
MOE_ROW_BLOCK = 512
VMEM_LIMIT = 56 * 1024 * 1024
NEG_BIG = -1e30
SAFE_LOG2_RANGE = 80.0
NORM_SLACK = 1.02
_N_QG, _N_KG = 0, GQA_Q_HEADS
_N_QM, _N_KM = _N_KG + GQA_KV_HEADS, _N_KG + GQA_KV_HEADS + MLA_HEADS

F32 = jnp.float32
BF16 = jnp.bfloat16


def _params(sem, vmem=VMEM_LIMIT):
    return pltpu.CompilerParams(dimension_semantics=sem, vmem_limit_bytes=vmem)


def _resident(shape):
    zeros = (0,) * len(shape)
    return pl.BlockSpec(shape, lambda *_: zeros, pipeline_mode=pl.Buffered(1))


def _rms(x, n):
    ss = jnp.sum(x * x, axis=-1, keepdims=True)
    return x * lax.rsqrt(ss * (1.0 / n) + NORM_EPS)


def _dot(a, b):
    return jnp.dot(a, b, preferred_element_type=F32)


def _dot_nt(a, b):
    return lax.dot_general(a, b, (((1,), (1,)), ((), ())), preferred_element_type=F32)


def _silu(x):
    return x * jax.nn.sigmoid(x)


def _ada_kernel(c_ref, w_ref, b_ref, o_ref):
    s = _silu(c_ref[...])
    o_ref[0] = jnp.dot(s, w_ref[0], precision=lax.Precision.HIGHEST,
                       preferred_element_type=F32) + b_ref[0]


def _ada_call(c_rows, w_ada, b_ada):
    depth, d, n = w_ada.shape
    rows = c_rows.shape[0]
    tn = 1536 if n % 1536 == 0 else n
    return pl.pallas_call(
        _ada_kernel,
        out_shape=jax.ShapeDtypeStruct((depth, rows, n), F32),
        grid=(depth, n // tn),
        in_specs=[
            pl.BlockSpec((rows, d), lambda i, j: (0, 0)),
            pl.BlockSpec((1, d, tn), lambda i, j: (i, 0, j)),
            pl.BlockSpec((1, 1, tn), lambda i, j: (i, 0, j)),
        ],
        out_specs=pl.BlockSpec((1, rows, tn), lambda i, j: (i, 0, j)),
        compiler_params=_params(("arbitrary", "arbitrary")),
        name="ada_mod",
    )(c_rows, w_ada, b_ada.reshape(depth, 1, n))


def _rope(xh, cos, sa, sb, shift):
    return xh * cos + pltpu.roll(xh, LANE - shift, 1) * sa + pltpu.roll(xh, shift, 1) * sb


_W_NA = 3 * NA_HEADS * HEAD_DIM
_W_QG = GQA_Q_HEADS * HEAD_DIM
_W_KVG = 2 * GQA_KV_HEADS * LANE
_W_LORA = MLA_Q_RANK + MLA_KV_RANK + LANE
_C_NA = 0
_C_QG = _C_NA + _W_NA
_C_KVG = _C_QG + _W_QG
_C_LORA = _C_KVG + _W_KVG
_C_GATE = _C_LORA + _W_LORA


def _mixin_kernel(x_ref, mod_ref, gmix_ref, w_ref, qn_ref, kn_ref, qln_ref, kvln_ref,
                  wuq_ref, wukv_ref, rg_ref, rm_ref,
                  qna_ref, kna_ref, vna_ref, qg_ref, kg_ref, vg_ref, qm_ref, km_ref, vm_ref,
                  gate_ref, nrm_ref):
    x = x_ref[0]
    d = x.shape[-1]
    tm = x.shape[0]
    mod = mod_ref[0]
    h = _rms(x, d) * gmix_ref[...]
    h = h * (1.0 + mod[1:2]) + mod[0:1]
    hb = h.astype(BF16)

    def proj(lo, width):
        return _dot(hb, w_ref[:, lo:lo + width])

    def block(a, j):
        return a[:, j * LANE:(j + 1) * LANE]

    nw = NA_HEADS * HEAD_DIM
    p_na = proj(_C_NA, _W_NA)
    qna_ref[0] = (p_na[:, 0:nw] * (NA_SCALE * LOG2_E)).astype(BF16)
    kna_ref[0] = p_na[:, nw:2 * nw].astype(BF16)
    vna_ref[0] = p_na[:, 2 * nw:3 * nw].astype(BF16)

    lane = lax.broadcasted_iota(jnp.int32, (tm, LANE), 1)
    low = lane < HEAD_DIM
    ones_pad = jnp.where(low, 0.0, 1.0)
    norms = []

    def unit_at(pos):
        return jnp.where(lane == pos, 1.0, 0.0)

    cos_g, sa_g, sb_g = rg_ref[0], rg_ref[1], rg_ref[2]
    cos_m, sa_m, sb_m = rm_ref[0], rm_ref[1], rm_ref[2]
    g_half = HEAD_DIM // 2
    m_half = MLA_ROPE_DIM // 2

    p_qg = proj(_C_QG, _W_QG)
    for j in range(GQA_Q_HEADS // 2):
        xp = block(p_qg, j)
        sq = xp * xp
        ss_lo = jnp.sum(jnp.where(low, sq, 0.0), axis=-1, keepdims=True)
        ss_hi = jnp.sum(jnp.where(low, 0.0, sq), axis=-1, keepdims=True)
        inv = jnp.where(low, lax.rsqrt(ss_lo * (1.0 / HEAD_DIM) + NORM_EPS),
                        lax.rsqrt(ss_hi * (1.0 / HEAD_DIM) + NORM_EPS))
        xp = _rope(xp * inv * qn_ref[...], cos_g, sa_g, sb_g, g_half) * (GQA_SCALE * LOG2_E)
        qg_ref[0, :, j * LANE:(j + 1) * LANE] = xp.astype(BF16)
        sq = xp * xp
        norms.append((_N_QG + 2 * j, jnp.sum(jnp.where(low, sq, 0.0), axis=-1, keepdims=True)))
        norms.append((_N_QG + 2 * j + 1, jnp.sum(jnp.where(low, 0.0, sq), axis=-1, keepdims=True)))
    p_kvg = proj(_C_KVG, _W_KVG)
    for hh in range(GQA_KV_HEADS):
        xh = _rope(_rms(block(p_kvg, hh), HEAD_DIM) * kn_ref[...], cos_g, sa_g, sb_g, g_half)
        norms.append((_N_KG + hh, jnp.sum(xh * xh, axis=-1, keepdims=True)))
        kg_ref[0, :, hh * LANE:(hh + 1) * LANE] = (xh + unit_at(HEAD_DIM)).astype(BF16)
        vh = block(p_kvg, GQA_KV_HEADS + hh) + ones_pad
        vg_ref[0, :, hh * LANE:(hh + 1) * LANE] = vh.astype(BF16)

    p_lora = proj(_C_LORA, _W_LORA)
    cq = (_rms(p_lora[:, 0:MLA_Q_RANK], MLA_Q_RANK) * qln_ref[...]).astype(BF16)
    q_up = _dot(cq, wuq_ref[...])
    ckv = p_lora[:, MLA_Q_RANK:MLA_Q_RANK + MLA_KV_RANK]
    ckv = (_rms(ckv, MLA_KV_RANK) * kvln_ref[...]).astype(BF16)
    kv_up = _dot(ckv, wukv_ref[...])
    k_rope = _rope(p_lora[:, MLA_Q_RANK + MLA_KV_RANK:], cos_m, sa_m, sb_m, m_half)
    for hh in range(MLA_HEADS):
        sl = slice(hh * LANE, (hh + 1) * LANE)
        qh = _rope(block(q_up, hh), cos_m, sa_m, sb_m, m_half) * (MLA_SCALE * LOG2_E)
        qm_ref[0, :, sl] = qh.astype(BF16)
        kh = block(kv_up, hh) + k_rope
        norms.append((_N_QM + hh, jnp.sum(qh * qh, axis=-1, keepdims=True)))
        norms.append((_N_KM + hh, jnp.sum(kh * kh, axis=-1, keepdims=True)))
        km_ref[0, :, sl] = (kh + unit_at(MLA_NOPE_DIM + MLA_ROPE_DIM)).astype(BF16)
        vm_ref[0, :, sl] = (block(kv_up, MLA_HEADS + hh) + ones_pad).astype(BF16)

    packed = jnp.zeros((tm, LANE), F32)
    for pos, val in norms:
        packed = jnp.where(lane == pos, val, packed)
    nrm_ref[0] = packed

    for j in range(3):
        gate_ref[0, :, j * d:(j + 1) * d] = jax.nn.sigmoid(proj(_C_GATE + j * d, d)).astype(BF16)


def _mixin_call(xc, mod, gmix, w_cat, qn, kn, qln, kvln, wuq, wukv, rope_g, rope_m):
    b, t, d = xc.shape
    tm = TOKEN_TILE
    nt = t // tm
    tok = lambda w: pl.BlockSpec((1, tm, w), lambda ti, bi: (bi, ti, 0))
    widths = [NA_HEADS * HEAD_DIM] * 3 + [GQA_Q_HEADS * HEAD_DIM, GQA_KV_HEADS * LANE,
                                          GQA_KV_HEADS * LANE, MLA_HEADS * LANE,
                                          MLA_HEADS * LANE, MLA_HEADS * LANE, 3 * d]
    return pl.pallas_call(
        _mixin_kernel,
        out_shape=[jax.ShapeDtypeStruct((b, t, w), BF16) for w in widths]
        + [jax.ShapeDtypeStruct((b, t, LANE), F32)],
        grid=(nt, b),
        in_specs=[
            tok(d),
            pl.BlockSpec((1, 6, d), lambda ti, bi: (2 * bi + (ti == nt - 1).astype(jnp.int32), 0, 0)),
            _resident((1, d)),
            _resident(w_cat.shape),
            _resident((1, LANE)), _resident((1, LANE)),
            _resident((1, MLA_Q_RANK)), _resident((1, MLA_KV_RANK)),
            _resident(wuq.shape), _resident(wukv.shape),
            pl.BlockSpec((3, tm, LANE), lambda ti, bi: (0, ti, 0)),
            pl.BlockSpec((3, tm, LANE), lambda ti, bi: (0, ti, 0)),
        ],
        out_specs=[tok(w) for w in widths] + [tok(LANE)],
        compiler_params=_params(("arbitrary", "arbitrary")),
        name="mixer_in",
    )(xc, mod, gmix, w_cat, qn, kn, qln, kvln, wuq, wukv, rope_g, rope_m)


def _na_kernel(q_ref, k_ref, v_ref, bias_ref, o_ref, *, n_groups, n_rows, s_len, c_len):
    g = pl.program_id(1)
    qn = NA_Q_ROWS * GRID_W
    kn = NA_K_ROWS * GRID_W
    lane = lax.broadcasted_iota(jnp.int32, (qn, LANE), 1)
    low = lane < HEAD_DIM

    def attend_pairs(k_nb, v_nb):
        k_c = k_ref[0, s_len:s_len + c_len, :]
        v_c = v_ref[0, s_len:s_len + c_len, :]
        for pair in range(NA_HEADS // 2):
            sl = slice(pair * LANE, (pair + 1) * LANE)
            qp = q_ref[0, :, sl]
            outs = []
            for sub in range(2):
                head = 2 * pair + sub
                qm = jnp.where(low if sub == 0 else jnp.logical_not(low), qp, jnp.zeros_like(qp))
                s_c = _dot_nt(qm, k_c[:, sl])
                m = jnp.max(s_c, axis=-1, keepdims=True)
                if k_nb is not None:
                    s_nb = _dot_nt(qm, k_nb[:, sl]) + bias_ref[0, head]
                    m = jnp.maximum(m, jnp.max(s_nb, axis=-1, keepdims=True))
                p_c = jnp.exp2(s_c - m)
                den = jnp.sum(p_c, axis=-1, keepdims=True)
                o = _dot(p_c.astype(BF16), v_c[:, sl])
                if k_nb is not None:
                    p_nb = jnp.exp2(s_nb - m)
                    den = den + jnp.sum(p_nb, axis=-1, keepdims=True)
                    o = o + _dot(p_nb.astype(BF16), v_nb[:, sl])
                outs.append(o / den)
            o_ref[0, :, sl] = jnp.where(low, outs[0], outs[1]).astype(BF16)

    @pl.when(g < n_groups)
    def _():
        first_row = jnp.clip(NA_Q_ROWS * g - WIN_H // 2, 0, n_rows - NA_K_ROWS)
        start = pl.multiple_of(first_row * GRID_W, GRID_W)
        attend_pairs(k_ref[0, pl.ds(start, kn), :], v_ref[0, pl.ds(start, kn), :])

    @pl.when(g == n_groups)
    def _():
        attend_pairs(None, None)


def _na_call(q, k, v, bias, s_len, c_len):
    b, t, w = q.shape
    qn = NA_Q_ROWS * GRID_W
    kn = NA_K_ROWS * GRID_W
    n_rows = s_len // GRID_W
    n_groups = s_len // qn
    assert c_len == qn and n_rows >= NA_K_ROWS

    def variant(g):
        return jnp.where(g == 0, 0, jnp.where(g >= n_groups - 1, 2, 1))

    return pl.pallas_call(
        functools.partial(_na_kernel, n_groups=n_groups, n_rows=n_rows, s_len=s_len, c_len=c_len),
        out_shape=jax.ShapeDtypeStruct((b, t, w), BF16),
        grid=(b, n_groups + 1),
        in_specs=[
            pl.BlockSpec((1, qn, w), lambda bi, g: (bi, g, 0)),
            pl.BlockSpec((1, t, w), lambda bi, g: (bi, 0, 0)),
            pl.BlockSpec((1, t, w), lambda bi, g: (bi, 0, 0)),
            pl.BlockSpec((1, NA_HEADS, qn, kn), lambda bi, g: (variant(g), 0, 0, 0)),
        ],
        out_specs=pl.BlockSpec((1, qn, w), lambda bi, g: (bi, g, 0)),
        compiler_params=_params(("arbitrary", "arbitrary")),
        name="na_attn",
    )(q, k, v, bias)


def _na_bias_tables(rpb, n_rows):
    qi = np.arange(NA_Q_ROWS)[:, None]
    kj = np.arange(NA_K_ROWS)[None, :]
    qc = np.arange(GRID_W)[:, None]
    kc = np.arange(GRID_W)[None, :]
    col0 = np.clip(qc - WIN_W // 2, 0, GRID_W - WIN_W)
    col_ok = (kc >= col0) & (kc < col0 + WIN_W)
    dc = np.clip(kc - qc, -(WIN_W - 1), WIN_W - 1) + WIN_W - 1
    col_sel = np.eye(2 * WIN_W - 1, dtype=np.float32)[dc]
    row_sel, valid = [], []
    for r0, u0 in ((0, 0), (WIN_H // 2, 0), (n_rows - NA_Q_ROWS, n_rows - NA_K_ROWS)):
        qr = r0 + qi
        kr = u0 + kj
        row0 = np.clip(qr - WIN_H // 2, 0, n_rows - WIN_H)
        row_ok = (kr >= row0) & (kr < row0 + WIN_H)
        dr = np.clip(kr - qr, -(WIN_H - 1), WIN_H - 1) + WIN_H - 1
        row_sel.append(np.eye(2 * WIN_H - 1, dtype=np.float32)[dr])
        valid.append(row_ok[:, None, :, None] & col_ok[None, :, None, :])
    row_sel = jnp.asarray(np.stack(row_sel))
    valid = np.stack(valid).reshape(3, 1, NA_Q_ROWS * GRID_W, NA_K_ROWS * GRID_W)
    picked = jnp.einsum('vija,lhab,qkb->lvhiqjk', row_sel, rpb.astype(F32) * LOG2_E, jnp.asarray(col_sel),
                        precision=lax.Precision.HIGHEST)
    picked = picked.reshape(rpb.shape[0], 3, NA_HEADS, NA_Q_ROWS * GRID_W, NA_K_ROWS * GRID_W)
    return jnp.where(valid[None], picked, NEG_BIG)


def _row_max(s):
    parts = [s[:, j * LANE:(j + 1) * LANE] for j in range(s.shape[1] // LANE)]
    while len(parts) > 1:
        parts = [jnp.maximum(parts[j], parts[j + 1]) for j in range(0, len(parts) - 1, 2)] + parts[len(parts) & ~1:]
    return jnp.max(parts[0], axis=-1, keepdims=True)


def _flash_kernel(q_ref, k_ref, v_ref, u_ref, o_ref, *, n_kv, group, tq, q_width, s_len, c_len,
                  with_latent, online):
    low = lax.broadcasted_iota(jnp.int32, (tq, LANE), 1) < HEAD_DIM
    heads = range(n_kv)
    lanes = [slice(h * LANE, (h + 1) * LANE) for h in heads]

    def load_q(head):
        if q_width == LANE:
            qh = q_ref[0, :, head * LANE:(head + 1) * LANE]
        else:
            pair = q_ref[0, :, (head // 2) * LANE:(head // 2 + 1) * LANE]
            if head % 2:
                pair = pltpu.roll(pair.astype(F32), HEAD_DIM, 1).astype(BF16)
            qh = jnp.where(low, pair, jnp.zeros_like(pair))
        return qh if online else qh + u_ref[0, head:head + 1, :]

    qs = []
    for h in heads:
        parts = [load_q(h * group + g) for g in range(group)]
        qs.append(parts[0] if group == 1 else jnp.concatenate(parts, axis=0))

    if online:
        carry = []
        for h in heads:
            s = _dot_nt(qs[h], k_ref[0, s_len:s_len + c_len, lanes[h]])
            m = _row_max(s)
            carry += [m, _dot(jnp.exp2(s - m).astype(BF16), v_ref[0, s_len:s_len + c_len, lanes[h]])]
        if with_latent:
            def body(i, carry):
                st = pl.multiple_of(i * KV_CHUNK, KV_CHUNK)
                new = []
                for h in heads:
                    m_old, acc_old = carry[2 * h], carry[2 * h + 1]
                    sc = _dot_nt(qs[h], k_ref[0, pl.ds(st, KV_CHUNK), lanes[h]])
                    m_new = jnp.maximum(m_old, _row_max(sc))
                    p = jnp.exp2(sc - m_new).astype(BF16)
                    pv = _dot(p, v_ref[0, pl.ds(st, KV_CHUNK), lanes[h]])
                    new += [m_new, jnp.exp2(m_old - m_new) * acc_old + pv]
                return tuple(new)
            carry = lax.fori_loop(0, s_len // KV_CHUNK, body, tuple(carry), unroll=2)
        accs = [carry[2 * h + 1] for h in heads]
    else:
        chunks = [(s_len, c_len)]
        if with_latent:
            chunks += [(i * KV_CHUNK, KV_CHUNK) for i in range(s_len // KV_CHUNK)]
        accs = [None] * n_kv
        for st, size in chunks:
            for h in heads:
                p = jnp.exp2(_dot_nt(qs[h], k_ref[0, st:st + size, lanes[h]])).astype(BF16)
                pv = _dot(p, v_ref[0, st:st + size, lanes[h]])
                accs[h] = pv if accs[h] is None else accs[h] + pv
    outs = []
    for h in heads:
        acc = accs[h]
        o = acc / pltpu.roll(acc, HEAD_DIM, 1)
        for g in range(group):
            outs.append(o[g * tq:(g + 1) * tq])
    for pair in range(len(outs) // 2):
        packed = jnp.where(low, outs[2 * pair], pltpu.roll(outs[2 * pair + 1], HEAD_DIM, 1))
        o_ref[0, :, pair * LANE:(pair + 1) * LANE] = packed.astype(BF16)


def _flash_call(q, k, v, shift, *, n_kv, group, tq, q_width, s_len, c_len, online, name):
    b, t, _ = q.shape
    n_heads = n_kv * group
    tq_ctx = min(tq, c_len)
    assert s_len % tq == 0 and c_len % tq_ctx == 0 and s_len % tq_ctx == 0 and s_len % KV_CHUNK == 0
    kv_spec = pl.BlockSpec((1, t, n_kv * LANE), lambda bi, ti: (bi, 0, 0))
    u_spec = pl.BlockSpec((1, n_heads, LANE), lambda bi, ti: (bi, 0, 0))
    common = dict(n_kv=n_kv, group=group, q_width=q_width, s_len=s_len, c_len=c_len, online=online)
    width = n_heads * HEAD_DIM
    tag = "_online" if online else ""
    y_lat = pl.pallas_call(
        functools.partial(_flash_kernel, tq=tq, with_latent=True, **common),
        out_shape=jax.ShapeDtypeStruct((b, s_len, width), BF16),
        grid=(b, s_len // tq),
        in_specs=[pl.BlockSpec((1, tq, n_heads * q_width), lambda bi, ti: (bi, ti, 0)), kv_spec, kv_spec, u_spec],
        out_specs=pl.BlockSpec((1, tq, width), lambda bi, ti: (bi, ti, 0)),
        compiler_params=_params(("arbitrary", "arbitrary")),
        name=name + tag,
    )(q, k, v, shift)
    first = s_len // tq_ctx
    y_ctx = pl.pallas_call(
        functools.partial(_flash_kernel, tq=tq_ctx, with_latent=False, **common),
        out_shape=jax.ShapeDtypeStruct((b, c_len, width), BF16),
        grid=(b, c_len // tq_ctx),
        in_specs=[pl.BlockSpec((1, tq_ctx, n_heads * q_width), lambda bi, ti: (bi, first + ti, 0)), kv_spec, kv_spec,
                  u_spec],
        out_specs=pl.BlockSpec((1, tq_ctx, width), lambda bi, ti: (bi, ti, 0)),
        compiler_params=_params(("arbitrary", "arbitrary")),
        name=name + "_ctx" + tag,
    )(q, k, v, shift)
    return y_lat, y_ctx


def _bounded_attention(q, k, v, q_norm2, k_norm2, unit_lane, **kw):
    n_kv, group = kw["n_kv"], kw["group"]
    k_norm2 = jnp.repeat(k_norm2, group, axis=1)
    bound = jnp.sqrt(q_norm2 * k_norm2) * NORM_SLACK
    lane = jnp.arange(LANE) == unit_lane
    shift = jnp.where(lane[None, None, :], -bound[:, :, None], 0.0).astype(BF16)
    safe = jnp.max(bound) * 2.0 < SAFE_LOG2_RANGE
    return lax.cond(safe,
                    lambda ops: _flash_call(*ops, online=False, **kw),
                    lambda ops: _flash_call(*ops, online=True, **kw),
                    (q, k, v, shift))


def _mod_spec(d, nt):
    return pl.BlockSpec((1, 6, d), lambda bi, ti: (2 * bi + (ti == nt - 1).astype(jnp.int32), 0, 0))


def _merged_residual(x_ref, mod_ref, yna_ref, yg_ref, ygc_ref, ym_ref, ymc_ref, gate_ref, wna_ref, wg_ref,
                     wm_ref, wout_ref, n_lat_tiles):
    d = x_ref.shape[-1]
    is_ctx = pl.program_id(1) >= n_lat_tiles
    y_g = jnp.where(is_ctx, ygc_ref[0], yg_ref[0])
    y_m = jnp.where(is_ctx, ymc_ref[0], ym_ref[0])
    m = gate_ref[0, :, 0:d].astype(F32) * _dot(yna_ref[0], wna_ref[...])
    m = m + gate_ref[0, :, d:2 * d].astype(F32) * _dot(y_g, wg_ref[...])
    m = m + gate_ref[0, :, 2 * d:3 * d].astype(F32) * _dot(y_m, wm_ref[...])
    r = _dot(m.astype(BF16), wout_ref[...])
    return x_ref[0] + mod_ref[0][2:3] * r


def _norm2(x, mod, gain):
    h = _rms(x, x.shape[-1]) * gain
    return h * (1.0 + mod[4:5]) + mod[3:4]


def _merge_ffn_kernel(*refs, n_lat_tiles):
    mix_refs, (gn_ref, wg_ref, wu_ref, wd_ref, o_ref) = refs[:12], refs[12:]
    x = _merged_residual(*mix_refs, n_lat_tiles)
    mod = mix_refs[1][0]
    hb = _norm2(x, mod, gn_ref[...]).astype(BF16)
    act = (_silu(_dot(hb, wg_ref[...])) * _dot(hb, wu_ref[...])).astype(BF16)
    o_ref[0] = x + mod[5:6] * _dot(act, wd_ref[...])


def _merge_router_kernel(*refs, n_lat_tiles, n_experts):
    mix_refs, (gn_ref, wr_ref, o_ref, idx_ref, wt_ref, cnt_ref, carry_ref) = refs[:12], refs[12:]

    @pl.when((pl.program_id(0) == 0) & (pl.program_id(1) == 0))
    def _():
        carry_ref[...] = jnp.zeros_like(carry_ref)

    x = _merged_residual(*mix_refs, n_lat_tiles)
    o_ref[0] = x
    tm = x.shape[0]
    h = _norm2(x, mix_refs[1][0], gn_ref[...])
    logits = jnp.dot(h, wr_ref[...], precision=lax.Precision.HIGHEST, preferred_element_type=F32)
    lane = lax.broadcasted_iota(jnp.int32, (tm, LANE), 1).astype(F32)
    logits = jnp.where(lane < n_experts, logits, -jnp.inf)
    m1 = jnp.max(logits, axis=-1, keepdims=True)
    i1 = jnp.min(jnp.where(logits == m1, lane, float(LANE)), axis=-1, keepdims=True)
    rest = jnp.where(lane == i1, -jnp.inf, logits)
    m2 = jnp.max(rest, axis=-1, keepdims=True)
    i2 = jnp.min(jnp.where(rest == m2, lane, float(LANE)), axis=-1, keepdims=True)
    e2 = jnp.exp(m2 - m1)
    w1 = 1.0 / (1.0 + e2)
    w2 = e2 / (1.0 + e2)
    hot1 = lane == i1
    hot2 = lane == i2
    hot = jnp.where(hot1 | hot2, 1.0, 0.0)
    rows = lax.broadcasted_iota(jnp.int32, (tm, tm), 0)
    cols = lax.broadcasted_iota(jnp.int32, (tm, tm), 1)
    below = jnp.where(rows > cols, 1.0, 0.0).astype(BF16)
    before = _dot(below, hot.astype(BF16)) + carry_ref[0:1, :]
    r1 = jnp.sum(jnp.where(hot1, before, 0.0), axis=-1, keepdims=True)
    r2 = jnp.sum(jnp.where(hot2, before, 0.0), axis=-1, keepdims=True)
    total = carry_ref[0:1, :] + jnp.sum(hot, axis=0, keepdims=True)
    carry_ref[...] = jnp.broadcast_to(total, carry_ref.shape)
    cnt_ref[...] = jnp.broadcast_to(total, cnt_ref.shape)
    packed = jnp.where(lane == 0, i1, jnp.where(lane == 1, i2, jnp.where(lane == 2, r1, jnp.where(lane == 3, r2, 0.0))))
    idx_ref[0] = packed.astype(jnp.int32)
    wt_ref[0] = jnp.where(lane == 0, w1, jnp.where(lane == 1, w2, 0.0))


def _merge_call(xc, mod, y_na, y_g, y_m, gate, w_na, w_g, w_m, w_out, gn, *, ffn=None, router=None):
    b, t, d = xc.shape
    tm = TOKEN_TILE
    nt = t // tm
    n_lat = y_g[0].shape[1] // tm
    tok = lambda w: pl.BlockSpec((1, tm, w), lambda bi, ti: (bi, ti, 0))
    lat = lambda w: pl.BlockSpec((1, tm, w), lambda bi, ti: (bi, jnp.minimum(ti, n_lat - 1), 0))
    ctx = lambda w: pl.BlockSpec((1, tm, w), lambda bi, ti: (bi, jnp.maximum(ti - n_lat, 0), 0))
    wg_, wm_ = y_g[0].shape[-1], y_m[0].shape[-1]
    mix_specs = [tok(d), _mod_spec(d, nt), tok(y_na.shape[-1]), lat(wg_), ctx(wg_), lat(wm_), ctx(wm_),
                 tok(3 * d), _resident(w_na.shape), _resident(w_g.shape), _resident(w_m.shape),
                 _resident(w_out.shape), _resident((1, d))]
    mix_args = (xc, mod, y_na, y_g[0], y_g[1], y_m[0], y_m[1], gate, w_na, w_g, w_m, w_out, gn)
    x_shape = jax.ShapeDtypeStruct(xc.shape, F32)
    if ffn is not None:
        return pl.pallas_call(
            functools.partial(_merge_ffn_kernel, n_lat_tiles=n_lat),
            out_shape=x_shape,
            grid=(b, nt),
            in_specs=mix_specs + [_resident(w.shape) for w in ffn],
            out_specs=tok(d),
            input_output_aliases={0: 0},
            compiler_params=_params(("arbitrary", "arbitrary")),
            name="merge_ffn",
        )(*mix_args, *ffn)
    w_router_pad, n_experts = router
    return pl.pallas_call(
        functools.partial(_merge_router_kernel, n_lat_tiles=n_lat, n_experts=n_experts),
        out_shape=[x_shape, jax.ShapeDtypeStruct((b, t, LANE), jnp.int32),
                   jax.ShapeDtypeStruct((b, t, LANE), F32), jax.ShapeDtypeStruct((8, LANE), F32)],
        grid=(b, nt),
        in_specs=mix_specs + [_resident(w_router_pad.shape)],
        out_specs=[tok(d), tok(LANE), tok(LANE), pl.BlockSpec((8, LANE), lambda bi, ti: (0, 0))],
        scratch_shapes=[pltpu.VMEM((8, LANE), F32)],
        input_output_aliases={0: 0},
        compiler_params=_params(("arbitrary", "arbitrary")),
        name="merge_router",
    )(*mix_args, w_router_pad)


def _row_copy(src, src_row, dst, dst_row, sem):
    return pltpu.make_async_copy(src.at[pl.ds(src_row, 1)], dst.at[pl.ds(dst_row, 1)], sem)


def _dispatch_kernel(pad_from_ref, pad_cnt_ref, dest_ref, x_ref, mod_ref, gn_ref, xs_ref, h_buf, zrow, sem,
                     zsem, *, n_tiles, n_exp):
    i = pl.program_id(0)
    slot = i % 2
    tm = x_ref.shape[1]

    @pl.when(i == 0)
    def _():
        zrow[...] = jnp.zeros_like(zrow)
        for e in range(n_exp):
            def zero_issue(r, carry, e=e):
                _row_copy(zrow, 0, xs_ref, pad_from_ref[e] + r, zsem).start()
                return carry
            lax.fori_loop(0, pad_cnt_ref[e], zero_issue, 0)
        for e in range(n_exp):
            def zero_drain(r, carry):
                _row_copy(zrow, 0, xs_ref, 0, zsem).wait()
                return carry
            lax.fori_loop(0, pad_cnt_ref[e], zero_drain, 0)

    @pl.when(i < n_tiles)
    def _():
        h_buf[slot] = _norm2(x_ref[0], mod_ref[0], gn_ref[...])

        def issue(r, carry):
            for k in range(TOP_K):
                _row_copy(h_buf.at[slot], r, xs_ref, dest_ref[0, 0, TOP_K * r + k], sem.at[slot]).start()
            return carry

        lax.fori_loop(0, tm, issue, 0, unroll=8)

    @pl.when(i > 0)
    def _():
        def drain(r, carry):
            for k in range(TOP_K):
                _row_copy(h_buf.at[1 - slot], 0, xs_ref, 0, sem.at[1 - slot]).wait()
            return carry

        lax.fori_loop(0, tm, drain, 0, unroll=8)


def _dispatch_call(pad_from, pad_cnt, dest_blocks, xc, mod, gn, n_rows):
    b, t, d = xc.shape
    tm = TOKEN_TILE
    nt = t // tm
    n_tiles = b * nt

    def tile(i):
        j = jnp.minimum(i, n_tiles - 1)
        return j // nt, j % nt

    grid_spec = pltpu.PrefetchScalarGridSpec(
        num_scalar_prefetch=2,
        grid=(n_tiles + 1,),
        in_specs=[
            pl.BlockSpec((1, 1, dest_blocks.shape[-1]), lambda i, pf, pc: (jnp.minimum(i, n_tiles - 1), 0, 0),
                         memory_space=pltpu.SMEM),
            pl.BlockSpec((1, tm, d), lambda i, pf, pc: (*tile(i), 0)),
            pl.BlockSpec((1, 6, d), lambda i, pf, pc: (
                2 * tile(i)[0] + (tile(i)[1] == nt - 1).astype(jnp.int32), 0, 0)),
            pl.BlockSpec((1, d), lambda i, pf, pc: (0, 0)),
        ],
        out_specs=pl.BlockSpec(memory_space=pl.ANY),
        scratch_shapes=[pltpu.VMEM((2, tm, d), F32), pltpu.VMEM((8, d), F32),
                        pltpu.SemaphoreType.DMA((2,)), pltpu.SemaphoreType.DMA],
    )
    return pl.pallas_call(
        functools.partial(_dispatch_kernel, n_tiles=n_tiles, n_exp=pad_from.shape[0]),
        out_shape=jax.ShapeDtypeStruct((n_rows, d), F32),
        grid_spec=grid_spec,
        compiler_params=_params(("arbitrary",)),
        name="moe_dispatch",
    )(pad_from, pad_cnt, dest_blocks, xc, mod, gn)


def _moe_kernel(be_ref, nu_ref, xs_ref, wgu_ref, wd_ref, y_ref, *, f_chunk):
    f_dim = wd_ref.shape[1]

    @pl.when(pl.program_id(0) < nu_ref[0])
    def _():
        xb = xs_ref[...].astype(BF16)
        acc = None
        for c in range(0, f_dim, f_chunk):
            gate = _dot(xb, wgu_ref[0, :, c:c + f_chunk])
            up = _dot(xb, wgu_ref[0, :, f_dim + c:f_dim + c + f_chunk])
            part = _dot((_silu(gate) * up).astype(BF16), wd_ref[0, c:c + f_chunk, :])
            acc = part if acc is None else acc + part
        y_ref[...] = acc

    @pl.when(pl.program_id(0) >= nu_ref[0])
    def _():
        y_ref[...] = jnp.zeros_like(y_ref)


def _moe_call(block_expert, n_used, xs, w_gu, w_dn):
    rows, d = xs.shape
    n_exp, _, two_f = w_gu.shape
    f_dim = two_f // 2
    f_chunk = 2 * LANE if f_dim % (2 * LANE) == 0 else f_dim
    tmr = MOE_ROW_BLOCK
    grid_spec = pltpu.PrefetchScalarGridSpec(
        num_scalar_prefetch=2,
        grid=(rows // tmr,),
        in_specs=[
            pl.BlockSpec((tmr, d), lambda i, be, nu: (jnp.minimum(i, nu[0] - 1), 0)),
            pl.BlockSpec((1, d, two_f), lambda i, be, nu: (be[i], 0, 0)),
            pl.BlockSpec((1, f_dim, d), lambda i, be, nu: (be[i], 0, 0)),
        ],
        out_specs=pl.BlockSpec((tmr, d), lambda i, be, nu: (i, 0)),
    )
    return pl.pallas_call(
        functools.partial(_moe_kernel, f_chunk=f_chunk),
        out_shape=jax.ShapeDtypeStruct((rows, d), F32),
        grid_spec=grid_spec,
        compiler_params=_params(("arbitrary",)),
        name="moe_experts",
    )(block_expert, n_used, xs, w_gu, w_dn)


def _combine_kernel(dest_ref, x_ref, mod_ref, wt_ref, gain_ref, y_ref, o_ref, buf1, buf2, sem, *, n_tiles,
                    final):
    i = pl.program_id(0)
    slot = i % 2
    tm = x_ref.shape[1]

    @pl.when(i < n_tiles)
    def _():
        def issue(r, carry):
            _row_copy(y_ref, dest_ref[0, 0, TOP_K * r], buf1.at[slot], r, sem.at[slot]).start()
            _row_copy(y_ref, dest_ref[0, 0, TOP_K * r + 1], buf2.at[slot], r, sem.at[slot]).start()
            return carry

        lax.fori_loop(0, tm, issue, 0, unroll=8)

    @pl.when(i > 0)
    def _():
        def drain(r, carry):
            _row_copy(y_ref, 0, buf1.at[1 - slot], 0, sem.at[1 - slot]).wait()
            _row_copy(y_ref, 0, buf2.at[1 - slot], 0, sem.at[1 - slot]).wait()
            return carry

        lax.fori_loop(0, tm, drain, 0, unroll=8)
        wt = wt_ref[0]
        mix = wt[:, 0:1] * buf1[1 - slot] + wt[:, 1:2] * buf2[1 - slot]
        x = x_ref[0] + mod_ref[0][5:6] * mix
        o_ref[0] = _rms(x, x.shape[-1]) * gain_ref[...] if final else x


def _combine_call(dest_blocks, xc, mod, wts, y_rows, final_gain=None, s_len=None):
    b, t, d = xc.shape
    tm = TOKEN_TILE
    nt = t // tm
    final = final_gain is not None
    used = s_len // tm if final else nt
    n_tiles = b * used

    def issued(i):
        j = jnp.minimum(i, n_tiles - 1)
        return j // used, j % used

    def finished(i):
        j = jnp.maximum(i - 1, 0)
        return j // used, j % used

    tok = lambda w: pl.BlockSpec((1, tm, w), lambda i: (*finished(i), 0))
    gain = final_gain if final else jnp.ones((1, d), F32)
    return pl.pallas_call(
        functools.partial(_combine_kernel, n_tiles=n_tiles, final=final),
        out_shape=jax.ShapeDtypeStruct((b, used * tm, d), F32),
        grid=(n_tiles + 1,),
        in_specs=[
            pl.BlockSpec((1, 1, dest_blocks.shape[-1]), lambda i: (issued(i)[0] * nt + issued(i)[1], 0, 0),
                         memory_space=pltpu.SMEM),
            tok(d),
            pl.BlockSpec((1, 6, d), lambda i: (
                2 * finished(i)[0] + (finished(i)[1] == nt - 1).astype(jnp.int32), 0, 0)),
            tok(LANE), pl.BlockSpec((1, d), lambda i: (0, 0)),
            pl.BlockSpec(memory_space=pl.ANY),
        ],
        out_specs=tok(d),
        scratch_shapes=[pltpu.VMEM((2, tm, d), F32), pltpu.VMEM((2, tm, d), F32),
                        pltpu.SemaphoreType.DMA((2,))],
        input_output_aliases={} if final else {1: 0},
        compiler_params=_params(("arbitrary",)),
        name="moe_combine_final" if final else "moe_combine",
    )(dest_blocks, xc, mod, wts, gain, y_rows)


def _final_kernel(x_ref, g_ref, o_ref):
    x = x_ref[0]
    o_ref[0] = _rms(x, x.shape[-1]) * g_ref[...]


def _final_call(xc, gain, s_len):
    b, t, d = xc.shape
    tm = TOKEN_TILE
    tok = pl.BlockSpec((1, tm, d), lambda bi, ti: (bi, ti, 0))
    return pl.pallas_call(
        _final_kernel,
        out_shape=jax.ShapeDtypeStruct((b, s_len, d), F32),
        grid=(b, s_len // tm),
        in_specs=[tok, _resident((1, d))],
        out_specs=tok,
        compiler_params=_params(("arbitrary", "arbitrary")),
        name="final_norm",
    )(xc, gain)


def _pad_heads(w, n_heads, width):
    k = w.shape[0]
    w = w.reshape(k, n_heads, width)
    return jnp.pad(w, ((0, 0), (0, 0), (0, LANE - width))).reshape(k, n_heads * LANE)


def _layer_weights(w_in, w_uq, w_ukv):
    d = w_in.shape[0]
    nw = NA_HEADS * HEAD_DIM
    gk = GQA_KV_HEADS * HEAD_DIM
    o = 0
    k_na = w_in[:, o:o + nw]; o += nw
    v_na = w_in[:, o:o + nw]; o += nw
    k_g = w_in[:, o:o + gk]; o += gk
    v_g = w_in[:, o:o + gk]; o += gk
    c_kv = w_in[:, o:o + MLA_KV_RANK]; o += MLA_KV_RANK
    k_r = w_in[:, o:o + MLA_ROPE_DIM]; o += MLA_ROPE_DIM
    q_na = w_in[:, o:o + nw]; o += nw
    q_g = w_in[:, o:o + GQA_Q_HEADS * HEAD_DIM]; o += GQA_Q_HEADS * HEAD_DIM
    c_q = w_in[:, o:o + MLA_Q_RANK]; o += MLA_Q_RANK
    gate = w_in[:, o:]
    k_r_pad = jnp.pad(k_r, ((0, 0), (MLA_NOPE_DIM, LANE - MLA_NOPE_DIM - MLA_ROPE_DIM)))
    w_cat = jnp.concatenate([
        q_na, k_na, v_na, q_g, _pad_heads(k_g, GQA_KV_HEADS, HEAD_DIM), _pad_heads(v_g, GQA_KV_HEADS, HEAD_DIM), c_q, c_kv, k_r_pad, gate],
        axis=1).astype(BF16)
    assert w_cat.shape[1] == _C_GATE + 3 * d
    wuq = _pad_heads(w_uq, MLA_HEADS, MLA_NOPE_DIM + MLA_ROPE_DIM).astype(BF16)
    kv = w_ukv.reshape(MLA_KV_RANK, MLA_HEADS, MLA_NOPE_DIM + MLA_V_DIM)
    wuk = _pad_heads(kv[:, :, :MLA_NOPE_DIM].reshape(MLA_KV_RANK, -1), MLA_HEADS, MLA_NOPE_DIM)
    wuv = _pad_heads(kv[:, :, MLA_NOPE_DIM:].reshape(MLA_KV_RANK, -1), MLA_HEADS, MLA_V_DIM)
    return w_cat, wuq, jnp.concatenate([wuk, wuv], axis=1).astype(BF16)


def _rope_tables(s_len, c_len, rot_dim, lane_off):
    half = rot_dim // 2
    n_freq = half // 2
    pos = jnp.arange(s_len)
    rows = (pos // GRID_W).astype(F32)
    cols = (pos % GRID_W).astype(F32)
    inv = jnp.power(ROPE_THETA, -jnp.arange(n_freq, dtype=F32) / n_freq)
    ang = jnp.concatenate([rows[:, None] * inv, cols[:, None] * inv], axis=-1)
    cos, sin = jnp.cos(ang), jnp.sin(ang)
    zeros = jnp.zeros((s_len, half), F32)
    right = LANE - lane_off - rot_dim
    pad = lambda a, b: jnp.pad(jnp.concatenate([a, b], axis=-1), ((0, 0), (lane_off, right)))
    cos_t = pad(cos, cos) + jnp.pad(jnp.ones((s_len, lane_off), F32), ((0, 0), (0, LANE - lane_off)))
    sa_t = pad(-sin, zeros)
    sb_t = pad(zeros, sin)
    ident = jnp.pad(jnp.ones((c_len, lane_off + rot_dim), F32), ((0, 0), (0, right)))
    zc = jnp.zeros((c_len, LANE), F32)
    tables = jnp.stack([jnp.concatenate([cos_t, ident]), jnp.concatenate([sa_t, zc]),
                        jnp.concatenate([sb_t, zc])])
    if lane_off == 0 and LANE % rot_dim == 0:
        tables = jnp.tile(tables[..., :rot_dim], (1, 1, LANE // rot_dim))
    return tables


def _tile_lane(v):
    return jnp.tile(v, LANE // v.shape[0]).reshape(1, LANE)


def kernel(x, c, ctx, c_ctx, w_ada, b_ada, norm_mix, norm_ffn, w_in, q_norm_gqa, k_norm_gqa,
           q_lora_norm, kv_lora_norm, w_uq, w_ukv, rpb, w_o_na, w_o_gqa, w_o_mla, w_out,
           w_ffn_gu, w_ffn_dn, w_router, w_moe_gu, w_moe_dn, norm_final):
    b, s_len, d = x.shape
    c_len = ctx.shape[1]
    t_len = s_len + c_len
    depth = w_ada.shape[0]
    n_rows = s_len // GRID_W
    n_tok = b * t_len
    n_tiles = n_tok // TOKEN_TILE

    c_rows = jnp.zeros((16, d), F32).at[:b].set(c).at[b].set(c_ctx)
    mods = _ada_call(c_rows, w_ada, b_ada)
    rope_g = _rope_tables(s_len, c_len, HEAD_DIM, 0)
    rope_m = _rope_tables(s_len, c_len, MLA_ROPE_DIM, MLA_NOPE_DIM)
    na_bias = _na_bias_tables(rpb, n_rows)

    xc = jnp.concatenate([x, ctx], axis=1)
    for i in range(depth):
        lat = mods[i, :b].reshape(b, 1, 6, d)
        cm = jnp.broadcast_to(mods[i, b].reshape(1, 1, 6, d), (b, 1, 6, d))
        mod = jnp.concatenate([lat, cm], axis=1).reshape(2 * b, 6, d)

        w_cat, wuq, wukv = _layer_weights(w_in[i], w_uq[i], w_ukv[i])
        q_na, k_na, v_na, q_g, k_g, v_g, q_m, k_m, v_m, gate, norms = _mixin_call(
            xc, mod, norm_mix[i].reshape(1, d), w_cat, _tile_lane(q_norm_gqa[i]), _tile_lane(k_norm_gqa[i]),
            q_lora_norm[i].reshape(1, -1), kv_lora_norm[i].reshape(1, -1), wuq, wukv, rope_g, rope_m)

        y_na = _na_call(q_na, k_na, v_na, na_bias[i], s_len, c_len)
        nmax = jnp.max(norms, axis=1)
        y_g = _bounded_attention(
            q_g, k_g, v_g, nmax[:, _N_QG:_N_QG + GQA_Q_HEADS], nmax[:, _N_KG:_N_KG + GQA_KV_HEADS], HEAD_DIM,
            n_kv=GQA_KV_HEADS, group=GQA_GROUP, tq=256, q_width=HEAD_DIM, s_len=s_len, c_len=c_len,
            name="gqa_attn")
        y_m = _bounded_attention(
            q_m, k_m, v_m, nmax[:, _N_QM:_N_QM + MLA_HEADS], nmax[:, _N_KM:_N_KM + MLA_HEADS],
            MLA_NOPE_DIM + MLA_ROPE_DIM, n_kv=MLA_HEADS, group=1, tq=512, q_width=LANE, s_len=s_len,
            c_len=c_len, name="mla_attn")
        j = i // 2
        gn = norm_ffn[i].reshape(1, d)
        merge_args = (xc, mod, y_na, y_g, y_m, gate, w_o_na[i].astype(BF16), w_o_gqa[i].astype(BF16),
                      w_o_mla[i].astype(BF16), w_out[i].astype(BF16), gn)
        if i % 2 == 0:
            f_dim = w_ffn_dn.shape[1]
            xc = _merge_call(*merge_args, ffn=(w_ffn_gu[j][:, :f_dim].astype(BF16),
                                               w_ffn_gu[j][:, f_dim:].astype(BF16), w_ffn_dn[j].astype(BF16)))
        else:
            n_exp = w_router.shape[-1]
            wr = jnp.pad(w_router[j], ((0, 0), (0, LANE - n_exp)))
            xc, idx, wts, cnt = _merge_call(*merge_args, router=(wr, n_exp))
            counts = cnt[0, :n_exp].astype(jnp.int32)
            padded = (counts + MOE_ROW_BLOCK - 1) // MOE_ROW_BLOCK * MOE_ROW_BLOCK
            pad_end = jnp.cumsum(padded)
            pad_start = pad_end - padded
            idx = idx.reshape(n_tok, LANE)
            dest = pad_start[idx[:, 0:TOP_K]] + idx[:, TOP_K:2 * TOP_K]
            dest_blocks = dest.reshape(n_tiles, 1, TOP_K * TOKEN_TILE)
            n_blocks = -(-(n_tok * TOP_K + n_exp * (MOE_ROW_BLOCK - 1)) // MOE_ROW_BLOCK)
            starts = jnp.arange(n_blocks) * MOE_ROW_BLOCK
            block_expert = jnp.minimum(jnp.sum(starts[:, None] >= pad_end[None, :], axis=-1),
                                       n_exp - 1).astype(jnp.int32)
            n_used = (pad_end[-1] // MOE_ROW_BLOCK).astype(jnp.int32).reshape(1)
            n_rows_x = n_blocks * MOE_ROW_BLOCK
            zero_from = jnp.concatenate([pad_start + counts, pad_end[-1:]]).astype(jnp.int32)
            zero_cnt = jnp.concatenate([padded - counts, n_rows_x - pad_end[-1:]]).astype(jnp.int32)
            xs = _dispatch_call(zero_from, zero_cnt, dest_blocks, xc, mod, gn, n_rows_x)
            y_rows = _moe_call(block_expert, n_used, xs, w_moe_gu[j].astype(BF16), w_moe_dn[j].astype(BF16))
            if i == depth - 1:
                return _combine_call(dest_blocks, xc, mod, wts, y_rows, norm_final.reshape(1, d), s_len)
            xc = _combine_call(dest_blocks, xc, mod, wts, y_rows)
    return _final_call(xc, norm_final.reshape(1, d), s_len)
```

```python
import functools

import numpy as np
import jax
import jax.numpy as jnp
from jax import lax
from jax.experimental import pallas as pl
from jax.experimental.pallas import tpu as pltpu

GRID_W = 64
HEAD_DIM = 64
NA_HEADS = 4
WIN_H = 8
WIN_W = 16
GQA_Q_HEADS = 8
GQA_KV_HEADS = 2
GQA_GROUP = GQA_Q_HEADS // GQA_KV_HEADS
MLA_HEADS = 4
MLA_Q_RANK = 256
MLA_KV_RANK = 128
MLA_NOPE_DIM = 64
MLA_ROPE_DIM = 32
MLA_V_DIM = 64
ROPE_THETA = 10000.0
TOP_K = 2
NORM_EPS = 1e-6
NA_SCALE = HEAD_DIM ** -0.5
GQA_SCALE = HEAD_DIM ** -0.5
MLA_SCALE = (MLA_NOPE_DIM + MLA_ROPE_DIM) ** -0.5
LOG2_E = 1.4426950408889634

LANE = 128
TOKEN_TILE = 256
NA_Q_ROWS = 4
NA_K_ROWS = NA_Q_ROWS + WIN_H
KV_CHUNK = 512
MOE_ROW_BLOCK = 512
VMEM_LIMIT = 56 * 1024 * 1024
NEG_BIG = -1e30
SAFE_LOG2_RANGE = 80.0
NORM_SLACK = 1.02
_N_QM, _N_KM = 0, MLA_HEADS

F32 = jnp.float32
BF16 = jnp.bfloat16


def _params(sem, vmem=VMEM_LIMIT):
    return pltpu.CompilerParams(dimension_semantics=sem, vmem_limit_bytes=vmem)


def _resident(shape):
    zeros = (0,) * len(shape)
    return pl.BlockSpec(shape, lambda *_: zeros, pipeline_mode=pl.Buffered(1))


def _rms(x, n):
    ss = jnp.sum(x * x, axis=-1, keepdims=True)
    return x * lax.rsqrt(ss * (1.0 / n) + NORM_EPS)


def _dot(a, b):
    return jnp.dot(a, b, preferred_element_type=F32)


def _dot_nt(a, b):
    return lax.dot_general(a, b, (((1,), (1,)), ((), ())), preferred_element_type=F32)


def _silu(x):
    return x * jax.nn.sigmoid(x)


def _ada_kernel(c_ref, w_ref, b_ref, o_ref):
    s = _silu(c_ref[...])
    o_ref[0] = jnp.dot(s, w_ref[0], precision=lax.Precision.HIGHEST,
                       preferred_element_type=F32) + b_ref[0]


def _ada_call(c_rows, w_ada, b_ada):
    depth, d, n = w_ada.shape
    rows = c_rows.shape[0]
    tn = 1536 if n % 1536 == 0 else n
    return pl.pallas_call(
        _ada_kernel,
        out_shape=jax.ShapeDtypeStruct((depth, rows, n), F32),
        grid=(depth, n // tn),
        in_specs=[
            pl.BlockSpec((rows, d), lambda i, j: (0, 0)),
            pl.BlockSpec((1, d, tn), lambda i, j: (i, 0, j)),
            pl.BlockSpec((1, 1, tn), lambda i, j: (i, 0, j)),
        ],
        out_specs=pl.BlockSpec((1, rows, tn), lambda i, j: (i, 0, j)),
        compiler_params=_params(("arbitrary", "arbitrary")),
        name="ada_mod",
    )(c_rows, w_ada, b_ada.reshape(depth, 1, n))


def _rope(xh, cos, sa, sb, shift):
    return xh * cos + pltpu.roll(xh, LANE - shift, 1) * sa + pltpu.roll(xh, shift, 1) * sb


_W_NA = 3 * NA_HEADS * HEAD_DIM
_W_QG = GQA_Q_HEADS * HEAD_DIM
_W_KVG = 2 * GQA_KV_HEADS * LANE
_W_LORA = MLA_Q_RANK + MLA_KV_RANK + LANE
_C_NA = 0
_C_QG = _C_NA + _W_NA
_C_KVG = _C_QG + _W_QG
_C_LORA = _C_KVG + _W_KVG
_C_GATE = _C_LORA + _W_LORA


def _mixin_kernel(x_ref, mod0_ref, mod1_ref, gmix_ref, w_ref, qn_ref, kn_ref, qln_ref, kvln_ref,
                  wuq_ref, wukv_ref, rg0_ref, rg1_ref, rm0_ref, rm1_ref, *out_refs):
    shared = (gmix_ref, w_ref, qn_ref, kn_ref, qln_ref, kvln_ref, wuq_ref, wukv_ref)
    half = x_ref.shape[0] // 2
    gates0 = _mixin_tile(slice(0, half), x_ref, mod0_ref, *shared, rg0_ref, rm0_ref, *out_refs)
    gates1 = _mixin_tile(slice(half, 2 * half), x_ref, mod1_ref, *shared, rg1_ref, rm1_ref, *out_refs)
    gates0()
    gates1()


def _mixin_tile(rows, x_ref, mod_ref, gmix_ref, w_ref, qn_ref, kn_ref, qln_ref, kvln_ref,
                wuq_ref, wukv_ref, rg_ref, rm_ref,
                qna_ref, kna_ref, vna_ref, qg_ref, kg_ref, vg_ref, qm_ref, km_ref, vm_ref,
                gate_ref, nrm_ref):
    x = x_ref[rows, :]
    d = x.shape[-1]
    tm = x.shape[0]
    mod = mod_ref[0]
    h = _rms(x, d) * gmix_ref[...]
    h = h * (1.0 + mod[1:2]) + mod[0:1]
    hb = h.astype(BF16)

    def proj(lo, width):
        return _dot(hb, w_ref[:, lo:lo + width])

    def block(a, j):
        return a[:, j * LANE:(j + 1) * LANE]

    lane = lax.broadcasted_iota(jnp.int32, (tm, LANE), 1)
    low = lane < HEAD_DIM
    ones_pad = jnp.where(low, 0.0, 1.0)
    norms = []

    def unit_at(pos):
        return jnp.where(lane == pos, 1.0, 0.0)

    cos_g, sa_g, sb_g = rg_ref[0], rg_ref[1], rg_ref[2]
    cos_m, sa_m, sb_m = rm_ref[0], rm_ref[1], rm_ref[2]
    g_half = HEAD_DIM // 2
    m_half = MLA_ROPE_DIM // 2

    p_lora = proj(_C_LORA, _W_LORA)
    cq = (_rms(p_lora[:, 0:MLA_Q_RANK], MLA_Q_RANK) * qln_ref[...]).astype(BF16)
    q_up = _dot(cq, wuq_ref[...])
    ckv = p_lora[:, MLA_Q_RANK:MLA_Q_RANK + MLA_KV_RANK]
    ckv = (_rms(ckv, MLA_KV_RANK) * kvln_ref[...]).astype(BF16)
    kv_up = _dot(ckv, wukv_ref[...])
    k_rope = _rope(p_lora[:, MLA_Q_RANK + MLA_KV_RANK:], cos_m, sa_m, sb_m, m_half)
    for hh in range(MLA_HEADS):
        sl = slice(hh * LANE, (hh + 1) * LANE)
        qh = _rope(block(q_up, hh), cos_m, sa_m, sb_m, m_half) * (MLA_SCALE * LOG2_E)
        qm_ref[rows, sl] = qh.astype(BF16)
        kh = block(kv_up, hh) + k_rope
        norms.append((_N_QM + hh, jnp.sum(qh * qh, axis=-1, keepdims=True)))
        norms.append((_N_KM + hh, jnp.sum(kh * kh, axis=-1, keepdims=True)))
        km_ref[rows, sl] = (kh + unit_at(MLA_NOPE_DIM + MLA_ROPE_DIM)).astype(BF16)
        vm_ref[rows, sl] = (block(kv_up, MLA_HEADS + hh) + ones_pad).astype(BF16)
    packed = jnp.zeros((tm, LANE), F32)
    for pos, val in norms:
        packed = jnp.where(lane == pos, val, packed)
    nrm_ref[rows, :] = packed

    p_qg = proj(_C_QG, _W_QG)
    for j in range(GQA_Q_HEADS // 2):
        xp = block(p_qg, j)
        sq = xp * xp
        ss_lo = jnp.sum(jnp.where(low, sq, 0.0), axis=-1, keepdims=True)
        ss_hi = jnp.sum(jnp.where(low, 0.0, sq), axis=-1, keepdims=True)
        inv = jnp.where(low, lax.rsqrt(ss_lo * (1.0 / HEAD_DIM) + NORM_EPS),
                        lax.rsqrt(ss_hi * (1.0 / HEAD_DIM) + NORM_EPS))
        xp = _rope(xp * inv * qn_ref[...], cos_g, sa_g, sb_g, g_half) * (GQA_SCALE * LOG2_E)
        qg_ref[rows, j * LANE:(j + 1) * LANE] = xp.astype(BF16)
    p_kvg = proj(_C_KVG, _W_KVG)
    for hh in range(GQA_KV_HEADS):
        xh = _rope(_rms(block(p_kvg, hh), HEAD_DIM) * kn_ref[...], cos_g, sa_g, sb_g, g_half)
        kg_ref[rows, hh * LANE:(hh + 1) * LANE] = (xh + unit_at(HEAD_DIM)).astype(BF16)
        vh = block(p_kvg, GQA_KV_HEADS + hh) + ones_pad
        vg_ref[rows, hh * LANE:(hh + 1) * LANE] = vh.astype(BF16)

    nw = NA_HEADS * HEAD_DIM
    p_na = proj(_C_NA, _W_NA)
    qna_ref[rows, :] = (p_na[:, 0:nw] * (NA_SCALE * LOG2_E)).astype(BF16)
    kna_ref[rows, :] = p_na[:, nw:2 * nw].astype(BF16)
    vna_ref[rows, :] = p_na[:, 2 * nw:3 * nw].astype(BF16)

    def gates():
        for j in range(3):
            gate_ref[rows, j * d:(j + 1) * d] = jax.nn.sigmoid(proj(_C_GATE + j * d, d)).astype(BF16)

    return gates


def _mixin_call(xc, mod, gmix, w_cat, qn, kn, qln, kvln, wuq, wukv, rope_g, rope_m):
    b, t, d = xc.shape
    tm = TOKEN_TILE
    nt = t // tm
    n_tiles = b * nt
    assert n_tiles % 2 == 0
    tok = lambda w: pl.BlockSpec((2 * tm, w), lambda i: (i, 0))
    widths = [NA_HEADS * HEAD_DIM] * 3 + [GQA_Q_HEADS * HEAD_DIM, GQA_KV_HEADS * LANE,
                                          GQA_KV_HEADS * LANE, MLA_HEADS * LANE,
                                          MLA_HEADS * LANE, MLA_HEADS * LANE, 3 * d]

    def mod_spec(h):
        return pl.BlockSpec((1, 6, d), lambda i: (
            2 * ((2 * i + h) // nt) + ((2 * i + h) % nt == nt - 1).astype(jnp.int32), 0, 0))

    def rope_spec(h):
        return pl.BlockSpec((3, tm, LANE), lambda i: (0, (2 * i + h) % nt, 0))

    outs = pl.pallas_call(
        _mixin_kernel,
        out_shape=[jax.ShapeDtypeStruct((b * t, w), BF16) for w in widths]
        + [jax.ShapeDtypeStruct((b * t, LANE), F32)],
        grid=(n_tiles // 2,),
        in_specs=[
            tok(d), mod_spec(0), mod_spec(1),
            _resident((1, d)),
            _resident(w_cat.shape),
            _resident((1, LANE)), _resident((1, LANE)),
            _resident((1, MLA_Q_RANK)), _resident((1, MLA_KV_RANK)),
            _resident(wuq.shape), _resident(wukv.shape),
            rope_spec(0), rope_spec(1), rope_spec(0), rope_spec(1),
        ],
        out_specs=[tok(w) for w in widths] + [tok(LANE)],
        compiler_params=_params(("arbitrary",)),
        name="mixer_in",
    )(xc.reshape(b * t, d), mod, mod, gmix, w_cat, qn, kn, qln, kvln, wuq, wukv, rope_g, rope_g, rope_m, rope_m)
    return [o.reshape(b, t, o.shape[-1]) for o in outs]


def _na_kernel(q_ref, k_ref, v_ref, bias_ref, o_ref, *, n_groups, n_rows, s_len, c_len):
    g = pl.program_id(1)
    qn = NA_Q_ROWS * GRID_W
    kn = NA_K_ROWS * GRID_W
    lane = lax.broadcasted_iota(jnp.int32, (qn, LANE), 1)
    low = lane < HEAD_DIM

    def attend_pairs(k_nb, v_nb):
        k_c = k_ref[0, s_len:s_len + c_len, :]
        v_c = v_ref[0, s_len:s_len + c_len, :]
        for pair in range(NA_HEADS // 2):
            sl = slice(pair * LANE, (pair + 1) * LANE)
            qp = q_ref[0, :, sl]
            outs = []
            for sub in range(2):
                head = 2 * pair + sub
                qm = jnp.where(low if sub == 0 else jnp.logical_not(low), qp, jnp.zeros_like(qp))
                s_c = _dot_nt(qm, k_c[:, sl])
                m = jnp.max(s_c, axis=-1, keepdims=True)
                if k_nb is not None:
                    s_nb = _dot_nt(qm, k_nb[:, sl]) + bias_ref[0, head]
                    m = jnp.maximum(m, jnp.max(s_nb, axis=-1, keepdims=True))
                p_c = jnp.exp2(s_c - m)
                den = jnp.sum(p_c, axis=-1, keepdims=True)
                o = _dot(p_c.astype(BF16), v_c[:, sl])
                if k_nb is not None:
                    p_nb = jnp.exp2(s_nb - m)
                    den = den + jnp.sum(p_nb, axis=-1, keepdims=True)
                    o = o + _dot(p_nb.astype(BF16), v_nb[:, sl])
                outs.append(o / den)
            o_ref[0, :, sl] = jnp.where(low, outs[0], outs[1]).astype(BF16)

    @pl.when(g < n_groups)
    def _():
        first_row = jnp.clip(NA_Q_ROWS * g - WIN_H // 2, 0, n_rows - NA_K_ROWS)
        start = pl.multiple_of(first_row * GRID_W, GRID_W)
        attend_pairs(k_ref[0, pl.ds(start, kn), :], v_ref[0, pl.ds(start, kn), :])

    @pl.when(g == n_groups)
    def _():
        attend_pairs(None, None)


def _na_call(q, k, v, bias, s_len, c_len):
    b, t, w = q.shape
    qn = NA_Q_ROWS * GRID_W
    kn = NA_K_ROWS * GRID_W
    n_rows = s_len // GRID_W
    n_groups = s_len // qn
    assert c_len == qn and n_rows >= NA_K_ROWS

    def variant(g):
        return jnp.where(g == 0, 0, jnp.where(g >= n_groups - 1, 2, 1))

    return pl.pallas_call(
        functools.partial(_na_kernel, n_groups=n_groups, n_rows=n_rows, s_len=s_len, c_len=c_len),
        out_shape=jax.ShapeDtypeStruct((b, t, w), BF16),
        grid=(b, n_groups + 1),
        in_specs=[
            pl.BlockSpec((1, qn, w), lambda bi, g: (bi, g, 0)),
            pl.BlockSpec((1, t, w), lambda bi, g: (bi, 0, 0)),
            pl.BlockSpec((1, t, w), lambda bi, g: (bi, 0, 0)),
            pl.BlockSpec((1, NA_HEADS, qn, kn), lambda bi, g: (variant(g), 0, 0, 0)),
        ],
        out_specs=pl.BlockSpec((1, qn, w), lambda bi, g: (bi, g, 0)),
        compiler_params=_params(("arbitrary", "arbitrary")),
        name="na_attn",
    )(q, k, v, bias)


def _na_bias_tables(rpb, n_rows):
    qi = np.arange(NA_Q_ROWS)[:, None]
    kj = np.arange(NA_K_ROWS)[None, :]
    qc = np.arange(GRID_W)[:, None]
    kc = np.arange(GRID_W)[None, :]
    col0 = np.clip(qc - WIN_W // 2, 0, GRID_W - WIN_W)
    col_ok = (kc >= col0) & (kc < col0 + WIN_W)
    dc = np.clip(kc - qc, -(WIN_W - 1), WIN_W - 1) + WIN_W - 1
    col_sel = np.eye(2 * WIN_W - 1, dtype=np.float32)[dc]
    row_sel, valid = [], []
    for r0, u0 in ((0, 0), (WIN_H // 2, 0), (n_rows - NA_Q_ROWS, n_rows - NA_K_ROWS)):
        qr = r0 + qi
        kr = u0 + kj
        row0 = np.clip(qr - WIN_H // 2, 0, n_rows - WIN_H)
        row_ok = (kr >= row0) & (kr < row0 + WIN_H)
        dr = np.clip(kr - qr, -(WIN_H - 1), WIN_H - 1) + WIN_H - 1
        row_sel.append(np.eye(2 * WIN_H - 1, dtype=np.float32)[dr])
        valid.append(row_ok[:, None, :, None] & col_ok[None, :, None, :])
    row_sel = jnp.asarray(np.stack(row_sel))
    valid = np.stack(valid).reshape(3, 1, NA_Q_ROWS * GRID_W, NA_K_ROWS * GRID_W)
    picked = jnp.einsum('vija,lhab,qkb->lvhiqjk', row_sel, rpb.astype(F32) * LOG2_E, jnp.asarray(col_sel),
                        precision=lax.Precision.HIGHEST)
    picked = picked.reshape(rpb.shape[0], 3, NA_HEADS, NA_Q_ROWS * GRID_W, NA_K_ROWS * GRID_W)
    return jnp.where(valid[None], picked, NEG_BIG)


def _row_max(s):
    parts = [s[:, j * LANE:(j + 1) * LANE] for j in range(s.shape[1] // LANE)]
    while len(parts) > 1:
        parts = [jnp.maximum(parts[j], parts[j + 1]) for j in range(0, len(parts) - 1, 2)] + parts[len(parts) & ~1:]
    return jnp.max(parts[0], axis=-1, keepdims=True)


def _flash_kernel(q_ref, k_ref, v_ref, u_ref, o_ref, *, n_kv, group, tq, q_width, s_len, c_len,
                  with_latent, online):
    low = lax.broadcasted_iota(jnp.int32, (tq, LANE), 1) < HEAD_DIM
    heads = range(n_kv)
    lanes = [slice(h * LANE, (h + 1) * LANE) for h in heads]

    def load_q(head):
        if q_width == LANE:
            qh = q_ref[0, :, head * LANE:(head + 1) * LANE]
        else:
            pair = q_ref[0, :, (head // 2) * LANE:(head // 2 + 1) * LANE]
            if head % 2:
                pair = pltpu.roll(pair.astype(F32), HEAD_DIM, 1).astype(BF16)
            qh = jnp.where(low, pair, jnp.zeros_like(pair))
        return qh if online else qh + u_ref[0, head:head + 1, :]

    qs = []
    for h in heads:
        parts = [load_q(h * group + g) for g in range(group)]
        qs.append(parts[0] if group == 1 else jnp.concatenate(parts, axis=0))

    if online:
        carry = []
        for h in heads:
            s = _dot_nt(qs[h], k_ref[0, s_len:s_len + c_len, lanes[h]])
            m = _row_max(s)
            carry += [m, _dot(jnp.exp2(s - m).astype(BF16), v_ref[0, s_len:s_len + c_len, lanes[h]])]
        if with_latent:
            def body(i, carry):
                st = pl.multiple_of(i * KV_CHUNK, KV_CHUNK)
                new = []
                for h in heads:
                    m_old, acc_old = carry[2 * h], carry[2 * h + 1]
                    sc = _dot_nt(qs[h], k_ref[0, pl.ds(st, KV_CHUNK), lanes[h]])
                    m_new = jnp.maximum(m_old, _row_max(sc))
                    p = jnp.exp2(sc - m_new).astype(BF16)
                    pv = _dot(p, v_ref[0, pl.ds(st, KV_CHUNK), lanes[h]])
                    new += [m_new, jnp.exp2(m_old - m_new) * acc_old + pv]
                return tuple(new)
            carry = lax.fori_loop(0, s_len // KV_CHUNK, body, tuple(carry), unroll=2)
        accs = [carry[2 * h + 1] for h in heads]
    else:
        chunks = [(s_len, c_len)]
        if with_latent:
            chunks += [(i * KV_CHUNK, KV_CHUNK) for i in range(s_len // KV_CHUNK)]
        accs = [None] * n_kv
        for st, size in chunks:
            for h in heads:
                p = jnp.exp2(_dot_nt(qs[h], k_ref[0, st:st + size, lanes[h]])).astype(BF16)
                pv = _dot(p, v_ref[0, st:st + size, lanes[h]])
                accs[h] = pv if accs[h] is None else accs[h] + pv
    outs = []
    for h in heads:
        acc = accs[h]
        o = acc / pltpu.roll(acc, HEAD_DIM, 1)
        for g in range(group):
            outs.append(o[g * tq:(g + 1) * tq])
    for pair in range(len(outs) // 2):
        packed = jnp.where(low, outs[2 * pair], pltpu.roll(outs[2 * pair + 1], HEAD_DIM, 1))
        o_ref[0, :, pair * LANE:(pair + 1) * LANE] = packed.astype(BF16)


def _flash_call(q, k, v, shift, *, n_kv, group, tq, q_width, s_len, c_len, online, name):
    b, t, _ = q.shape
    n_heads = n_kv * group
    tq_ctx = min(tq, c_len)
    assert s_len % tq == 0 and c_len % tq_ctx == 0 and s_len % tq_ctx == 0 and s_len % KV_CHUNK == 0
    kv_spec = pl.BlockSpec((1, t, n_kv * LANE), lambda bi, ti: (bi, 0, 0))
    u_spec = pl.BlockSpec((1, n_heads, LANE), lambda bi, ti: (bi, 0, 0))
    common = dict(n_kv=n_kv, group=group, q_width=q_width, s_len=s_len, c_len=c_len, online=online)
    width = n_heads * HEAD_DIM
    tag = "_online" if online else ""
    y_lat = pl.pallas_call(
        functools.partial(_flash_kernel, tq=tq, with_latent=True, **common),
        out_shape=jax.ShapeDtypeStruct((b, s_len, width), BF16),
        grid=(b, s_len // tq),
        in_specs=[pl.BlockSpec((1, tq, n_heads * q_width), lambda bi, ti: (bi, ti, 0)), kv_spec, kv_spec, u_spec],
        out_specs=pl.BlockSpec((1, tq, width), lambda bi, ti: (bi, ti, 0)),
        compiler_params=_params(("arbitrary", "arbitrary")),
        name=name + tag,
    )(q, k, v, shift)
    first = s_len // tq_ctx
    y_ctx = pl.pallas_call(
        functools.partial(_flash_kernel, tq=tq_ctx, with_latent=False, **common),
        out_shape=jax.ShapeDtypeStruct((b, c_len, width), BF16),
        grid=(b, c_len // tq_ctx),
        in_specs=[pl.BlockSpec((1, tq_ctx, n_heads * q_width), lambda bi, ti: (bi, first + ti, 0)), kv_spec, kv_spec,
                  u_spec],
        out_specs=pl.BlockSpec((1, tq_ctx, width), lambda bi, ti: (bi, ti, 0)),
        compiler_params=_params(("arbitrary", "arbitrary")),
        name=name + "_ctx" + tag,
    )(q, k, v, shift)
    return y_lat, y_ctx


def _bounded_attention(q, k, v, bound, unit_lane, **kw):
    bound = bound * NORM_SLACK
    lane = jnp.arange(LANE) == unit_lane
    shift = jnp.where(lane[None, None, :], -bound[:, :, None], 0.0).astype(BF16)
    safe = jnp.max(bound) * 2.0 < SAFE_LOG2_RANGE
    return lax.cond(safe,
                    lambda ops: _flash_call(*ops, online=False, **kw),
                    lambda ops: _flash_call(*ops, online=True, **kw),
                    (q, k, v, shift))


def _mod_spec(d, nt):
    return pl.BlockSpec((1, 6, d), lambda bi, ti: (2 * bi + (ti == nt - 1).astype(jnp.int32), 0, 0))


def _merged_residual(x_ref, mod_ref, yna_ref, yg_ref, ygc_ref, ym_ref, ymc_ref, gate_ref, wna_ref, wg_ref,
                     wm_ref, wout_ref, n_lat_tiles):
    d = x_ref.shape[-1]
    is_ctx = pl.program_id(1) >= n_lat_tiles
    y_g = jnp.where(is_ctx, ygc_ref[0], yg_ref[0])
    y_m = jnp.where(is_ctx, ymc_ref[0], ym_ref[0])
    m = gate_ref[0, :, 0:d].astype(F32) * _dot(yna_ref[0], wna_ref[...])
    m = m + gate_ref[0, :, d:2 * d].astype(F32) * _dot(y_g, wg_ref[...])
    m = m + gate_ref[0, :, 2 * d:3 * d].astype(F32) * _dot(y_m, wm_ref[...])
    r = _dot(m.astype(BF16), wout_ref[...])
    return x_ref[0] + mod_ref[0][2:3] * r


def _norm2(x, mod, gain):
    h = _rms(x, x.shape[-1]) * gain
    return h * (1.0 + mod[4:5]) + mod[3:4]


def _merge_ffn_kernel(*refs, n_lat_tiles):
    mix_refs, (gn_ref, wg_ref, wu_ref, wd_ref, o_ref) = refs[:12], refs[12:]
    x = _merged_residual(*mix_refs, n_lat_tiles)
    mod = mix_refs[1][0]
    hb = _norm2(x, mod, gn_ref[...]).astype(BF16)
    act = (_silu(_dot(hb, wg_ref[...])) * _dot(hb, wu_ref[...])).astype(BF16)
    o_ref[0] = x + mod[5:6] * _dot(act, wd_ref[...])


def _merge_router_kernel(*refs, n_lat_tiles, n_experts):
    mix_refs, (gn_ref, wr_ref, o_ref, idx_ref, wt_ref, cnt_ref, carry_ref) = refs[:12], refs[12:]

    @pl.when((pl.program_id(0) == 0) & (pl.program_id(1) == 0))
    def _():
        carry_ref[...] = jnp.zeros_like(carry_ref)

    x = _merged_residual(*mix_refs, n_lat_tiles)
    o_ref[0] = x
    tm = x.shape[0]
    h = _norm2(x, mix_refs[1][0], gn_ref[...])
    logits = jnp.dot(h, wr_ref[...], precision=lax.Precision.HIGHEST, preferred_element_type=F32)
    lane = lax.broadcasted_iota(jnp.int32, (tm, LANE), 1).astype(F32)
    logits = jnp.where(lane < n_experts, logits, -jnp.inf)
    m1 = jnp.max(logits, axis=-1, keepdims=True)
    i1 = jnp.min(jnp.where(logits == m1, lane, float(LANE)), axis=-1, keepdims=True)
    rest = jnp.where(lane == i1, -jnp.inf, logits)
    m2 = jnp.max(rest, axis=-1, keepdims=True)
    i2 = jnp.min(jnp.where(rest == m2, lane, float(LANE)), axis=-1, keepdims=True)
    e2 = jnp.exp(m2 - m1)
    w1 = 1.0 / (1.0 + e2)
    w2 = e2 / (1.0 + e2)
    hot1 = lane == i1
    hot2 = lane == i2
    hot = jnp.where(hot1 | hot2, 1.0, 0.0)
    rows = lax.broadcasted_iota(jnp.int32, (tm, tm), 0)
    cols = lax.broadcasted_iota(jnp.int32, (tm, tm), 1)
    below = jnp.where(rows > cols, 1.0, 0.0).astype(BF16)
    before = _dot(below, hot.astype(BF16)) + carry_ref[0:1, :]
    r1 = jnp.sum(jnp.where(hot1, before, 0.0), axis=-1, keepdims=True)
    r2 = jnp.sum(jnp.where(hot2, before, 0.0), axis=-1, keepdims=True)
    total = carry_ref[0:1, :] + jnp.sum(hot, axis=0, keepdims=True)
    carry_ref[...] = jnp.broadcast_to(total, carry_ref.shape)
    cnt_ref[...] = jnp.broadcast_to(total, cnt_ref.shape)
    packed = jnp.where(lane == 0, i1, jnp.where(lane == 1, i2, jnp.where(lane == 2, r1, jnp.where(lane == 3, r2, 0.0))))
    idx_ref[0] = packed.astype(jnp.int32)
    wt_ref[0] = jnp.where(lane == 0, w1, jnp.where(lane == 1, w2, 0.0))


def _merge_call(xc, mod, y_na, y_g, y_m, gate, w_na, w_g, w_m, w_out, gn, *, ffn=None, router=None):
    b, t, d = xc.shape
    tm = TOKEN_TILE
    nt = t // tm
    n_lat = y_g[0].shape[1] // tm
    tok = lambda w: pl.BlockSpec((1, tm, w), lambda bi, ti: (bi, ti, 0))
    lat = lambda w: pl.BlockSpec((1, tm, w), lambda bi, ti: (bi, jnp.minimum(ti, n_lat - 1), 0))
    ctx = lambda w: pl.BlockSpec((1, tm, w), lambda bi, ti: (bi, jnp.maximum(ti - n_lat, 0), 0))
    wg_, wm_ = y_g[0].shape[-1], y_m[0].shape[-1]
    mix_specs = [tok(d), _mod_spec(d, nt), tok(y_na.shape[-1]), lat(wg_), ctx(wg_), lat(wm_), ctx(wm_),
                 tok(3 * d), _resident(w_na.shape), _resident(w_g.shape), _resident(w_m.shape),
                 _resident(w_out.shape), _resident((1, d))]
    mix_args = (xc, mod, y_na, y_g[0], y_g[1], y_m[0], y_m[1], gate, w_na, w_g, w_m, w_out, gn)
    x_shape = jax.ShapeDtypeStruct(xc.shape, F32)
    if ffn is not None:
        return pl.pallas_call(
            functools.partial(_merge_ffn_kernel, n_lat_tiles=n_lat),
            out_shape=x_shape,
            grid=(b, nt),
            in_specs=mix_specs + [_resident(w.shape) for w in ffn],
            out_specs=tok(d),
            input_output_aliases={0: 0},
            compiler_params=_params(("arbitrary", "arbitrary")),
            name="merge_ffn",
        )(*mix_args, *ffn)
    w_router_pad, n_experts = router
    return pl.pallas_call(
        functools.partial(_merge_router_kernel, n_lat_tiles=n_lat, n_experts=n_experts),
        out_shape=[x_shape, jax.ShapeDtypeStruct((b, t, LANE), jnp.int32),
                   jax.ShapeDtypeStruct((b, t, LANE), F32), jax.ShapeDtypeStruct((8, LANE), F32)],
        grid=(b, nt),
        in_specs=mix_specs + [_resident(w_router_pad.shape)],
        out_specs=[tok(d), tok(LANE), tok(LANE), pl.BlockSpec((8, LANE), lambda bi, ti: (0, 0))],
        scratch_shapes=[pltpu.VMEM((8, LANE), F32)],
        input_output_aliases={0: 0},
        compiler_params=_params(("arbitrary", "arbitrary")),
        name="merge_router",
    )(*mix_args, w_router_pad)


def _row_copy(src, src_row, dst, dst_row, sem):
    return pltpu.make_async_copy(src.at[pl.ds(src_row, 1)], dst.at[pl.ds(dst_row, 1)], sem)


def _dispatch_kernel(pad_from_ref, pad_cnt_ref, dest_ref, x_ref, mod_ref, gn_ref, xs_ref, h_buf, zrow, sem,
                     zsem, *, n_tiles, n_exp):
    i = pl.program_id(0)
    slot = i % 2
    tm = x_ref.shape[1]

    @pl.when(i == 0)
    def _():
        zrow[...] = jnp.zeros_like(zrow)
        for e in range(n_exp):
            def zero_issue(r, carry, e=e):
                _row_copy(zrow, 0, xs_ref, pad_from_ref[e] + r, zsem).start()
                return carry
            lax.fori_loop(0, pad_cnt_ref[e], zero_issue, 0)
        for e in range(n_exp):
            def zero_drain(r, carry):
                _row_copy(zrow, 0, xs_ref, 0, zsem).wait()
                return carry
            lax.fori_loop(0, pad_cnt_ref[e], zero_drain, 0)

    @pl.when(i < n_tiles)
    def _():
        h_buf[slot] = _norm2(x_ref[0], mod_ref[0], gn_ref[...])

        def issue(r, carry):
            for k in range(TOP_K):
                _row_copy(h_buf.at[slot], r, xs_ref, dest_ref[0, 0, TOP_K * r + k], sem.at[slot]).start()
            return carry

        lax.fori_loop(0, tm, issue, 0, unroll=8)

    @pl.when(i > 0)
    def _():
        def drain(r, carry):
            for k in range(TOP_K):
                _row_copy(h_buf.at[1 - slot], 0, xs_ref, 0, sem.at[1 - slot]).wait()
            return carry

        lax.fori_loop(0, tm, drain, 0, unroll=8)


def _dispatch_call(pad_from, pad_cnt, dest_blocks, xc, mod, gn, n_rows):
    b, t, d = xc.shape
    tm = TOKEN_TILE
    nt = t // tm
    n_tiles = b * nt

    def tile(i):
        j = jnp.minimum(i, n_tiles - 1)
        return j // nt, j % nt

    grid_spec = pltpu.PrefetchScalarGridSpec(
        num_scalar_prefetch=2,
        grid=(n_tiles + 1,),
        in_specs=[
            pl.BlockSpec((1, 1, dest_blocks.shape[-1]), lambda i, pf, pc: (jnp.minimum(i, n_tiles - 1), 0, 0),
                         memory_space=pltpu.SMEM),
            pl.BlockSpec((1, tm, d), lambda i, pf, pc: (*tile(i), 0)),
            pl.BlockSpec((1, 6, d), lambda i, pf, pc: (
                2 * tile(i)[0] + (tile(i)[1] == nt - 1).astype(jnp.int32), 0, 0)),
            pl.BlockSpec((1, d), lambda i, pf, pc: (0, 0)),
        ],
        out_specs=pl.BlockSpec(memory_space=pl.ANY),
        scratch_shapes=[pltpu.VMEM((2, tm, d), F32), pltpu.VMEM((8, d), F32),
                        pltpu.SemaphoreType.DMA((2,)), pltpu.SemaphoreType.DMA],
    )
    return pl.pallas_call(
        functools.partial(_dispatch_kernel, n_tiles=n_tiles, n_exp=pad_from.shape[0]),
        out_shape=jax.ShapeDtypeStruct((n_rows, d), F32),
        grid_spec=grid_spec,
        compiler_params=_params(("arbitrary",)),
        name="moe_dispatch",
    )(pad_from, pad_cnt, dest_blocks, xc, mod, gn)


def _moe_kernel(be_ref, nu_ref, xs_ref, wgu_ref, wd_ref, y_ref, *, f_chunk):
    f_dim = wd_ref.shape[1]

    @pl.when(pl.program_id(0) < nu_ref[0])
    def _():
        xb = xs_ref[...].astype(BF16)
        acc = None
        for c in range(0, f_dim, f_chunk):
            gate = _dot(xb, wgu_ref[0, :, c:c + f_chunk])
            up = _dot(xb, wgu_ref[0, :, f_dim + c:f_dim + c + f_chunk])
            part = _dot((_silu(gate) * up).astype(BF16), wd_ref[0, c:c + f_chunk, :])
            acc = part if acc is None else acc + part
        y_ref[...] = acc

    @pl.when(pl.program_id(0) >= nu_ref[0])
    def _():
        y_ref[...] = jnp.zeros_like(y_ref)


def _moe_call(block_expert, n_used, xs, w_gu, w_dn):
    rows, d = xs.shape
    n_exp, _, two_f = w_gu.shape
    f_dim = two_f // 2
    f_chunk = 2 * LANE if f_dim % (2 * LANE) == 0 else f_dim
    tmr = MOE_ROW_BLOCK
    grid_spec = pltpu.PrefetchScalarGridSpec(
        num_scalar_prefetch=2,
        grid=(rows // tmr,),
        in_specs=[
            pl.BlockSpec((tmr, d), lambda i, be, nu: (jnp.minimum(i, nu[0] - 1), 0)),
            pl.BlockSpec((1, d, two_f), lambda i, be, nu: (be[i], 0, 0)),
            pl.BlockSpec((1, f_dim, d), lambda i, be, nu: (be[i], 0, 0)),
        ],
        out_specs=pl.BlockSpec((tmr, d), lambda i, be, nu: (i, 0)),
    )
    return pl.pallas_call(
        functools.partial(_moe_kernel, f_chunk=f_chunk),
        out_shape=jax.ShapeDtypeStruct((rows, d), F32),
        grid_spec=grid_spec,
        compiler_params=_params(("arbitrary",)),
        name="moe_experts",
    )(block_expert, n_used, xs, w_gu, w_dn)


def _combine_kernel(dest_ref, x_ref, mod_ref, wt_ref, gain_ref, y_ref, o_ref, buf1, buf2, sem, *, n_tiles,
                    final):
    i = pl.program_id(0)
    slot = i % 2
    tm = x_ref.shape[1]

    @pl.when(i < n_tiles)
    def _():
        def issue(r, carry):
            _row_copy(y_ref, dest_ref[0, 0, TOP_K * r], buf1.at[slot], r, sem.at[slot]).start()
            _row_copy(y_ref, dest_ref[0, 0, TOP_K * r + 1], buf2.at[slot], r, sem.at[slot]).start()
            return carry

        lax.fori_loop(0, tm, issue, 0, unroll=8)

    @pl.when(i > 0)
    def _():
        def drain(r, carry):
            _row_copy(y_ref, 0, buf1.at[1 - slot], 0, sem.at[1 - slot]).wait()
            _row_copy(y_ref, 0, buf2.at[1 - slot], 0, sem.at[1 - slot]).wait()
            return carry

        lax.fori_loop(0, tm, drain, 0, unroll=8)
        wt = wt_ref[0]
        mix = wt[:, 0:1] * buf1[1 - slot] + wt[:, 1:2] * buf2[1 - slot]
        x = x_ref[0] + mod_ref[0][5:6] * mix
        o_ref[0] = _rms(x, x.shape[-1]) * gain_ref[...] if final else x


def _combine_call(dest_blocks, xc, mod, wts, y_rows, final_gain=None, s_len=None):
    b, t, d = xc.shape
    tm = TOKEN_TILE
    nt = t // tm
    final = final_gain is not None
    used = s_len // tm if final else nt
    n_tiles = b * used

    def issued(i):
        j = jnp.minimum(i, n_tiles - 1)
        return j // used, j % used

    def finished(i):
        j = jnp.maximum(i - 1, 0)
        return j // used, j % used

    tok = lambda w: pl.BlockSpec((1, tm, w), lambda i: (*finished(i), 0))
    gain = final_gain if final else jnp.ones((1, d), F32)
    return pl.pallas_call(
        functools.partial(_combine_kernel, n_tiles=n_tiles, final=final),
        out_shape=jax.ShapeDtypeStruct((b, used * tm, d), F32),
        grid=(n_tiles + 1,),
        in_specs=[
            pl.BlockSpec((1, 1, dest_blocks.shape[-1]), lambda i: (issued(i)[0] * nt + issued(i)[1], 0, 0),
                         memory_space=pltpu.SMEM),
            tok(d),
            pl.BlockSpec((1, 6, d), lambda i: (
                2 * finished(i)[0] + (finished(i)[1] == nt - 1).astype(jnp.int32), 0, 0)),
            tok(LANE), pl.BlockSpec((1, d), lambda i: (0, 0)),
            pl.BlockSpec(memory_space=pl.ANY),
        ],
        out_specs=tok(d),
        scratch_shapes=[pltpu.VMEM((2, tm, d), F32), pltpu.VMEM((2, tm, d), F32),
                        pltpu.SemaphoreType.DMA((2,))],
        input_output_aliases={} if final else {1: 0},
        compiler_params=_params(("arbitrary",)),
        name="moe_combine_final" if final else "moe_combine",
    )(dest_blocks, xc, mod, wts, gain, y_rows)


def _final_kernel(x_ref, g_ref, o_ref):
    x = x_ref[0]
    o_ref[0] = _rms(x, x.shape[-1]) * g_ref[...]


def _final_call(xc, gain, s_len):
    b, t, d = xc.shape
    tm = TOKEN_TILE
    tok = pl.BlockSpec((1, tm, d), lambda bi, ti: (bi, ti, 0))
    return pl.pallas_call(
        _final_kernel,
        out_shape=jax.ShapeDtypeStruct((b, s_len, d), F32),
        grid=(b, s_len // tm),
        in_specs=[tok, _resident((1, d))],
        out_specs=tok,
        compiler_params=_params(("arbitrary", "arbitrary")),
        name="final_norm",
    )(xc, gain)


def _pad_heads(w, n_heads, width):
    k = w.shape[0]
    w = w.reshape(k, n_heads, width)
    return jnp.pad(w, ((0, 0), (0, 0), (0, LANE - width))).reshape(k, n_heads * LANE)


def _layer_weights(w_in, w_uq, w_ukv):
    d = w_in.shape[0]
    nw = NA_HEADS * HEAD_DIM
    gk = GQA_KV_HEADS * HEAD_DIM
    o = 0
    k_na = w_in[:, o:o + nw]; o += nw
    v_na = w_in[:, o:o + nw]; o += nw
    k_g = w_in[:, o:o + gk]; o += gk
    v_g = w_in[:, o:o + gk]; o += gk
    c_kv = w_in[:, o:o + MLA_KV_RANK]; o += MLA_KV_RANK
    k_r = w_in[:, o:o + MLA_ROPE_DIM]; o += MLA_ROPE_DIM
    q_na = w_in[:, o:o + nw]; o += nw
    q_g = w_in[:, o:o + GQA_Q_HEADS * HEAD_DIM]; o += GQA_Q_HEADS * HEAD_DIM
    c_q = w_in[:, o:o + MLA_Q_RANK]; o += MLA_Q_RANK
    gate = w_in[:, o:]
    k_r_pad = jnp.pad(k_r, ((0, 0), (MLA_NOPE_DIM, LANE - MLA_NOPE_DIM - MLA_ROPE_DIM)))
    w_cat = jnp.concatenate([
        q_na, k_na, v_na, q_g, _pad_heads(k_g, GQA_KV_HEADS, HEAD_DIM), _pad_heads(v_g, GQA_KV_HEADS, HEAD_DIM), c_q, c_kv, k_r_pad, gate],
        axis=1).astype(BF16)
    assert w_cat.shape[1] == _C_GATE + 3 * d
    wuq = _pad_heads(w_uq, MLA_HEADS, MLA_NOPE_DIM + MLA_ROPE_DIM).astype(BF16)
    kv = w_ukv.reshape(MLA_KV_RANK, MLA_HEADS, MLA_NOPE_DIM + MLA_V_DIM)
    wuk = _pad_heads(kv[:, :, :MLA_NOPE_DIM].reshape(MLA_KV_RANK, -1), MLA_HEADS, MLA_NOPE_DIM)
    wuv = _pad_heads(kv[:, :, MLA_NOPE_DIM:].reshape(MLA_KV_RANK, -1), MLA_HEADS, MLA_V_DIM)
    return w_cat, wuq, jnp.concatenate([wuk, wuv], axis=1).astype(BF16)


def _rope_tables(s_len, c_len, rot_dim, lane_off):
    half = rot_dim // 2
    n_freq = half // 2
    pos = jnp.arange(s_len)
    rows = (pos // GRID_W).astype(F32)
    cols = (pos % GRID_W).astype(F32)
    inv = jnp.power(ROPE_THETA, -jnp.arange(n_freq, dtype=F32) / n_freq)
    ang = jnp.concatenate([rows[:, None] * inv, cols[:, None] * inv], axis=-1)
    cos, sin = jnp.cos(ang), jnp.sin(ang)
    zeros = jnp.zeros((s_len, half), F32)
    right = LANE - lane_off - rot_dim
    pad = lambda a, b: jnp.pad(jnp.concatenate([a, b], axis=-1), ((0, 0), (lane_off, right)))
    cos_t = pad(cos, cos) + jnp.pad(jnp.ones((s_len, lane_off), F32), ((0, 0), (0, LANE - lane_off)))
    sa_t = pad(-sin, zeros)
    sb_t = pad(zeros, sin)
    ident = jnp.pad(jnp.ones((c_len, lane_off + rot_dim), F32), ((0, 0), (0, right)))
    zc = jnp.zeros((c_len, LANE), F32)
    tables = jnp.stack([jnp.concatenate([cos_t, ident]), jnp.concatenate([sa_t, zc]),
                        jnp.concatenate([sb_t, zc])])
    if lane_off == 0 and LANE % rot_dim == 0:
        tables = jnp.tile(tables[..., :rot_dim], (1, 1, LANE // rot_dim))
    return tables


def _tile_lane(v):
    return jnp.tile(v, LANE // v.shape[0]).reshape(1, LANE)


def kernel(x, c, ctx, c_ctx, w_ada, b_ada, norm_mix, norm_ffn, w_in, q_norm_gqa, k_norm_gqa,
           q_lora_norm, kv_lora_norm, w_uq, w_ukv, rpb, w_o_na, w_o_gqa, w_o_mla, w_out,
           w_ffn_gu, w_ffn_dn, w_router, w_moe_gu, w_moe_dn, norm_final):
    b, s_len, d = x.shape
    c_len = ctx.shape[1]
    t_len = s_len + c_len
    depth = w_ada.shape[0]
    n_rows = s_len // GRID_W
    n_tok = b * t_len
    n_tiles = n_tok // TOKEN_TILE

    c_rows = jnp.zeros((16, d), F32).at[:b].set(c).at[b].set(c_ctx)
    mods = _ada_call(c_rows, w_ada, b_ada)
    rope_g = _rope_tables(s_len, c_len, HEAD_DIM, 0)
    rope_m = _rope_tables(s_len, c_len, MLA_ROPE_DIM, MLA_NOPE_DIM)
    na_bias = _na_bias_tables(rpb, n_rows)

    xc = jnp.concatenate([x, ctx], axis=1)
    for i in range(depth):
        lat = mods[i, :b].reshape(b, 1, 6, d)
        cm = jnp.broadcast_to(mods[i, b].reshape(1, 1, 6, d), (b, 1, 6, d))
        mod = jnp.concatenate([lat, cm], axis=1).reshape(2 * b, 6, d)

        w_cat, wuq, wukv = _layer_weights(w_in[i], w_uq[i], w_ukv[i])
        q_na, k_na, v_na, q_g, k_g, v_g, q_m, k_m, v_m, gate, norms = _mixin_call(
            xc, mod, norm_mix[i].reshape(1, d), w_cat, _tile_lane(q_norm_gqa[i]), _tile_lane(k_norm_gqa[i]),
            q_lora_norm[i].reshape(1, -1), kv_lora_norm[i].reshape(1, -1), wuq, wukv, rope_g, rope_m)

        y_na = _na_call(q_na, k_na, v_na, na_bias[i], s_len, c_len)
        u_g = (HEAD_DIM * GQA_SCALE * LOG2_E) * jnp.max(jnp.abs(q_norm_gqa[i])) * jnp.max(jnp.abs(k_norm_gqa[i]))
        y_g = _bounded_attention(
            q_g, k_g, v_g, jnp.full((b, GQA_Q_HEADS), u_g, F32), HEAD_DIM,
            n_kv=GQA_KV_HEADS, group=GQA_GROUP, tq=256, q_width=HEAD_DIM, s_len=s_len, c_len=c_len,
            name="gqa_attn")
        nmax = jnp.max(norms, axis=1)
        u_m = jnp.sqrt(nmax[:, _N_QM:_N_QM + MLA_HEADS] * nmax[:, _N_KM:_N_KM + MLA_HEADS])
        y_m = _bounded_attention(
            q_m, k_m, v_m, u_m, MLA_NOPE_DIM + MLA_ROPE_DIM, n_kv=MLA_HEADS, group=1, tq=512, q_width=LANE,
            s_len=s_len, c_len=c_len, name="mla_attn")
        j = i // 2
        gn = norm_ffn[i].reshape(1, d)
        merge_args = (xc, mod, y_na, y_g, y_m, gate, w_o_na[i].astype(BF16), w_o_gqa[i].astype(BF16),
                      w_o_mla[i].astype(BF16), w_out[i].astype(BF16), gn)
        if i % 2 == 0:
            f_dim = w_ffn_dn.shape[1]
            xc = _merge_call(*merge_args, ffn=(w_ffn_gu[j][:, :f_dim].astype(BF16),
                                               w_ffn_gu[j][:, f_dim:].astype(BF16), w_ffn_dn[j].astype(BF16)))
        else:
            n_exp = w_router.shape[-1]
            wr = jnp.pad(w_router[j], ((0, 0), (0, LANE - n_exp)))
            xc, idx, wts, cnt = _merge_call(*merge_args, router=(wr, n_exp))
            counts = cnt[0, :n_exp].astype(jnp.int32)
            padded = (counts + MOE_ROW_BLOCK - 1) // MOE_ROW_BLOCK * MOE_ROW_BLOCK
            pad_end = jnp.cumsum(padded)
            pad_start = pad_end - padded
            idx = idx.reshape(n_tok, LANE)
            dest = pad_start[idx[:, 0:TOP_K]] + idx[:, TOP_K:2 * TOP_K]
            dest_blocks = dest.reshape(n_tiles, 1, TOP_K * TOKEN_TILE)
            n_blocks = -(-(n_tok * TOP_K + n_exp * (MOE_ROW_BLOCK - 1)) // MOE_ROW_BLOCK)
            starts = jnp.arange(n_blocks) * MOE_ROW_BLOCK
            block_expert = jnp.minimum(jnp.sum(starts[:, None] >= pad_end[None, :], axis=-1),
                                       n_exp - 1).astype(jnp.int32)
            n_used = (pad_end[-1] // MOE_ROW_BLOCK).astype(jnp.int32).reshape(1)
            n_rows_x = n_blocks * MOE_ROW_BLOCK
            zero_from = jnp.concatenate([pad_start + counts, pad_end[-1:]]).astype(jnp.int32)
            zero_cnt = jnp.concatenate([padded - counts, n_rows_x - pad_end[-1:]]).astype(jnp.int32)
            xs = _dispatch_call(zero_from, zero_cnt, dest_blocks, xc, mod, gn, n_rows_x)
            y_rows = _moe_call(block_expert, n_used, xs, w_moe_gu[j].astype(BF16), w_moe_dn[j].astype(BF16))
            if i == depth - 1:
                return _combine_call(dest_blocks, xc, mod, wts, y_rows, norm_final.reshape(1, d), s_len)
            xc = _combine_call(dest_blocks, xc, mod, wts, y_rows)
    return _final_call(xc, norm_final.reshape(1, d), s_len)
```

```python
import functools

import numpy as np
import jax
import jax.numpy as jnp
from jax import lax
from jax.experimental import pallas as pl
from jax.experimental.pallas import tpu as pltpu

GRID_W = 64
HEAD_DIM = 64
NA_HEADS = 4
WIN_H = 8
WIN_W = 16
GQA_Q_HEADS = 8
GQA_KV_HEADS = 2
GQA_GROUP = GQA_Q_HEADS // GQA_KV_HEADS
MLA_HEADS = 4
MLA_Q_RANK = 256
MLA_KV_RANK = 128
MLA_NOPE_DIM = 64
MLA_ROPE_DIM = 32
MLA_V_DIM = 64
ROPE_THETA = 10000.0
TOP_K = 2
NORM_EPS = 1e-6
NA_SCALE = HEAD_DIM ** -0.5
GQA_SCALE = HEAD_DIM ** -0.5
MLA_SCALE = (MLA_NOPE_DIM + MLA_ROPE_DIM) ** -0.5
LOG2_E = 1.4426950408889634

LANE = 128
TOKEN_TILE = 256
NA_Q_ROWS = 4
NA_K_ROWS = NA_Q_ROWS + WIN_H
KV_CHUNK = 512
MOE_ROW_BLOCK = 512
VMEM_LIMIT = 56 * 1024 * 1024
NEG_BIG = -1e30
SAFE_LOG2_RANGE = 80.0
NORM_SLACK = 1.02
_N_QM, _N_KM = 0, MLA_HEADS
_N_QNA, _N_KNA = 2 * MLA_HEADS, 2 * MLA_HEADS + NA_HEADS

F32 = jnp.float32
BF16 = jnp.bfloat16


def _params(sem, vmem=VMEM_LIMIT):
    return pltpu.CompilerParams(dimension_semantics=sem, vmem_limit_bytes=vmem)


def _resident(shape):
    zeros = (0,) * len(shape)
    return pl.BlockSpec(shape, lambda *_: zeros, pipeline_mode=pl.Buffered(1))


def _rms(x, n):
    ss = jnp.sum(x * x, axis=-1, keepdims=True)
    return x * lax.rsqrt(ss * (1.0 / n) + NORM_EPS)


def _dot(a, b):
    return jnp.dot(a, b, preferred_element_type=F32)


def _dot_nt(a, b):
    return lax.dot_general(a, b, (((1,), (1,)), ((), ())), preferred_element_type=F32)


def _silu(x):
    return x * jax.nn.sigmoid(x)


def _ada_kernel(c_ref, w_ref, b_ref, o_ref):
    s = _silu(c_ref[...])
    o_ref[0] = jnp.dot(s, w_ref[0], precision=lax.Precision.HIGHEST,
                       preferred_element_type=F32) + b_ref[0]


def _ada_call(c_rows, w_ada, b_ada):
    depth, d, n = w_ada.shape
    rows = c_rows.shape[0]
    tn = 1536 if n % 1536 == 0 else n
    return pl.pallas_call(
        _ada_kernel,
        out_shape=jax.ShapeDtypeStruct((depth, rows, n), F32),
        grid=(depth, n // tn),
        in_specs=[
            pl.BlockSpec((rows, d), lambda i, j: (0, 0)),
            pl.BlockSpec((1, d, tn), lambda i, j: (i, 0, j)),
            pl.BlockSpec((1, 1, tn), lambda i, j: (i, 0, j)),
        ],
        out_specs=pl.BlockSpec((1, rows, tn), lambda i, j: (i, 0, j)),
        compiler_params=_params(("arbitrary", "arbitrary")),
        name="ada_mod",
    )(c_rows, w_ada, b_ada.reshape(depth, 1, n))


def _rope(xh, cos, sa, sb, shift):
    return xh * cos + pltpu.roll(xh, LANE - shift, 1) * sa + pltpu.roll(xh, shift, 1) * sb


_W_NA = 3 * NA_HEADS * HEAD_DIM
_W_QG = GQA_Q_HEADS * HEAD_DIM
_W_KVG = 2 * GQA_KV_HEADS * LANE
_W_LORA = MLA_Q_RANK + MLA_KV_RANK + LANE
_C_NA = 0
_C_QG = _C_NA + _W_NA
_C_KVG = _C_QG + _W_QG
_C_LORA = _C_KVG + _W_KVG
_C_GATE = _C_LORA + _W_LORA


def _mixin_kernel(x_ref, mod0_ref, mod1_ref, gmix_ref, w_ref, qn_ref, kn_ref, qln_ref, kvln_ref,
                  wuq_ref, wukv_ref, rg0_ref, rg1_ref, rm0_ref, rm1_ref, *out_refs):
    shared = (gmix_ref, w_ref, qn_ref, kn_ref, qln_ref, kvln_ref, wuq_ref, wukv_ref)
    half = x_ref.shape[0] // 2
    gates0 = _mixin_tile(slice(0, half), x_ref, mod0_ref, *shared, rg0_ref, rm0_ref, *out_refs)
    gates1 = _mixin_tile(slice(half, 2 * half), x_ref, mod1_ref, *shared, rg1_ref, rm1_ref, *out_refs)
    gates0()
    gates1()


def _mixin_tile(rows, x_ref, mod_ref, gmix_ref, w_ref, qn_ref, kn_ref, qln_ref, kvln_ref,
                wuq_ref, wukv_ref, rg_ref, rm_ref,
                qna_ref, kna_ref, vna_ref, qg_ref, kg_ref, vg_ref, qm_ref, km_ref, vm_ref,
                gate_ref, nrm_ref):
    x = x_ref[rows, :]
    d = x.shape[-1]
    tm = x.shape[0]
    mod = mod_ref[0]
    h = _rms(x, d) * gmix_ref[...]
    h = h * (1.0 + mod[1:2]) + mod[0:1]
    hb = h.astype(BF16)

    def proj(lo, width):
        return _dot(hb, w_ref[:, lo:lo + width])

    def block(a, j):
        return a[:, j * LANE:(j + 1) * LANE]

    lane = lax.broadcasted_iota(jnp.int32, (tm, LANE), 1)
    low = lane < HEAD_DIM
    ones_pad = jnp.where(low, 0.0, 1.0)
    norms = []

    def unit_at(pos):
        return jnp.where(lane == pos, 1.0, 0.0)

    cos_g, sa_g, sb_g = rg_ref[0], rg_ref[1], rg_ref[2]
    cos_m, sa_m, sb_m = rm_ref[0], rm_ref[1], rm_ref[2]
    g_half = HEAD_DIM // 2
    m_half = MLA_ROPE_DIM // 2

    p_lora = proj(_C_LORA, _W_LORA)
    cq = (_rms(p_lora[:, 0:MLA_Q_RANK], MLA_Q_RANK) * qln_ref[...]).astype(BF16)
    q_up = _dot(cq, wuq_ref[...])
    ckv = p_lora[:, MLA_Q_RANK:MLA_Q_RANK + MLA_KV_RANK]
    ckv = (_rms(ckv, MLA_KV_RANK) * kvln_ref[...]).astype(BF16)
    kv_up = _dot(ckv, wukv_ref[...])
    k_rope = _rope(p_lora[:, MLA_Q_RANK + MLA_KV_RANK:], cos_m, sa_m, sb_m, m_half)
    for hh in range(MLA_HEADS):
        sl = slice(hh * LANE, (hh + 1) * LANE)
        qh = _rope(block(q_up, hh), cos_m, sa_m, sb_m, m_half) * (MLA_SCALE * LOG2_E)
        qm_ref[rows, sl] = qh.astype(BF16)
        kh = block(kv_up, hh) + k_rope
        norms.append((_N_QM + hh, jnp.sum(qh * qh, axis=-1, keepdims=True)))
        norms.append((_N_KM + hh, jnp.sum(kh * kh, axis=-1, keepdims=True)))
        km_ref[rows, sl] = (kh + unit_at(MLA_NOPE_DIM + MLA_ROPE_DIM)).astype(BF16)
        vm_ref[rows, sl] = (block(kv_up, MLA_HEADS + hh) + ones_pad).astype(BF16)

    p_qg = proj(_C_QG, _W_QG)
    for j in range(GQA_Q_HEADS // 2):
        xp = block(p_qg, j)
        sq = xp * xp
        ss_lo = jnp.sum(jnp.where(low, sq, 0.0), axis=-1, keepdims=True)
        ss_hi = jnp.sum(jnp.where(low, 0.0, sq), axis=-1, keepdims=True)
        inv = jnp.where(low, lax.rsqrt(ss_lo * (1.0 / HEAD_DIM) + NORM_EPS),
                        lax.rsqrt(ss_hi * (1.0 / HEAD_DIM) + NORM_EPS))
        xp = _rope(xp * inv * qn_ref[...], cos_g, sa_g, sb_g, g_half) * (GQA_SCALE * LOG2_E)
        qg_ref[rows, j * LANE:(j + 1) * LANE] = xp.astype(BF16)
    p_kvg = proj(_C_KVG, _W_KVG)
    for hh in range(GQA_KV_HEADS):
        xh = _rope(_rms(block(p_kvg, hh), HEAD_DIM) * kn_ref[...], cos_g, sa_g, sb_g, g_half)
        kg_ref[rows, hh * LANE:(hh + 1) * LANE] = (xh + unit_at(HEAD_DIM)).astype(BF16)
        vh = block(p_kvg, GQA_KV_HEADS + hh) + ones_pad
        vg_ref[rows, hh * LANE:(hh + 1) * LANE] = vh.astype(BF16)

    nw = NA_HEADS * HEAD_DIM
    p_na = proj(_C_NA, _W_NA)
    q_na = p_na[:, 0:nw] * (NA_SCALE * LOG2_E)
    qna_ref[rows, :] = q_na.astype(BF16)
    kna_ref[rows, :] = p_na[:, nw:2 * nw].astype(BF16)
    vna_ref[rows, :] = p_na[:, 2 * nw:3 * nw].astype(BF16)
    for j in range(NA_HEADS // 2):
        for pos, blk in ((_N_QNA, block(q_na, j)), (_N_KNA, block(p_na, NA_HEADS // 2 + j))):
            sq = blk * blk
            norms.append((pos + 2 * j, jnp.sum(jnp.where(low, sq, 0.0), axis=-1, keepdims=True)))
            norms.append((pos + 2 * j + 1, jnp.sum(jnp.where(low, 0.0, sq), axis=-1, keepdims=True)))
    packed = jnp.zeros((tm, LANE), F32)
    for pos, val in norms:
        packed = jnp.where(lane == pos, val, packed)
    nrm_ref[rows, :] = packed

    def gates():
        for j in range(3):
            gate_ref[rows, j * d:(j + 1) * d] = jax.nn.sigmoid(proj(_C_GATE + j * d, d)).astype(BF16)

    return gates


def _mixin_call(xc, mod, gmix, w_cat, qn, kn, qln, kvln, wuq, wukv, rope_g, rope_m):
    b, t, d = xc.shape
    tm = TOKEN_TILE
    nt = t // tm
    n_tiles = b * nt
    assert n_tiles % 2 == 0
    tok = lambda w: pl.BlockSpec((2 * tm, w), lambda i: (i, 0))
    widths = [NA_HEADS * HEAD_DIM] * 3 + [GQA_Q_HEADS * HEAD_DIM, GQA_KV_HEADS * LANE,
                                          GQA_KV_HEADS * LANE, MLA_HEADS * LANE,
                                          MLA_HEADS * LANE, MLA_HEADS * LANE, 3 * d]

    def mod_spec(h):
        return pl.BlockSpec((1, 6, d), lambda i: (
            2 * ((2 * i + h) // nt) + ((2 * i + h) % nt == nt - 1).astype(jnp.int32), 0, 0))

    def rope_spec(h):
        return pl.BlockSpec((3, tm, LANE), lambda i: (0, (2 * i + h) % nt, 0))

    outs = pl.pallas_call(
        _mixin_kernel,
        out_shape=[jax.ShapeDtypeStruct((b * t, w), BF16) for w in widths]
        + [jax.ShapeDtypeStruct((b * t, LANE), F32)],
        grid=(n_tiles // 2,),
        in_specs=[
            tok(d), mod_spec(0), mod_spec(1),
            _resident((1, d)),
            _resident(w_cat.shape),
            _resident((1, LANE)), _resident((1, LANE)),
            _resident((1, MLA_Q_RANK)), _resident((1, MLA_KV_RANK)),
            _resident(wuq.shape), _resident(wukv.shape),
            rope_spec(0), rope_spec(1), rope_spec(0), rope_spec(1),
        ],
        out_specs=[tok(w) for w in widths] + [tok(LANE)],
        compiler_params=_params(("arbitrary",)),
        name="mixer_in",
    )(xc.reshape(b * t, d), mod, mod, gmix, w_cat, qn, kn, qln, kvln, wuq, wukv, rope_g, rope_g, rope_m, rope_m)
    return [o.reshape(b, t, o.shape[-1]) for o in outs]


def _na_kernel(q_ref, k_ref, v_ref, bias_ref, u_ref, o_ref, *, n_groups, n_rows, s_len, c_len, bounded):
    g = pl.program_id(1)
    qn = NA_Q_ROWS * GRID_W
    kn = NA_K_ROWS * GRID_W
    lane = lax.broadcasted_iota(jnp.int32, (qn, LANE), 1)
    low = lane < HEAD_DIM

    def attend_pairs(k_nb, v_nb):
        k_c = k_ref[0, s_len:s_len + c_len, :]
        v_c = v_ref[0, s_len:s_len + c_len, :]
        for pair in range(NA_HEADS // 2):
            sl = slice(pair * LANE, (pair + 1) * LANE)
            qp = q_ref[0, :, sl]
            outs = []
            for sub in range(2):
                head = 2 * pair + sub
                qm = jnp.where(low if sub == 0 else jnp.logical_not(low), qp, jnp.zeros_like(qp))
                s_c = _dot_nt(qm, k_c[:, sl])
                if k_nb is not None:
                    s_nb = _dot_nt(qm, k_nb[:, sl]) + bias_ref[0, head]
                if bounded:
                    p_c = jnp.exp2(s_c - u_ref[head:head + 1, :])
                else:
                    m = jnp.max(s_c, axis=-1, keepdims=True)
                    if k_nb is not None:
                        m = jnp.maximum(m, jnp.max(s_nb, axis=-1, keepdims=True))
                    p_c = jnp.exp2(s_c - m)
                den = jnp.sum(p_c, axis=-1, keepdims=True)
                o = _dot(p_c.astype(BF16), v_c[:, sl])
                if k_nb is not None:
                    p_nb = jnp.exp2(s_nb if bounded else s_nb - m)
                    den = den + jnp.sum(p_nb, axis=-1, keepdims=True)
                    o = o + _dot(p_nb.astype(BF16), v_nb[:, sl])
                outs.append(o / den)
            o_ref[0, :, sl] = jnp.where(low, outs[0], outs[1]).astype(BF16)

    @pl.when(g < n_groups)
    def _():
        first_row = jnp.clip(NA_Q_ROWS * g - WIN_H // 2, 0, n_rows - NA_K_ROWS)
        start = pl.multiple_of(first_row * GRID_W, GRID_W)
        attend_pairs(k_ref[0, pl.ds(start, kn), :], v_ref[0, pl.ds(start, kn), :])

    @pl.when(g == n_groups)
    def _():
        attend_pairs(None, None)


def _na_attention(q, k, v, bias, rpb_l, q_norm2, k_norm2, s_len, c_len):
    qk = jnp.max(jnp.sqrt(q_norm2 * k_norm2), axis=0) * NORM_SLACK
    b_hi = jnp.maximum(jnp.max(rpb_l, axis=(1, 2)) * LOG2_E, 0.0)
    b_lo = jnp.minimum(jnp.min(rpb_l, axis=(1, 2)) * LOG2_E, 0.0)
    u = qk + b_hi
    safe = jnp.max(2.0 * qk + b_hi - b_lo) < SAFE_LOG2_RANGE
    u_rows = jnp.broadcast_to(u[:, None], (NA_HEADS, c_len)).astype(F32)
    shifted = jnp.where(bias > 0.5 * NEG_BIG, bias - u[None, :, None, None], NEG_BIG)
    return lax.cond(safe,
                    lambda ops: _na_call(ops[0], ops[1], ops[2], ops[3], ops[5], s_len, c_len, True),
                    lambda ops: _na_call(ops[0], ops[1], ops[2], ops[4], ops[5], s_len, c_len, False),
                    (q, k, v, shifted, bias, u_rows))


def _na_call(q, k, v, bias, u_rows, s_len, c_len, bounded):
    b, t, w = q.shape
    qn = NA_Q_ROWS * GRID_W
    kn = NA_K_ROWS * GRID_W
    n_rows = s_len // GRID_W
    n_groups = s_len // qn
    assert c_len == qn and n_rows >= NA_K_ROWS

    def variant(g):
        return jnp.where(g == 0, 0, jnp.where(g >= n_groups - 1, 2, 1))

    return pl.pallas_call(
        functools.partial(_na_kernel, n_groups=n_groups, n_rows=n_rows, s_len=s_len, c_len=c_len,
                          bounded=bounded),
        out_shape=jax.ShapeDtypeStruct((b, t, w), BF16),
        grid=(b, n_groups + 1),
        in_specs=[
            pl.BlockSpec((1, qn, w), lambda bi, g: (bi, g, 0)),
            pl.BlockSpec((1, t, w), lambda bi, g: (bi, 0, 0)),
            pl.BlockSpec((1, t, w), lambda bi, g: (bi, 0, 0)),
            pl.BlockSpec((1, NA_HEADS, qn, kn), lambda bi, g: (variant(g), 0, 0, 0)),
            pl.BlockSpec(u_rows.shape, lambda bi, g: (0, 0)),
        ],
        out_specs=pl.BlockSpec((1, qn, w), lambda bi, g: (bi, g, 0)),
        compiler_params=_params(("arbitrary", "arbitrary")),
        name="na_attn" if bounded else "na_attn_max",
    )(q, k, v, bias, u_rows)


def _na_bias_tables(rpb, n_rows):
    qi = np.arange(NA_Q_ROWS)[:, None]
    kj = np.arange(NA_K_ROWS)[None, :]
    qc = np.arange(GRID_W)[:, None]
    kc = np.arange(GRID_W)[None, :]
    col0 = np.clip(qc - WIN_W // 2, 0, GRID_W - WIN_W)
    col_ok = (kc >= col0) & (kc < col0 + WIN_W)
    dc = np.clip(kc - qc, -(WIN_W - 1), WIN_W - 1) + WIN_W - 1
    col_sel = np.eye(2 * WIN_W - 1, dtype=np.float32)[dc]
    row_sel, valid = [], []
    for r0, u0 in ((0, 0), (WIN_H // 2, 0), (n_rows - NA_Q_ROWS, n_rows - NA_K_ROWS)):
        qr = r0 + qi
        kr = u0 + kj
        row0 = np.clip(qr - WIN_H // 2, 0, n_rows - WIN_H)
        row_ok = (kr >= row0) & (kr < row0 + WIN_H)
        dr = np.clip(kr - qr, -(WIN_H - 1), WIN_H - 1) + WIN_H - 1
        row_sel.append(np.eye(2 * WIN_H - 1, dtype=np.float32)[dr])
        valid.append(row_ok[:, None, :, None] & col_ok[None, :, None, :])
    row_sel = jnp.asarray(np.stack(row_sel))
    valid = np.stack(valid).reshape(3, 1, NA_Q_ROWS * GRID_W, NA_K_ROWS * GRID_W)
    picked = jnp.einsum('vija,lhab,qkb->lvhiqjk', row_sel, rpb.astype(F32) * LOG2_E, jnp.asarray(col_sel),
                        precision=lax.Precision.HIGHEST)
    picked = picked.reshape(rpb.shape[0], 3, NA_HEADS, NA_Q_ROWS * GRID_W, NA_K_ROWS * GRID_W)
    return jnp.where(valid[None], picked, NEG_BIG)


def _row_max(s):
    parts = [s[:, j * LANE:(j + 1) * LANE] for j in range(s.shape[1] // LANE)]
    while len(parts) > 1:
        parts = [jnp.maximum(parts[j], parts[j + 1]) for j in range(0, len(parts) - 1, 2)] + parts[len(parts) & ~1:]
    return jnp.max(parts[0], axis=-1, keepdims=True)


def _flash_kernel(q_ref, k_ref, v_ref, u_ref, o_ref, *, n_kv, group, tq, q_width, s_len, c_len,
                  with_latent, online):
    low = lax.broadcasted_iota(jnp.int32, (tq, LANE), 1) < HEAD_DIM
    heads = range(n_kv)
    lanes = [slice(h * LANE, (h + 1) * LANE) for h in heads]

    def load_q(head):
        if q_width == LANE:
            qh = q_ref[0, :, head * LANE:(head + 1) * LANE]
        else:
            pair = q_ref[0, :, (head // 2) * LANE:(head // 2 + 1) * LANE]
            if head % 2:
                pair = pltpu.roll(pair.astype(F32), HEAD_DIM, 1).astype(BF16)
            qh = jnp.where(low, pair, jnp.zeros_like(pair))
        return qh if online else qh + u_ref[0, head:head + 1, :]

    qs = []
    for h in heads:
        parts = [load_q(h * group + g) for g in range(group)]
        qs.append(parts[0] if group == 1 else jnp.concatenate(parts, axis=0))

    if online:
        carry = []
        for h in heads:
            s = _dot_nt(qs[h], k_ref[0, s_len:s_len + c_len, lanes[h]])
            m = _row_max(s)
            carry += [m, _dot(jnp.exp2(s - m).astype(BF16), v_ref[0, s_len:s_len + c_len, lanes[h]])]
        if with_latent:
            def body(i, carry):
                st = pl.multiple_of(i * KV_CHUNK, KV_CHUNK)
                new = []
                for h in heads:
                    m_old, acc_old = carry[2 * h], carry[2 * h + 1]
                    sc = _dot_nt(qs[h], k_ref[0, pl.ds(st, KV_CHUNK), lanes[h]])
                    m_new = jnp.maximum(m_old, _row_max(sc))
                    p = jnp.exp2(sc - m_new).astype(BF16)
                    pv = _dot(p, v_ref[0, pl.ds(st, KV_CHUNK), lanes[h]])
                    new += [m_new, jnp.exp2(m_old - m_new) * acc_old + pv]
                return tuple(new)
            carry = lax.fori_loop(0, s_len // KV_CHUNK, body, tuple(carry), unroll=2)
        accs = [carry[2 * h + 1] for h in heads]
    else:
        chunks = [(s_len, c_len)]
        if with_latent:
            chunks += [(i * KV_CHUNK, KV_CHUNK) for i in range(s_len // KV_CHUNK)]
        accs = [None] * n_kv
        for st, size in chunks:
            for h in heads:
                p = jnp.exp2(_dot_nt(qs[h], k_ref[0, st:st + size, lanes[h]])).astype(BF16)
                pv = _dot(p, v_ref[0, st:st + size, lanes[h]])
                accs[h] = pv if accs[h] is None else accs[h] + pv
    outs = []
    for h in heads:
        acc = accs[h]
        o = acc / pltpu.roll(acc, HEAD_DIM, 1)
        for g in range(group):
            outs.append(o[g * tq:(g + 1) * tq])
    for pair in range(len(outs) // 2):
        packed = jnp.where(low, outs[2 * pair], pltpu.roll(outs[2 * pair + 1], HEAD_DIM, 1))
        o_ref[0, :, pair * LANE:(pair + 1) * LANE] = packed.astype(BF16)


def _flash_call(q, k, v, shift, *, n_kv, group, tq, q_width, s_len, c_len, online, name):
    b, t, _ = q.shape
    n_heads = n_kv * group
    tq_ctx = min(tq, c_len)
    assert s_len % tq == 0 and c_len % tq_ctx == 0 and s_len % tq_ctx == 0 and s_len % KV_CHUNK == 0
    kv_spec = pl.BlockSpec((1, t, n_kv * LANE), lambda bi, ti: (bi, 0, 0))
    u_spec = pl.BlockSpec((1, n_heads, LANE), lambda bi, ti: (bi, 0, 0))
    common = dict(n_kv=n_kv, group=group, q_width=q_width, s_len=s_len, c_len=c_len, online=online)
    width = n_heads * HEAD_DIM
    tag = "_online" if online else ""
    y_lat = pl.pallas_call(
        functools.partial(_flash_kernel, tq=tq, with_latent=True, **common),
        out_shape=jax.ShapeDtypeStruct((b, s_len, width), BF16),
        grid=(b, s_len // tq),
        in_specs=[pl.BlockSpec((1, tq, n_heads * q_width), lambda bi, ti: (bi, ti, 0)), kv_spec, kv_spec, u_spec],
        out_specs=pl.BlockSpec((1, tq, width), lambda bi, ti: (bi, ti, 0)),
        compiler_params=_params(("arbitrary", "arbitrary")),
        name=name + tag,
    )(q, k, v, shift)
    first = s_len // tq_ctx
    y_ctx = pl.pallas_call(
        functools.partial(_flash_kernel, tq=tq_ctx, with_latent=False, **common),
        out_shape=jax.ShapeDtypeStruct((b, c_len, width), BF16),
        grid=(b, c_len // tq_ctx),
        in_specs=[pl.BlockSpec((1, tq_ctx, n_heads * q_width), lambda bi, ti: (bi, first + ti, 0)), kv_spec, kv_spec,
                  u_spec],
        out_specs=pl.BlockSpec((1, tq_ctx, width), lambda bi, ti: (bi, ti, 0)),
        compiler_params=_params(("arbitrary", "arbitrary")),
        name=name + "_ctx" + tag,
    )(q, k, v, shift)
    return y_lat, y_ctx


def _bounded_attention(q, k, v, bound, unit_lane, **kw):
    bound = bound * NORM_SLACK
    lane = jnp.arange(LANE) == unit_lane
    shift = jnp.where(lane[None, None, :], -bound[:, :, None], 0.0).astype(BF16)
    safe = jnp.max(bound) * 2.0 < SAFE_LOG2_RANGE
    return lax.cond(safe,
                    lambda ops: _flash_call(*ops, online=False, **kw),
                    lambda ops: _flash_call(*ops, online=True, **kw),
                    (q, k, v, shift))


def _mod_spec(d, nt):
    return pl.BlockSpec((1, 6, d), lambda bi, ti: (2 * bi + (ti == nt - 1).astype(jnp.int32), 0, 0))


def _merged_residual(x_ref, mod_ref, yna_ref, yg_ref, ygc_ref, ym_ref, ymc_ref, gate_ref, wna_ref, wg_ref,
                     wm_ref, wout_ref, n_lat_tiles):
    d = x_ref.shape[-1]
    is_ctx = pl.program_id(1) >= n_lat_tiles
    y_g = jnp.where(is_ctx, ygc_ref[0], yg_ref[0])
    y_m = jnp.where(is_ctx, ymc_ref[0], ym_ref[0])
    m = gate_ref[0, :, 0:d].astype(F32) * _dot(yna_ref[0], wna_ref[...])
    m = m + gate_ref[0, :, d:2 * d].astype(F32) * _dot(y_g, wg_ref[...])
    m = m + gate_ref[0, :, 2 * d:3 * d].astype(F32) * _dot(y_m, wm_ref[...])
    r = _dot(m.astype(BF16), wout_ref[...])
    return x_ref[0] + mod_ref[0][2:3] * r


def _norm2(x, mod, gain):
    h = _rms(x, x.shape[-1]) * gain
    return h * (1.0 + mod[4:5]) + mod[3:4]


def _merge_ffn_kernel(*refs, n_lat_tiles):
    mix_refs, (gn_ref, wg_ref, wu_ref, wd_ref, o_ref) = refs[:12], refs[12:]
    x = _merged_residual(*mix_refs, n_lat_tiles)
    mod = mix_refs[1][0]
    hb = _norm2(x, mod, gn_ref[...]).astype(BF16)
    act = (_silu(_dot(hb, wg_ref[...])) * _dot(hb, wu_ref[...])).astype(BF16)
    o_ref[0] = x + mod[5:6] * _dot(act, wd_ref[...])


def _merge_router_kernel(*refs, n_lat_tiles, n_experts):
    mix_refs, (gn_ref, wr_ref, o_ref, idx_ref, wt_ref, cnt_ref, carry_ref) = refs[:12], refs[12:]

    @pl.when((pl.program_id(0) == 0) & (pl.program_id(1) == 0))
    def _():
        carry_ref[...] = jnp.zeros_like(carry_ref)

    x = _merged_residual(*mix_refs, n_lat_tiles)
    o_ref[0] = x
    tm = x.shape[0]
    h = _norm2(x, mix_refs[1][0], gn_ref[...])
    h_hi = h.astype(BF16)
    h_lo = (h - h_hi.astype(F32)).astype(BF16)
    logits = _dot(h_hi, wr_ref[0]) + (_dot(h_lo, wr_ref[0]) + _dot(h_hi, wr_ref[1]))
    lane = lax.broadcasted_iota(jnp.int32, (tm, LANE), 1).astype(F32)
    logits = jnp.where(lane < n_experts, logits, -jnp.inf)
    m1 = jnp.max(logits, axis=-1, keepdims=True)
    i1 = jnp.min(jnp.where(logits == m1, lane, float(LANE)), axis=-1, keepdims=True)
    rest = jnp.where(lane == i1, -jnp.inf, logits)
    m2 = jnp.max(rest, axis=-1, keepdims=True)
    i2 = jnp.min(jnp.where(rest == m2, lane, float(LANE)), axis=-1, keepdims=True)
    e2 = jnp.exp(m2 - m1)
    w1 = 1.0 / (1.0 + e2)
    w2 = e2 / (1.0 + e2)
    hot1 = lane == i1
    hot2 = lane == i2
    hot = jnp.where(hot1 | hot2, 1.0, 0.0)
    rows = lax.broadcasted_iota(jnp.int32, (tm, tm), 0)
    cols = lax.broadcasted_iota(jnp.int32, (tm, tm), 1)
    below = jnp.where(rows > cols, 1.0, 0.0).astype(BF16)
    before = _dot(below, hot.astype(BF16)) + carry_ref[0:1, :]
    r1 = jnp.sum(jnp.where(hot1, before, 0.0), axis=-1, keepdims=True)
    r2 = jnp.sum(jnp.where(hot2, before, 0.0), axis=-1, keepdims=True)
    total = carry_ref[0:1, :] + jnp.sum(hot, axis=0, keepdims=True)
    carry_ref[...] = jnp.broadcast_to(total, carry_ref.shape)
    cnt_ref[...] = jnp.broadcast_to(total, cnt_ref.shape)
    packed = jnp.where(lane == 0, i1, jnp.where(lane == 1, i2, jnp.where(lane == 2, r1, jnp.where(lane == 3, r2, 0.0))))
    idx_ref[0] = packed.astype(jnp.int32)
    wt_ref[0] = jnp.where(lane == 0, w1, jnp.where(lane == 1, w2, 0.0))


def _merge_call(xc, mod, y_na, y_g, y_m, gate, w_na, w_g, w_m, w_out, gn, *, ffn=None, router=None):
    b, t, d = xc.shape
    tm = TOKEN_TILE
    nt = t // tm
    n_lat = y_g[0].shape[1] // tm
    tok = lambda w: pl.BlockSpec((1, tm, w), lambda bi, ti: (bi, ti, 0))
    lat = lambda w: pl.BlockSpec((1, tm, w), lambda bi, ti: (bi, jnp.minimum(ti, n_lat - 1), 0))
    ctx = lambda w: pl.BlockSpec((1, tm, w), lambda bi, ti: (bi, jnp.maximum(ti - n_lat, 0), 0))
    wg_, wm_ = y_g[0].shape[-1], y_m[0].shape[-1]
    mix_specs = [tok(d), _mod_spec(d, nt), tok(y_na.shape[-1]), lat(wg_), ctx(wg_), lat(wm_), ctx(wm_),
                 tok(3 * d), _resident(w_na.shape), _resident(w_g.shape), _resident(w_m.shape),
                 _resident(w_out.shape), _resident((1, d))]
    mix_args = (xc, mod, y_na, y_g[0], y_g[1], y_m[0], y_m[1], gate, w_na, w_g, w_m, w_out, gn)
    x_shape = jax.ShapeDtypeStruct(xc.shape, F32)
    if ffn is not None:
        return pl.pallas_call(
            functools.partial(_merge_ffn_kernel, n_lat_tiles=n_lat),
            out_shape=x_shape,
            grid=(b, nt),
            in_specs=mix_specs + [_resident(w.shape) for w in ffn],
            out_specs=tok(d),
            input_output_aliases={0: 0},
            compiler_params=_params(("arbitrary", "arbitrary")),
            name="merge_ffn",
        )(*mix_args, *ffn)
    w_router_pad, n_experts = router
    return pl.pallas_call(
        functools.partial(_merge_router_kernel, n_lat_tiles=n_lat, n_experts=n_experts),
        out_shape=[x_shape, jax.ShapeDtypeStruct((b, t, LANE), jnp.int32),
                   jax.ShapeDtypeStruct((b, t, LANE), F32), jax.ShapeDtypeStruct((8, LANE), F32)],
        grid=(b, nt),
        in_specs=mix_specs + [_resident(w_router_pad.shape)],
        out_specs=[tok(d), tok(LANE), tok(LANE), pl.BlockSpec((8, LANE), lambda bi, ti: (0, 0))],
        scratch_shapes=[pltpu.VMEM((8, LANE), F32)],
        input_output_aliases={0: 0},
        compiler_params=_params(("arbitrary", "arbitrary")),
        name="merge_router",
    )(*mix_args, w_router_pad)


def _row_copy(src, src_row, dst, dst_row, sem):
    return pltpu.make_async_copy(src.at[pl.ds(src_row, 1)], dst.at[pl.ds(dst_row, 1)], sem)


def _dispatch_kernel(pad_from_ref, pad_cnt_ref, dest_ref, x_ref, mod_ref, gn_ref, xs_ref, h_buf, zrow, sem,
                     zsem, *, n_tiles, n_exp):
    i = pl.program_id(0)
    slot = i % 2
    tm = x_ref.shape[1]

    @pl.when(i == 0)
    def _():
        zrow[...] = jnp.zeros_like(zrow)
        for e in range(n_exp):
            def zero_issue(r, carry, e=e):
                _row_copy(zrow, 0, xs_ref, pad_from_ref[e] + r, zsem).start()
                return carry
            lax.fori_loop(0, pad_cnt_ref[e], zero_issue, 0)
        for e in range(n_exp):
            def zero_drain(r, carry):
                _row_copy(zrow, 0, xs_ref, 0, zsem).wait()
                return carry
            lax.fori_loop(0, pad_cnt_ref[e], zero_drain, 0)

    @pl.when(i < n_tiles)
    def _():
        h_buf[slot] = _norm2(x_ref[0], mod_ref[0], gn_ref[...])

        def issue(r, carry):
            for k in range(TOP_K):
                _row_copy(h_buf.at[slot], r, xs_ref, dest_ref[0, 0, TOP_K * r + k], sem.at[slot]).start()
            return carry

        lax.fori_loop(0, tm, issue, 0, unroll=8)

    @pl.when(i > 0)
    def _():
        def drain(r, carry):
            for k in range(TOP_K):
                _row_copy(h_buf.at[1 - slot], 0, xs_ref, 0, sem.at[1 - slot]).wait()
            return carry

        lax.fori_loop(0, tm, drain, 0, unroll=8)


def _dispatch_call(pad_from, pad_cnt, dest_blocks, xc, mod, gn, n_rows):
    b, t, d = xc.shape
    tm = TOKEN_TILE
    nt = t // tm
    n_tiles = b * nt

    def tile(i):
        j = jnp.minimum(i, n_tiles - 1)
        return j // nt, j % nt

    grid_spec = pltpu.PrefetchScalarGridSpec(
        num_scalar_prefetch=2,
        grid=(n_tiles + 1,),
        in_specs=[
            pl.BlockSpec((1, 1, dest_blocks.shape[-1]), lambda i, pf, pc: (jnp.minimum(i, n_tiles - 1), 0, 0),
                         memory_space=pltpu.SMEM),
            pl.BlockSpec((1, tm, d), lambda i, pf, pc: (*tile(i), 0)),
            pl.BlockSpec((1, 6, d), lambda i, pf, pc: (
                2 * tile(i)[0] + (tile(i)[1] == nt - 1).astype(jnp.int32), 0, 0)),
            pl.BlockSpec((1, d), lambda i, pf, pc: (0, 0)),
        ],
        out_specs=pl.BlockSpec(memory_space=pl.ANY),
        scratch_shapes=[pltpu.VMEM((2, tm, d), F32), pltpu.VMEM((8, d), F32),
                        pltpu.SemaphoreType.DMA((2,)), pltpu.SemaphoreType.DMA],
    )
    return pl.pallas_call(
        functools.partial(_dispatch_kernel, n_tiles=n_tiles, n_exp=pad_from.shape[0]),
        out_shape=jax.ShapeDtypeStruct((n_rows, d), F32),
        grid_spec=grid_spec,
        compiler_params=_params(("arbitrary",)),
        name="moe_dispatch",
    )(pad_from, pad_cnt, dest_blocks, xc, mod, gn)


def _moe_kernel(be_ref, nu_ref, xs_ref, wgu_ref, wd_ref, y_ref, *, f_chunk):
    f_dim = wd_ref.shape[1]

    @pl.when(pl.program_id(0) < nu_ref[0])
    def _():
        xb = xs_ref[...].astype(BF16)
        acc = None
        for c in range(0, f_dim, f_chunk):
            gate = _dot(xb, wgu_ref[0, :, c:c + f_chunk])
            up = _dot(xb, wgu_ref[0, :, f_dim + c:f_dim + c + f_chunk])
            part = _dot((_silu(gate) * up).astype(BF16), wd_ref[0, c:c + f_chunk, :])
            acc = part if acc is None else acc + part
        y_ref[...] = acc

    @pl.when(pl.program_id(0) >= nu_ref[0])
    def _():
        y_ref[...] = jnp.zeros_like(y_ref)


def _moe_call(block_expert, n_used, xs, w_gu, w_dn):
    rows, d = xs.shape
    n_exp, _, two_f = w_gu.shape
    f_dim = two_f // 2
    f_chunk = 2 * LANE if f_dim % (2 * LANE) == 0 else f_dim
    tmr = MOE_ROW_BLOCK
    grid_spec = pltpu.PrefetchScalarGridSpec(
        num_scalar_prefetch=2,
        grid=(rows // tmr,),
        in_specs=[
            pl.BlockSpec((tmr, d), lambda i, be, nu: (jnp.minimum(i, nu[0] - 1), 0)),
            pl.BlockSpec((1, d, two_f), lambda i, be, nu: (be[i], 0, 0)),
            pl.BlockSpec((1, f_dim, d), lambda i, be, nu: (be[i], 0, 0)),
        ],
        out_specs=pl.BlockSpec((tmr, d), lambda i, be, nu: (i, 0)),
    )
    return pl.pallas_call(
        functools.partial(_moe_kernel, f_chunk=f_chunk),
        out_shape=jax.ShapeDtypeStruct((rows, d), F32),
        grid_spec=grid_spec,
        compiler_params=_params(("arbitrary",)),
        name="moe_experts",
    )(block_expert, n_used, xs, w_gu, w_dn)


def _combine_kernel(dest_ref, x_ref, mod_ref, wt_ref, gain_ref, y_ref, o_ref, buf1, buf2, sem, *, n_tiles,
                    final):
    i = pl.program_id(0)
    slot = i % 2
    tm = x_ref.shape[1]

    @pl.when(i < n_tiles)
    def _():
        def issue(r, carry):
            _row_copy(y_ref, dest_ref[0, 0, TOP_K * r], buf1.at[slot], r, sem.at[slot]).start()
            _row_copy(y_ref, dest_ref[0, 0, TOP_K * r + 1], buf2.at[slot], r, sem.at[slot]).start()
            return carry

        lax.fori_loop(0, tm, issue, 0, unroll=8)

    @pl.when(i > 0)
    def _():
        def drain(r, carry):
            _row_copy(y_ref, 0, buf1.at[1 - slot], 0, sem.at[1 - slot]).wait()
            _row_copy(y_ref, 0, buf2.at[1 - slot], 0, sem.at[1 - slot]).wait()
            return carry

        lax.fori_loop(0, tm, drain, 0, unroll=8)
        wt = wt_ref[0]
        mix = wt[:, 0:1] * buf1[1 - slot] + wt[:, 1:2] * buf2[1 - slot]
        x = x_ref[0] + mod_ref[0][5:6] * mix
        o_ref[0] = _rms(x, x.shape[-1]) * gain_ref[...] if final else x


def _combine_call(dest_blocks, xc, mod, wts, y_rows, final_gain=None, s_len=None):
    b, t, d = xc.shape
    tm = TOKEN_TILE
    nt = t // tm
    final = final_gain is not None
    used = s_len // tm if final else nt
    n_tiles = b * used

    def issued(i):
        j = jnp.minimum(i, n_tiles - 1)
        return j // used, j % used

    def finished(i):
        j = jnp.maximum(i - 1, 0)
        return j // used, j % used

    tok = lambda w: pl.BlockSpec((1, tm, w), lambda i: (*finished(i), 0))
    gain = final_gain if final else jnp.ones((1, d), F32)
    return pl.pallas_call(
        functools.partial(_combine_kernel, n_tiles=n_tiles, final=final),
        out_shape=jax.ShapeDtypeStruct((b, used * tm, d), F32),
        grid=(n_tiles + 1,),
        in_specs=[
            pl.BlockSpec((1, 1, dest_blocks.shape[-1]), lambda i: (issued(i)[0] * nt + issued(i)[1], 0, 0),
                         memory_space=pltpu.SMEM),
            tok(d),
            pl.BlockSpec((1, 6, d), lambda i: (
                2 * finished(i)[0] + (finished(i)[1] == nt - 1).astype(jnp.int32), 0, 0)),
            tok(LANE), pl.BlockSpec((1, d), lambda i: (0, 0)),
            pl.BlockSpec(memory_space=pl.ANY),
        ],
        out_specs=tok(d),
        scratch_shapes=[pltpu.VMEM((2, tm, d), F32), pltpu.VMEM((2, tm, d), F32),
                        pltpu.SemaphoreType.DMA((2,))],
        input_output_aliases={} if final else {1: 0},
        compiler_params=_params(("arbitrary",)),
        name="moe_combine_final" if final else "moe_combine",
    )(dest_blocks, xc, mod, wts, gain, y_rows)


def _final_kernel(x_ref, g_ref, o_ref):
    x = x_ref[0]
    o_ref[0] = _rms(x, x.shape[-1]) * g_ref[...]


def _final_call(xc, gain, s_len):
    b, t, d = xc.shape
    tm = TOKEN_TILE
    tok = pl.BlockSpec((1, tm, d), lambda bi, ti: (bi, ti, 0))
    return pl.pallas_call(
        _final_kernel,
        out_shape=jax.ShapeDtypeStruct((b, s_len, d), F32),
        grid=(b, s_len // tm),
        in_specs=[tok, _resident((1, d))],
        out_specs=tok,
        compiler_params=_params(("arbitrary", "arbitrary")),
        name="final_norm",
    )(xc, gain)


def _pad_heads(w, n_heads, width):
    k = w.shape[0]
    w = w.reshape(k, n_heads, width)
    return jnp.pad(w, ((0, 0), (0, 0), (0, LANE - width))).reshape(k, n_heads * LANE)


def _layer_weights(w_in, w_uq, w_ukv):
    d = w_in.shape[0]
    nw = NA_HEADS * HEAD_DIM
    gk = GQA_KV_HEADS * HEAD_DIM
    o = 0
    k_na = w_in[:, o:o + nw]; o += nw
    v_na = w_in[:, o:o + nw]; o += nw
    k_g = w_in[:, o:o + gk]; o += gk
    v_g = w_in[:, o:o + gk]; o += gk
    c_kv = w_in[:, o:o + MLA_KV_RANK]; o += MLA_KV_RANK
    k_r = w_in[:, o:o + MLA_ROPE_DIM]; o += MLA_ROPE_DIM
    q_na = w_in[:, o:o + nw]; o += nw
    q_g = w_in[:, o:o + GQA_Q_HEADS * HEAD_DIM]; o += GQA_Q_HEADS * HEAD_DIM
    c_q = w_in[:, o:o + MLA_Q_RANK]; o += MLA_Q_RANK
    gate = w_in[:, o:]
    k_r_pad = jnp.pad(k_r, ((0, 0), (MLA_NOPE_DIM, LANE - MLA_NOPE_DIM - MLA_ROPE_DIM)))
    w_cat = jnp.concatenate([
        q_na, k_na, v_na, q_g, _pad_heads(k_g, GQA_KV_HEADS, HEAD_DIM), _pad_heads(v_g, GQA_KV_HEADS, HEAD_DIM), c_q, c_kv, k_r_pad, gate],
        axis=1).astype(BF16)
    assert w_cat.shape[1] == _C_GATE + 3 * d
    wuq = _pad_heads(w_uq, MLA_HEADS, MLA_NOPE_DIM + MLA_ROPE_DIM).astype(BF16)
    kv = w_ukv.reshape(MLA_KV_RANK, MLA_HEADS, MLA_NOPE_DIM + MLA_V_DIM)
    wuk = _pad_heads(kv[:, :, :MLA_NOPE_DIM].reshape(MLA_KV_RANK, -1), MLA_HEADS, MLA_NOPE_DIM)
    wuv = _pad_heads(kv[:, :, MLA_NOPE_DIM:].reshape(MLA_KV_RANK, -1), MLA_HEADS, MLA_V_DIM)
    return w_cat, wuq, jnp.concatenate([wuk, wuv], axis=1).astype(BF16)


def _rope_tables(s_len, c_len, rot_dim, lane_off):
    half = rot_dim // 2
    n_freq = half // 2
    pos = jnp.arange(s_len)
    rows = (pos // GRID_W).astype(F32)
    cols = (pos % GRID_W).astype(F32)
    inv = jnp.power(ROPE_THETA, -jnp.arange(n_freq, dtype=F32) / n_freq)
    ang = jnp.concatenate([rows[:, None] * inv, cols[:, None] * inv], axis=-1)
    cos, sin = jnp.cos(ang), jnp.sin(ang)
    zeros = jnp.zeros((s_len, half), F32)
    right = LANE - lane_off - rot_dim
    pad = lambda a, b: jnp.pad(jnp.concatenate([a, b], axis=-1), ((0, 0), (lane_off, right)))
    cos_t = pad(cos, cos) + jnp.pad(jnp.ones((s_len, lane_off), F32), ((0, 0), (0, LANE - lane_off)))
    sa_t = pad(-sin, zeros)
    sb_t = pad(zeros, sin)
    ident = jnp.pad(jnp.ones((c_len, lane_off + rot_dim), F32), ((0, 0), (0, right)))
    zc = jnp.zeros((c_len, LANE), F32)
    tables = jnp.stack([jnp.concatenate([cos_t, ident]), jnp.concatenate([sa_t, zc]),
                        jnp.concatenate([sb_t, zc])])
    if lane_off == 0 and LANE % rot_dim == 0:
        tables = jnp.tile(tables[..., :rot_dim], (1, 1, LANE // rot_dim))
    return tables


def _tile_lane(v):
    return jnp.tile(v, LANE // v.shape[0]).reshape(1, LANE)


def kernel(x, c, ctx, c_ctx, w_ada, b_ada, norm_mix, norm_ffn, w_in, q_norm_gqa, k_norm_gqa,
           q_lora_norm, kv_lora_norm, w_uq, w_ukv, rpb, w_o_na, w_o_gqa, w_o_mla, w_out,
           w_ffn_gu, w_ffn_dn, w_router, w_moe_gu, w_moe_dn, norm_final):
    b, s_len, d = x.shape
    c_len = ctx.shape[1]
    t_len = s_len + c_len
    depth = w_ada.shape[0]
    n_rows = s_len // GRID_W
    n_tok = b * t_len
    n_tiles = n_tok // TOKEN_TILE

    c_rows = jnp.zeros((16, d), F32).at[:b].set(c).at[b].set(c_ctx)
    mods = _ada_call(c_rows, w_ada, b_ada)
    rope_g = _rope_tables(s_len, c_len, HEAD_DIM, 0)
    rope_m = _rope_tables(s_len, c_len, MLA_ROPE_DIM, MLA_NOPE_DIM)
    na_bias = _na_bias_tables(rpb, n_rows)

    xc = jnp.concatenate([x, ctx], axis=1)
    for i in range(depth):
        lat = mods[i, :b].reshape(b, 1, 6, d)
        cm = jnp.broadcast_to(mods[i, b].reshape(1, 1, 6, d), (b, 1, 6, d))
        mod = jnp.concatenate([lat, cm], axis=1).reshape(2 * b, 6, d)

        w_cat, wuq, wukv = _layer_weights(w_in[i], w_uq[i], w_ukv[i])
        q_na, k_na, v_na, q_g, k_g, v_g, q_m, k_m, v_m, gate, norms = _mixin_call(
            xc, mod, norm_mix[i].reshape(1, d), w_cat, _tile_lane(q_norm_gqa[i]), _tile_lane(k_norm_gqa[i]),
            q_lora_norm[i].reshape(1, -1), kv_lora_norm[i].reshape(1, -1), wuq, wukv, rope_g, rope_m)

        nmax = jnp.max(norms, axis=1)
        y_na = _na_attention(q_na, k_na, v_na, na_bias[i], rpb[i], nmax[:, _N_QNA:_N_QNA + NA_HEADS],
                             nmax[:, _N_KNA:_N_KNA + NA_HEADS], s_len, c_len)
        u_g = (HEAD_DIM * GQA_SCALE * LOG2_E) * jnp.max(jnp.abs(q_norm_gqa[i])) * jnp.max(jnp.abs(k_norm_gqa[i]))
        y_g = _bounded_attention(
            q_g, k_g, v_g, jnp.full((b, GQA_Q_HEADS), u_g, F32), HEAD_DIM,
            n_kv=GQA_KV_HEADS, group=GQA_GROUP, tq=256, q_width=HEAD_DIM, s_len=s_len, c_len=c_len,
            name="gqa_attn")
        u_m = jnp.sqrt(nmax[:, _N_QM:_N_QM + MLA_HEADS] * nmax[:, _N_KM:_N_KM + MLA_HEADS])
        y_m = _bounded_attention(
            q_m, k_m, v_m, u_m, MLA_NOPE_DIM + MLA_ROPE_DIM, n_kv=MLA_HEADS, group=1, tq=512, q_width=LANE,
            s_len=s_len, c_len=c_len, name="mla_attn")
        j = i // 2
        gn = norm_ffn[i].reshape(1, d)
        merge_args = (xc, mod, y_na, y_g, y_m, gate, w_o_na[i].astype(BF16), w_o_gqa[i].astype(BF16),
                      w_o_mla[i].astype(BF16), w_out[i].astype(BF16), gn)
        if i % 2 == 0:
            f_dim = w_ffn_dn.shape[1]
            xc = _merge_call(*merge_args, ffn=(w_ffn_gu[j][:, :f_dim].astype(BF16),
                                               w_ffn_gu[j][:, f_dim:].astype(BF16), w_ffn_dn[j].astype(BF16)))
        else:
            n_exp = w_router.shape[-1]
            wr = jnp.pad(w_router[j], ((0, 0), (0, LANE - n_exp)))
            wr_hi = wr.astype(BF16)
            wr = jnp.stack([wr_hi, (wr - wr_hi.astype(F32)).astype(BF16)])
            xc, idx, wts, cnt = _merge_call(*merge_args, router=(wr, n_exp))
            counts = cnt[0, :n_exp].astype(jnp.int32)
            padded = (counts + MOE_ROW_BLOCK - 1) // MOE_ROW_BLOCK * MOE_ROW_BLOCK
            pad_end = jnp.cumsum(padded)
            pad_start = pad_end - padded
            idx = idx.reshape(n_tok, LANE)
            dest = pad_start[idx[:, 0:TOP_K]] + idx[:, TOP_K:2 * TOP_K]
            dest_blocks = dest.reshape(n_tiles, 1, TOP_K * TOKEN_TILE)
            n_blocks = -(-(n_tok * TOP_K + n_exp * (MOE_ROW_BLOCK - 1)) // MOE_ROW_BLOCK)
            starts = jnp.arange(n_blocks) * MOE_ROW_BLOCK
            block_expert = jnp.minimum(jnp.sum(starts[:, None] >= pad_end[None, :], axis=-1),
                                       n_exp - 1).astype(jnp.int32)
            n_used = (pad_end[-1] // MOE_ROW_BLOCK).astype(jnp.int32).reshape(1)
            n_rows_x = n_blocks * MOE_ROW_BLOCK
            zero_from = jnp.concatenate([pad_start + counts, pad_end[-1:]]).astype(jnp.int32)
            zero_cnt = jnp.concatenate([padded - counts, n_rows_x - pad_end[-1:]]).astype(jnp.int32)
            xs = _dispatch_call(zero_from, zero_cnt, dest_blocks, xc, mod, gn, n_rows_x)
            y_rows = _moe_call(block_expert, n_used, xs, w_moe_gu[j].astype(BF16), w_moe_dn[j].astype(BF16))
            if i == depth - 1:
                return _combine_call(dest_blocks, xc, mod, wts, y_rows, norm_final.reshape(1, d), s_len)
            xc = _combine_call(dest_blocks, xc, mod, wts, y_rows)
    return _final_call(xc, norm_final.reshape(1, d), s_len)
```

```python
import functools

import numpy as np
import jax
import jax.numpy as jnp
from jax import lax
from jax.experimental import pallas as pl
from jax.experimental.pallas import tpu as pltpu

GRID_W = 64
HEAD_DIM = 64
NA_HEADS = 4
WIN_H = 8
WIN_W = 16
GQA_Q_HEADS = 8
GQA_KV_HEADS = 2
GQA_GROUP = GQA_Q_HEADS // GQA_KV_HEADS
MLA_HEADS = 4
MLA_Q_RANK = 256
MLA_KV_RANK = 128
MLA_NOPE_DIM = 64
MLA_ROPE_DIM = 32
MLA_V_DIM = 64
ROPE_THETA = 10000.0
TOP_K = 2
NORM_EPS = 1e-6
NA_SCALE = HEAD_DIM ** -0.5
GQA_SCALE = HEAD_DIM ** -0.5
MLA_SCALE = (MLA_NOPE_DIM + MLA_ROPE_DIM) ** -0.5
LOG2_E = 1.4426950408889634

LANE = 128
MXU_WIDTH = 256
ADA_ROWS = 16
TOKEN_TILE = 256
NA_Q_ROWS = 4
NA_K_ROWS = NA_Q_ROWS + WIN_H
KV_CHUNK = 512
MOE_ROW_BLOCK = 512
VMEM_LIMIT = 56 * 1024 * 1024
NEG_BIG = -1e30
SAFE_LOG2_RANGE = 80.0
NORM_SLACK = 1.02
_N_QM, _N_KM = 0, MLA_HEADS
_N_QNA, _N_KNA = 2 * MLA_HEADS, 2 * MLA_HEADS + NA_HEADS

F32 = jnp.float32
BF16 = jnp.bfloat16


def _params(sem, vmem=VMEM_LIMIT):
    return pltpu.CompilerParams(dimension_semantics=sem, vmem_limit_bytes=vmem)


def _resident(shape):
    zeros = (0,) * len(shape)
    return pl.BlockSpec(shape, lambda *_: zeros, pipeline_mode=pl.Buffered(1))


def _rms(x, n):
    ss = jnp.sum(x * x, axis=-1, keepdims=True)
    return x * lax.rsqrt(ss * (1.0 / n) + NORM_EPS)


def _dot(a, b):
    return jnp.dot(a, b, preferred_element_type=F32)


def _dot_nt(a, b):
    return lax.dot_general(a, b, (((1,), (1,)), ((), ())), preferred_element_type=F32)


def _silu(x):
    return x * jax.nn.sigmoid(x)


def _ada_kernel(c_ref, w_ref, b_ref, o_ref):
    s = _silu(c_ref[...])
    o_ref[0] = jnp.dot(s, w_ref[0], precision=lax.Precision.HIGHEST,
                       preferred_element_type=F32) + b_ref[0]


def _ada_call(c_rows, w_ada, b_ada):
    depth, d, n = w_ada.shape
    rows = c_rows.shape[0]
    tn = 1536 if n % 1536 == 0 else n
    return pl.pallas_call(
        _ada_kernel,
        out_shape=jax.ShapeDtypeStruct((depth, rows, n), F32),
        grid=(depth, n // tn),
        in_specs=[
            pl.BlockSpec((rows, d), lambda i, j: (0, 0)),
            pl.BlockSpec((1, d, tn), lambda i, j: (i, 0, j)),
            pl.BlockSpec((1, 1, tn), lambda i, j: (i, 0, j)),
        ],
        out_specs=pl.BlockSpec((1, rows, tn), lambda i, j: (i, 0, j)),
        compiler_params=_params(("arbitrary", "arbitrary")),
        name="ada_mod",
    )(c_rows, w_ada, b_ada.reshape(depth, 1, n))


def _rope(xh, cos, sa, sb, shift):
    return xh * cos + pltpu.roll(xh, LANE - shift, 1) * sa + pltpu.roll(xh, shift, 1) * sb


_W_NA = 3 * NA_HEADS * HEAD_DIM
_W_QG = GQA_Q_HEADS * HEAD_DIM
_W_KVG = 2 * GQA_KV_HEADS * LANE
_W_LORA = MLA_Q_RANK + MLA_KV_RANK + LANE
_C_NA = 0
_C_QG = _C_NA + _W_NA
_C_KVG = _C_QG + _W_QG
_C_LORA = _C_KVG + _W_KVG
_C_GATE = _C_LORA + _W_LORA


def _mixin_kernel(x_ref, mod0_ref, mod1_ref, gmix_ref, w_ref, qn_ref, kn_ref, qln_ref, kvln_ref,
                  wuq_ref, wukv_ref, rg0_ref, rg1_ref, rm0_ref, rm1_ref, *out_refs):
    shared = (gmix_ref, w_ref, qn_ref, kn_ref, qln_ref, kvln_ref, wuq_ref, wukv_ref)
    half = x_ref.shape[0] // 2
    gates0 = _mixin_tile(slice(0, half), x_ref, mod0_ref, *shared, rg0_ref, rm0_ref, *out_refs)
    gates1 = _mixin_tile(slice(half, 2 * half), x_ref, mod1_ref, *shared, rg1_ref, rm1_ref, *out_refs)
    gates0()
    gates1()


def _mixin_tile(rows, x_ref, mod_ref, gmix_ref, w_ref, qn_ref, kn_ref, qln_ref, kvln_ref,
                wuq_ref, wukv_ref, rg_ref, rm_ref,
                qna_ref, kna_ref, vna_ref, qg_ref, kg_ref, vg_ref, qm_ref, km_ref, vm_ref,
                gate_ref, nrm_ref):
    x = x_ref[rows, :]
    d = x.shape[-1]
    tm = x.shape[0]
    mod = mod_ref[0]
    h = _rms(x, d) * gmix_ref[...]
    h = h * (1.0 + mod[1:2]) + mod[0:1]
    hb = h.astype(BF16)

    def proj(lo, width):
        return _dot(hb, w_ref[:, lo:lo + width])

    def block(a, j):
        return a[:, j * LANE:(j + 1) * LANE]

    lane = lax.broadcasted_iota(jnp.int32, (tm, LANE), 1)
    low = lane < HEAD_DIM
    ones_pad = jnp.where(low, 0.0, 1.0)
    norms = []

    def unit_at(pos):
        return jnp.where(lane == pos, 1.0, 0.0)

    cos_g, sa_g, sb_g = rg_ref[0], rg_ref[1], rg_ref[2]
    cos_m, sa_m, sb_m = rm_ref[0], rm_ref[1], rm_ref[2]
    g_half = HEAD_DIM // 2
    m_half = MLA_ROPE_DIM // 2

    p_lora = proj(_C_LORA, _W_LORA)
    cq = (_rms(p_lora[:, 0:MLA_Q_RANK], MLA_Q_RANK) * qln_ref[...]).astype(BF16)
    q_up = _dot(cq, wuq_ref[...])
    ckv = p_lora[:, MLA_Q_RANK:MLA_Q_RANK + MLA_KV_RANK]
    ckv = (_rms(ckv, MLA_KV_RANK) * kvln_ref[...]).astype(BF16)
    kv_up = _dot(ckv, wukv_ref[...])
    k_rope = _rope(p_lora[:, MLA_Q_RANK + MLA_KV_RANK:], cos_m, sa_m, sb_m, m_half)
    for hh in range(MLA_HEADS):
        sl = slice(hh * LANE, (hh + 1) * LANE)
        qh = _rope(block(q_up, hh), cos_m, sa_m, sb_m, m_half) * (MLA_SCALE * LOG2_E)
        qm_ref[rows, sl] = qh.astype(BF16)
        kh = block(kv_up, hh) + k_rope
        norms.append((_N_QM + hh, jnp.sum(qh * qh, axis=-1, keepdims=True)))
        norms.append((_N_KM + hh, jnp.sum(kh * kh, axis=-1, keepdims=True)))
        km_ref[rows, sl] = (kh + unit_at(MLA_NOPE_DIM + MLA_ROPE_DIM)).astype(BF16)
        vm_ref[rows, sl] = (block(kv_up, MLA_HEADS + hh) + ones_pad).astype(BF16)

    p_qg = proj(_C_QG, _W_QG)
    for j in range(GQA_Q_HEADS // 2):
        xp = block(p_qg, j)
        sq = xp * xp
        ss_lo = jnp.sum(jnp.where(low, sq, 0.0), axis=-1, keepdims=True)
        ss_hi = jnp.sum(jnp.where(low, 0.0, sq), axis=-1, keepdims=True)
        inv = jnp.where(low, lax.rsqrt(ss_lo * (1.0 / HEAD_DIM) + NORM_EPS),
                        lax.rsqrt(ss_hi * (1.0 / HEAD_DIM) + NORM_EPS))
        xp = _rope(xp * inv * qn_ref[...], cos_g, sa_g, sb_g, g_half) * (GQA_SCALE * LOG2_E)
        qg_ref[rows, j * LANE:(j + 1) * LANE] = xp.astype(BF16)
    p_kvg = proj(_C_KVG, _W_KVG)
    for hh in range(GQA_KV_HEADS):
        xh = _rope(_rms(block(p_kvg, hh), HEAD_DIM) * kn_ref[...], cos_g, sa_g, sb_g, g_half)
        kg_ref[rows, hh * LANE:(hh + 1) * LANE] = (xh + unit_at(HEAD_DIM)).astype(BF16)
        vh = block(p_kvg, GQA_KV_HEADS + hh) + ones_pad
        vg_ref[rows, hh * LANE:(hh + 1) * LANE] = vh.astype(BF16)

    nw = NA_HEADS * HEAD_DIM
    p_na = proj(_C_NA, _W_NA)
    q_na = p_na[:, 0:nw] * (NA_SCALE * LOG2_E)
    qna_ref[rows, :] = q_na.astype(BF16)
    kna_ref[rows, :] = p_na[:, nw:2 * nw].astype(BF16)
    vna_ref[rows, :] = p_na[:, 2 * nw:3 * nw].astype(BF16)
    for j in range(NA_HEADS // 2):
        for pos, blk in ((_N_QNA, block(q_na, j)), (_N_KNA, block(p_na, NA_HEADS // 2 + j))):
            sq = blk * blk
            norms.append((pos + 2 * j, jnp.sum(jnp.where(low, sq, 0.0), axis=-1, keepdims=True)))
            norms.append((pos + 2 * j + 1, jnp.sum(jnp.where(low, 0.0, sq), axis=-1, keepdims=True)))
    packed = jnp.zeros((tm, LANE), F32)
    for pos, val in norms:
        packed = jnp.where(lane == pos, val, packed)
    nrm_ref[rows, :] = packed

    def gates():
        for j in range(3):
            gate_ref[rows, j * d:(j + 1) * d] = jax.nn.sigmoid(proj(_C_GATE + j * d, d)).astype(BF16)

    return gates


def _mixin_call(xc, mod, gmix, w_cat, qn, kn, qln, kvln, wuq, wukv, rope_g, rope_m):
    b, t, d = xc.shape
    tm = TOKEN_TILE
    nt = t // tm
    n_tiles = b * nt
    assert n_tiles % 2 == 0
    tok = lambda w: pl.BlockSpec((2 * tm, w), lambda i: (i, 0))
    widths = [NA_HEADS * HEAD_DIM] * 3 + [GQA_Q_HEADS * HEAD_DIM, GQA_KV_HEADS * LANE,
                                          GQA_KV_HEADS * LANE, MLA_HEADS * LANE,
                                          MLA_HEADS * LANE, MLA_HEADS * LANE, 3 * d]

    def mod_spec(h):
        return pl.BlockSpec((1, 6, d), lambda i: (
            2 * ((2 * i + h) // nt) + ((2 * i + h) % nt == nt - 1).astype(jnp.int32), 0, 0))

    def rope_spec(h):
        return pl.BlockSpec((3, tm, LANE), lambda i: (0, (2 * i + h) % nt, 0))

    outs = pl.pallas_call(
        _mixin_kernel,
        out_shape=[jax.ShapeDtypeStruct((b * t, w), BF16) for w in widths]
        + [jax.ShapeDtypeStruct((b * t, LANE), F32)],
        grid=(n_tiles // 2,),
        in_specs=[
            tok(d), mod_spec(0), mod_spec(1),
            _resident((1, d)),
            _resident(w_cat.shape),
            _resident((1, LANE)), _resident((1, LANE)),
            _resident((1, MLA_Q_RANK)), _resident((1, MLA_KV_RANK)),
            _resident(wuq.shape), _resident(wukv.shape),
            rope_spec(0), rope_spec(1), rope_spec(0), rope_spec(1),
        ],
        out_specs=[tok(w) for w in widths] + [tok(LANE)],
        compiler_params=_params(("arbitrary",)),
        name="mixer_in",
    )(xc.reshape(b * t, d), mod, mod, gmix, w_cat, qn, kn, qln, kvln, wuq, wukv, rope_g, rope_g, rope_m, rope_m)
    return [o.reshape(b, t, o.shape[-1]) for o in outs]


def _na_kernel(q_ref, k_ref, v_ref, bias_ref, u_ref, o_ref, *, n_groups, n_rows, s_len, c_len, bounded):
    g = pl.program_id(1)
    qn = NA_Q_ROWS * GRID_W
    kn = NA_K_ROWS * GRID_W
    lane = lax.broadcasted_iota(jnp.int32, (qn, LANE), 1)
    low = lane < HEAD_DIM

    def attend_pairs(k_nb, v_nb):
        k_c = k_ref[0, s_len:s_len + c_len, :]
        v_c = v_ref[0, s_len:s_len + c_len, :]
        for pair in range(NA_HEADS // 2):
            sl = slice(pair * LANE, (pair + 1) * LANE)
            qp = q_ref[0, :, sl]
            outs = []
            for sub in range(2):
                head = 2 * pair + sub
                qm = jnp.where(low if sub == 0 else jnp.logical_not(low), qp, jnp.zeros_like(qp))
                s_c = _dot_nt(qm, k_c[:, sl])
                if k_nb is not None:
                    s_nb = _dot_nt(qm, k_nb[:, sl]) + bias_ref[0, head]
                if bounded:
                    p_c = jnp.exp2(s_c - u_ref[head:head + 1, :])
                else:
                    m = jnp.max(s_c, axis=-1, keepdims=True)
                    if k_nb is not None:
                        m = jnp.maximum(m, jnp.max(s_nb, axis=-1, keepdims=True))
                    p_c = jnp.exp2(s_c - m)
                den = jnp.sum(p_c, axis=-1, keepdims=True)
                o = _dot(p_c.astype(BF16), v_c[:, sl])
                if k_nb is not None:
                    p_nb = jnp.exp2(s_nb if bounded else s_nb - m)
                    den = den + jnp.sum(p_nb, axis=-1, keepdims=True)
                    o = o + _dot(p_nb.astype(BF16), v_nb[:, sl])
                outs.append(o / den)
            o_ref[0, :, sl] = jnp.where(low, outs[0], outs[1]).astype(BF16)

    @pl.when(g < n_groups)
    def _():
        first_row = jnp.clip(NA_Q_ROWS * g - WIN_H // 2, 0, n_rows - NA_K_ROWS)
        start = pl.multiple_of(first_row * GRID_W, GRID_W)
        attend_pairs(k_ref[0, pl.ds(start, kn), :], v_ref[0, pl.ds(start, kn), :])

    @pl.when(g == n_groups)
    def _():
        attend_pairs(None, None)


def _na_attention(q, k, v, bias, rpb_l, q_norm2, k_norm2, s_len, c_len):
    qk = jnp.max(jnp.sqrt(q_norm2 * k_norm2), axis=0) * NORM_SLACK
    b_hi = jnp.maximum(jnp.max(rpb_l, axis=(1, 2)) * LOG2_E, 0.0)
    b_lo = jnp.minimum(jnp.min(rpb_l, axis=(1, 2)) * LOG2_E, 0.0)
    u = qk + b_hi
    safe = jnp.max(2.0 * qk + b_hi - b_lo) < SAFE_LOG2_RANGE
    u_rows = jnp.broadcast_to(u[:, None], (NA_HEADS, c_len)).astype(F32)
    shifted = jnp.where(bias > 0.5 * NEG_BIG, bias - u[None, :, None, None], NEG_BIG)
    return lax.cond(safe,
                    lambda ops: _na_call(ops[0], ops[1], ops[2], ops[3], ops[5], s_len, c_len, True),
                    lambda ops: _na_call(ops[0], ops[1], ops[2], ops[4], ops[5], s_len, c_len, False),
                    (q, k, v, shifted, bias, u_rows))


def _na_call(q, k, v, bias, u_rows, s_len, c_len, bounded):
    b, t, w = q.shape
    qn = NA_Q_ROWS * GRID_W
    kn = NA_K_ROWS * GRID_W
    n_rows = s_len // GRID_W
    n_groups = s_len // qn
    assert c_len == qn and n_rows >= NA_K_ROWS

    def variant(g):
        return jnp.where(g == 0, 0, jnp.where(g >= n_groups - 1, 2, 1))

    return pl.pallas_call(
        functools.partial(_na_kernel, n_groups=n_groups, n_rows=n_rows, s_len=s_len, c_len=c_len,
                          bounded=bounded),
        out_shape=jax.ShapeDtypeStruct((b, t, w), BF16),
        grid=(b, n_groups + 1),
        in_specs=[
            pl.BlockSpec((1, qn, w), lambda bi, g: (bi, g, 0)),
            pl.BlockSpec((1, t, w), lambda bi, g: (bi, 0, 0)),
            pl.BlockSpec((1, t, w), lambda bi, g: (bi, 0, 0)),
            pl.BlockSpec((1, NA_HEADS, qn, kn), lambda bi, g: (variant(g), 0, 0, 0)),
            pl.BlockSpec(u_rows.shape, lambda bi, g: (0, 0)),
        ],
        out_specs=pl.BlockSpec((1, qn, w), lambda bi, g: (bi, g, 0)),
        compiler_params=_params(("arbitrary", "arbitrary")),
        name="na_attn" if bounded else "na_attn_max",
    )(q, k, v, bias, u_rows)


def _na_bias_tables(rpb, n_rows):
    qi = np.arange(NA_Q_ROWS)[:, None]
    kj = np.arange(NA_K_ROWS)[None, :]
    qc = np.arange(GRID_W)[:, None]
    kc = np.arange(GRID_W)[None, :]
    col0 = np.clip(qc - WIN_W // 2, 0, GRID_W - WIN_W)
    col_ok = (kc >= col0) & (kc < col0 + WIN_W)
    dc = np.clip(kc - qc, -(WIN_W - 1), WIN_W - 1) + WIN_W - 1
    col_sel = np.eye(2 * WIN_W - 1, dtype=np.float32)[dc]
    row_sel, valid = [], []
    for r0, u0 in ((0, 0), (WIN_H // 2, 0), (n_rows - NA_Q_ROWS, n_rows - NA_K_ROWS)):
        qr = r0 + qi
        kr = u0 + kj
        row0 = np.clip(qr - WIN_H // 2, 0, n_rows - WIN_H)
        row_ok = (kr >= row0) & (kr < row0 + WIN_H)
        dr = np.clip(kr - qr, -(WIN_H - 1), WIN_H - 1) + WIN_H - 1
        row_sel.append(np.eye(2 * WIN_H - 1, dtype=np.float32)[dr])
        valid.append(row_ok[:, None, :, None] & col_ok[None, :, None, :])
    row_sel = jnp.asarray(np.stack(row_sel))
    valid = np.stack(valid).reshape(3, 1, NA_Q_ROWS * GRID_W, NA_K_ROWS * GRID_W)
    picked = jnp.einsum('vija,lhab,qkb->lvhiqjk', row_sel, rpb.astype(F32) * LOG2_E, jnp.asarray(col_sel),
                        precision=lax.Precision.HIGHEST)
    picked = picked.reshape(rpb.shape[0], 3, NA_HEADS, NA_Q_ROWS * GRID_W, NA_K_ROWS * GRID_W)
    return jnp.where(valid[None], picked, NEG_BIG)


def _row_max(s):
    parts = [s[:, j * LANE:(j + 1) * LANE] for j in range(s.shape[1] // LANE)]
    while len(parts) > 1:
        parts = [jnp.maximum(parts[j], parts[j + 1]) for j in range(0, len(parts) - 1, 2)] + parts[len(parts) & ~1:]
    return jnp.max(parts[0], axis=-1, keepdims=True)


def _flash_kernel(q_ref, k_ref, v_ref, u_ref, o_ref, *, n_kv, group, tq, q_width, s_len, c_len,
                  with_latent, online):
    low = lax.broadcasted_iota(jnp.int32, (tq, LANE), 1) < HEAD_DIM
    heads = range(n_kv)
    lanes = [slice(h * LANE, (h + 1) * LANE) for h in heads]

    def load_q(head):
        if q_width == LANE:
            qh = q_ref[0, :, head * LANE:(head + 1) * LANE]
        else:
            pair = q_ref[0, :, (head // 2) * LANE:(head // 2 + 1) * LANE]
            if head % 2:
                pair = pltpu.roll(pair.astype(F32), HEAD_DIM, 1).astype(BF16)
            qh = jnp.where(low, pair, jnp.zeros_like(pair))
        return qh if online else qh + u_ref[0, head:head + 1, :]

    qs = []
    for h in heads:
        parts = [load_q(h * group + g) for g in range(group)]
        qs.append(parts[0] if group == 1 else jnp.concatenate(parts, axis=0))

    if online:
        carry = []
        for h in heads:
            s = _dot_nt(qs[h], k_ref[0, s_len:s_len + c_len, lanes[h]])
            m = _row_max(s)
            carry += [m, _dot(jnp.exp2(s - m).astype(BF16), v_ref[0, s_len:s_len + c_len, lanes[h]])]
        if with_latent:
            def body(i, carry):
                st = pl.multiple_of(i * KV_CHUNK, KV_CHUNK)
                new = []
                for h in heads:
                    m_old, acc_old = carry[2 * h], carry[2 * h + 1]
                    sc = _dot_nt(qs[h], k_ref[0, pl.ds(st, KV_CHUNK), lanes[h]])
                    m_new = jnp.maximum(m_old, _row_max(sc))
                    p = jnp.exp2(sc - m_new).astype(BF16)
                    pv = _dot(p, v_ref[0, pl.ds(st, KV_CHUNK), lanes[h]])
                    new += [m_new, jnp.exp2(m_old - m_new) * acc_old + pv]
                return tuple(new)
            carry = lax.fori_loop(0, s_len // KV_CHUNK, body, tuple(carry), unroll=2)
        accs = [carry[2 * h + 1] for h in heads]
    else:
        chunks = [(s_len, c_len)]
        if with_latent:
            chunks += [(i * KV_CHUNK, KV_CHUNK) for i in range(s_len // KV_CHUNK)]
        accs = [None] * n_kv
        for st, size in chunks:
            for h in heads:
                p = jnp.exp2(_dot_nt(qs[h], k_ref[0, st:st + size, lanes[h]])).astype(BF16)
                pv = _dot(p, v_ref[0, st:st + size, lanes[h]])
                accs[h] = pv if accs[h] is None else accs[h] + pv
    outs = []
    for h in heads:
        acc = accs[h]
        o = acc / pltpu.roll(acc, HEAD_DIM, 1)
        for g in range(group):
            outs.append(o[g * tq:(g + 1) * tq])
    for pair in range(len(outs) // 2):
        packed = jnp.where(low, outs[2 * pair], pltpu.roll(outs[2 * pair + 1], HEAD_DIM, 1))
        o_ref[0, :, pair * LANE:(pair + 1) * LANE] = packed.astype(BF16)


def _flash_call(q, k, v, shift, *, n_kv, group, tq, q_width, s_len, c_len, online, name):
    b, t, _ = q.shape
    n_heads = n_kv * group
    tq_ctx = min(tq, c_len)
    assert s_len % tq == 0 and c_len % tq_ctx == 0 and s_len % tq_ctx == 0 and s_len % KV_CHUNK == 0
    kv_spec = pl.BlockSpec((1, t, n_kv * LANE), lambda bi, ti: (bi, 0, 0))
    u_spec = pl.BlockSpec((1, n_heads, LANE), lambda bi, ti: (bi, 0, 0))
    common = dict(n_kv=n_kv, group=group, q_width=q_width, s_len=s_len, c_len=c_len, online=online)
    width = n_heads * HEAD_DIM
    tag = "_online" if online else ""
    y_lat = pl.pallas_call(
        functools.partial(_flash_kernel, tq=tq, with_latent=True, **common),
        out_shape=jax.ShapeDtypeStruct((b, s_len, width), BF16),
        grid=(b, s_len // tq),
        in_specs=[pl.BlockSpec((1, tq, n_heads * q_width), lambda bi, ti: (bi, ti, 0)), kv_spec, kv_spec, u_spec],
        out_specs=pl.BlockSpec((1, tq, width), lambda bi, ti: (bi, ti, 0)),
        compiler_params=_params(("arbitrary", "arbitrary")),
        name=name + tag,
    )(q, k, v, shift)
    first = s_len // tq_ctx
    y_ctx = pl.pallas_call(
        functools.partial(_flash_kernel, tq=tq_ctx, with_latent=False, **common),
        out_shape=jax.ShapeDtypeStruct((b, c_len, width), BF16),
        grid=(b, c_len // tq_ctx),
        in_specs=[pl.BlockSpec((1, tq_ctx, n_heads * q_width), lambda bi, ti: (bi, first + ti, 0)), kv_spec, kv_spec,
                  u_spec],
        out_specs=pl.BlockSpec((1, tq_ctx, width), lambda bi, ti: (bi, ti, 0)),
        compiler_params=_params(("arbitrary", "arbitrary")),
        name=name + "_ctx" + tag,
    )(q, k, v, shift)
    return y_lat, y_ctx


def _bounded_attention(q, k, v, bound, unit_lane, **kw):
    bound = bound * NORM_SLACK
    lane = jnp.arange(LANE) == unit_lane
    shift = jnp.where(lane[None, None, :], -bound[:, :, None], 0.0).astype(BF16)
    safe = jnp.max(bound) * 2.0 < SAFE_LOG2_RANGE
    return lax.cond(safe,
                    lambda ops: _flash_call(*ops, online=False, **kw),
                    lambda ops: _flash_call(*ops, online=True, **kw),
                    (q, k, v, shift))


def _mod_spec(d, nt):
    return pl.BlockSpec((1, 6, d), lambda bi, ti: (2 * bi + (ti == nt - 1).astype(jnp.int32), 0, 0))


def _merged_residual(x_ref, mod_ref, yna_ref, yg_ref, ygc_ref, ym_ref, ymc_ref, gate_ref, wna_ref, wg_ref,
                     wm_ref, wout_ref, n_lat_tiles):
    d = x_ref.shape[-1]
    is_ctx = pl.program_id(1) >= n_lat_tiles
    y_g = jnp.where(is_ctx, ygc_ref[0], yg_ref[0])
    y_m = jnp.where(is_ctx, ymc_ref[0], ym_ref[0])
    m = gate_ref[0, :, 0:d].astype(F32) * _dot(yna_ref[0], wna_ref[...])
    m = m + gate_ref[0, :, d:2 * d].astype(F32) * _dot(y_g, wg_ref[...])
    m = m + gate_ref[0, :, 2 * d:3 * d].astype(F32) * _dot(y_m, wm_ref[...])
    r = _dot(m.astype(BF16), wout_ref[...])
    return x_ref[0] + mod_ref[0][2:3] * r


def _norm2(x, mod, gain):
    h = _rms(x, x.shape[-1]) * gain
    return h * (1.0 + mod[4:5]) + mod[3:4]


def _merge_ffn_kernel(*refs, n_lat_tiles):
    mix_refs, (gn_ref, wg_ref, wu_ref, wd_ref, o_ref) = refs[:12], refs[12:]
    x = _merged_residual(*mix_refs, n_lat_tiles)
    mod = mix_refs[1][0]
    hb = _norm2(x, mod, gn_ref[...]).astype(BF16)
    act = (_silu(_dot(hb, wg_ref[...])) * _dot(hb, wu_ref[...])).astype(BF16)
    o_ref[0] = x + mod[5:6] * _dot(act, wd_ref[...])


def _merge_router_kernel(*refs, n_lat_tiles, n_experts):
    mix_refs, (gn_ref, wr_ref, o_ref, idx_ref, wt_ref, cnt_ref, carry_ref) = refs[:12], refs[12:]

    @pl.when((pl.program_id(0) == 0) & (pl.program_id(1) == 0))
    def _():
        carry_ref[...] = jnp.zeros_like(carry_ref)

    x = _merged_residual(*mix_refs, n_lat_tiles)
    o_ref[0] = x
    tm = x.shape[0]
    h = _norm2(x, mix_refs[1][0], gn_ref[...])
    h_hi = h.astype(BF16)
    h_lo = (h - h_hi.astype(F32)).astype(BF16)
    logits = _dot(h_hi, wr_ref[0]) + (_dot(h_lo, wr_ref[0]) + _dot(h_hi, wr_ref[1]))
    lane = lax.broadcasted_iota(jnp.int32, (tm, LANE), 1).astype(F32)
    logits = jnp.where(lane < n_experts, logits, -jnp.inf)
    m1 = jnp.max(logits, axis=-1, keepdims=True)
    i1 = jnp.min(jnp.where(logits == m1, lane, float(LANE)), axis=-1, keepdims=True)
    rest = jnp.where(lane == i1, -jnp.inf, logits)
    m2 = jnp.max(rest, axis=-1, keepdims=True)
    i2 = jnp.min(jnp.where(rest == m2, lane, float(LANE)), axis=-1, keepdims=True)
    e2 = jnp.exp(m2 - m1)
    w1 = 1.0 / (1.0 + e2)
    w2 = e2 / (1.0 + e2)
    hot1 = lane == i1
    hot2 = lane == i2
    hot = jnp.where(hot1 | hot2, 1.0, 0.0)
    rows = lax.broadcasted_iota(jnp.int32, (tm, tm), 0)
    cols = lax.broadcasted_iota(jnp.int32, (tm, tm), 1)
    below = jnp.where(rows > cols, 1.0, 0.0).astype(BF16)
    before = _dot(below, hot.astype(BF16)) + carry_ref[0:1, :]
    r1 = jnp.sum(jnp.where(hot1, before, 0.0), axis=-1, keepdims=True)
    r2 = jnp.sum(jnp.where(hot2, before, 0.0), axis=-1, keepdims=True)
    total = carry_ref[0:1, :] + jnp.sum(hot, axis=0, keepdims=True)
    carry_ref[...] = jnp.broadcast_to(total, carry_ref.shape)
    cnt_ref[...] = jnp.broadcast_to(total, cnt_ref.shape)
    packed = jnp.where(lane == 0, i1, jnp.where(lane == 1, i2, jnp.where(lane == 2, r1, jnp.where(lane == 3, r2, 0.0))))
    idx_ref[0] = packed.astype(jnp.int32)
    wt_ref[0] = jnp.where(lane == 0, w1, jnp.where(lane == 1, w2, 0.0))


def _merge_call(xc, mod, y_na, y_g, y_m, gate, w_na, w_g, w_m, w_out, gn, *, ffn=None, router=None):
    b, t, d = xc.shape
    tm = TOKEN_TILE
    nt = t // tm
    n_lat = y_g[0].shape[1] // tm
    tok = lambda w: pl.BlockSpec((1, tm, w), lambda bi, ti: (bi, ti, 0))
    lat = lambda w: pl.BlockSpec((1, tm, w), lambda bi, ti: (bi, jnp.minimum(ti, n_lat - 1), 0))
    ctx = lambda w: pl.BlockSpec((1, tm, w), lambda bi, ti: (bi, jnp.maximum(ti - n_lat, 0), 0))
    wg_, wm_ = y_g[0].shape[-1], y_m[0].shape[-1]
    mix_specs = [tok(d), _mod_spec(d, nt), tok(y_na.shape[-1]), lat(wg_), ctx(wg_), lat(wm_), ctx(wm_),
                 tok(3 * d), _resident(w_na.shape), _resident(w_g.shape), _resident(w_m.shape),
                 _resident(w_out.shape), _resident((1, d))]
    mix_args = (xc, mod, y_na, y_g[0], y_g[1], y_m[0], y_m[1], gate, w_na, w_g, w_m, w_out, gn)
    x_shape = jax.ShapeDtypeStruct(xc.shape, F32)
    if ffn is not None:
        return pl.pallas_call(
            functools.partial(_merge_ffn_kernel, n_lat_tiles=n_lat),
            out_shape=x_shape,
            grid=(b, nt),
            in_specs=mix_specs + [_resident(w.shape) for w in ffn],
            out_specs=tok(d),
            input_output_aliases={0: 0},
            compiler_params=_params(("arbitrary", "arbitrary")),
            name="merge_ffn",
        )(*mix_args, *ffn)
    w_router_pad, n_experts = router
    return pl.pallas_call(
        functools.partial(_merge_router_kernel, n_lat_tiles=n_lat, n_experts=n_experts),
        out_shape=[x_shape, jax.ShapeDtypeStruct((b, t, LANE), jnp.int32),
                   jax.ShapeDtypeStruct((b, t, LANE), F32), jax.ShapeDtypeStruct((8, LANE), F32)],
        grid=(b, nt),
        in_specs=mix_specs + [_resident(w_router_pad.shape)],
        out_specs=[tok(d), tok(LANE), tok(LANE), pl.BlockSpec((8, LANE), lambda bi, ti: (0, 0))],
        scratch_shapes=[pltpu.VMEM((8, LANE), F32)],
        input_output_aliases={0: 0},
        compiler_params=_params(("arbitrary", "arbitrary")),
        name="merge_router",
    )(*mix_args, w_router_pad)


def _row_copy(src, src_row, dst, dst_row, sem):
    return pltpu.make_async_copy(src.at[pl.ds(src_row, 1)], dst.at[pl.ds(dst_row, 1)], sem)


def _dispatch_kernel(pad_from_ref, pad_cnt_ref, dest_ref, x_ref, mod_ref, gn_ref, xs_ref, h_buf, zrow, sem,
                     zsem, *, n_tiles, n_exp):
    i = pl.program_id(0)
    slot = i % 2
    tm = x_ref.shape[1]

    @pl.when(i == 0)
    def _():
        zrow[...] = jnp.zeros_like(zrow)
        for e in range(n_exp):
            def zero_issue(r, carry, e=e):
                _row_copy(zrow, 0, xs_ref, pad_from_ref[e] + r, zsem).start()
                return carry
            lax.fori_loop(0, pad_cnt_ref[e], zero_issue, 0)
        for e in range(n_exp):
            def zero_drain(r, carry):
                _row_copy(zrow, 0, xs_ref, 0, zsem).wait()
                return carry
            lax.fori_loop(0, pad_cnt_ref[e], zero_drain, 0)

    @pl.when(i < n_tiles)
    def _():
        h_buf[slot] = _norm2(x_ref[0], mod_ref[0], gn_ref[...])

        def issue(r, carry):
            for k in range(TOP_K):
                _row_copy(h_buf.at[slot], r, xs_ref, dest_ref[0, 0, TOP_K * r + k], sem.at[slot]).start()
            return carry

        lax.fori_loop(0, tm, issue, 0, unroll=8)

    @pl.when(i > 0)
    def _():
        def drain(r, carry):
            for k in range(TOP_K):
                _row_copy(h_buf.at[1 - slot], 0, xs_ref, 0, sem.at[1 - slot]).wait()
            return carry

        lax.fori_loop(0, tm, drain, 0, unroll=8)


def _dispatch_call(pad_from, pad_cnt, dest_blocks, xc, mod, gn, n_rows):
    b, t, d = xc.shape
    tm = TOKEN_TILE
    nt = t // tm
    n_tiles = b * nt

    def tile(i):
        j = jnp.minimum(i, n_tiles - 1)
        return j // nt, j % nt

    grid_spec = pltpu.PrefetchScalarGridSpec(
        num_scalar_prefetch=2,
        grid=(n_tiles + 1,),
        in_specs=[
            pl.BlockSpec((1, 1, dest_blocks.shape[-1]), lambda i, pf, pc: (jnp.minimum(i, n_tiles - 1), 0, 0),
                         memory_space=pltpu.SMEM),
            pl.BlockSpec((1, tm, d), lambda i, pf, pc: (*tile(i), 0)),
            pl.BlockSpec((1, 6, d), lambda i, pf, pc: (
                2 * tile(i)[0] + (tile(i)[1] == nt - 1).astype(jnp.int32), 0, 0)),
            pl.BlockSpec((1, d), lambda i, pf, pc: (0, 0)),
        ],
        out_specs=pl.BlockSpec(memory_space=pl.ANY),
        scratch_shapes=[pltpu.VMEM((2, tm, d), F32), pltpu.VMEM((8, d), F32),
                        pltpu.SemaphoreType.DMA((2,)), pltpu.SemaphoreType.DMA],
    )
    return pl.pallas_call(
        functools.partial(_dispatch_kernel, n_tiles=n_tiles, n_exp=pad_from.shape[0]),
        out_shape=jax.ShapeDtypeStruct((n_rows, d), F32),
        grid_spec=grid_spec,
        compiler_params=_params(("arbitrary",)),
        name="moe_dispatch",
    )(pad_from, pad_cnt, dest_blocks, xc, mod, gn)


def _moe_kernel(be_ref, nu_ref, xs_ref, wgu_ref, wd_ref, y_ref, *, f_chunk):
    f_dim = wd_ref.shape[1]

    @pl.when(pl.program_id(0) < nu_ref[0])
    def _():
        xb = xs_ref[...].astype(BF16)
        acc = None
        for c in range(0, f_dim, f_chunk):
            gate = _dot(xb, wgu_ref[0, :, c:c + f_chunk])
            up = _dot(xb, wgu_ref[0, :, f_dim + c:f_dim + c + f_chunk])
            part = _dot((_silu(gate) * up).astype(BF16), wd_ref[0, c:c + f_chunk, :])
            acc = part if acc is None else acc + part
        y_ref[...] = acc

    @pl.when(pl.program_id(0) >= nu_ref[0])
    def _():
        y_ref[...] = jnp.zeros_like(y_ref)


def _moe_call(block_expert, n_used, xs, w_gu, w_dn):
    rows, d = xs.shape
    n_exp, _, two_f = w_gu.shape
    f_dim = two_f // 2
    f_chunk = MXU_WIDTH if f_dim % MXU_WIDTH == 0 else f_dim
    tmr = MOE_ROW_BLOCK
    grid_spec = pltpu.PrefetchScalarGridSpec(
        num_scalar_prefetch=2,
        grid=(rows // tmr,),
        in_specs=[
            pl.BlockSpec((tmr, d), lambda i, be, nu: (jnp.minimum(i, nu[0] - 1), 0)),
            pl.BlockSpec((1, d, two_f), lambda i, be, nu: (be[i], 0, 0)),
            pl.BlockSpec((1, f_dim, d), lambda i, be, nu: (be[i], 0, 0)),
        ],
        out_specs=pl.BlockSpec((tmr, d), lambda i, be, nu: (i, 0)),
    )
    return pl.pallas_call(
        functools.partial(_moe_kernel, f_chunk=f_chunk),
        out_shape=jax.ShapeDtypeStruct((rows, d), F32),
        grid_spec=grid_spec,
        compiler_params=_params(("arbitrary",)),
        name="moe_experts",
    )(block_expert, n_used, xs, w_gu, w_dn)


def _combine_kernel(dest_ref, x_ref, mod_ref, wt_ref, gain_ref, y_ref, o_ref, buf1, buf2, sem, *, n_tiles,
                    final):
    i = pl.program_id(0)
    slot = i % 2
    tm = x_ref.shape[1]

    @pl.when(i < n_tiles)
    def _():
        def issue(r, carry):
            _row_copy(y_ref, dest_ref[0, 0, TOP_K * r], buf1.at[slot], r, sem.at[slot]).start()
            _row_copy(y_ref, dest_ref[0, 0, TOP_K * r + 1], buf2.at[slot], r, sem.at[slot]).start()
            return carry

        lax.fori_loop(0, tm, issue, 0, unroll=8)

    @pl.when(i > 0)
    def _():
        def drain(r, carry):
            _row_copy(y_ref, 0, buf1.at[1 - slot], 0, sem.at[1 - slot]).wait()
            _row_copy(y_ref, 0, buf2.at[1 - slot], 0, sem.at[1 - slot]).wait()
            return carry

        lax.fori_loop(0, tm, drain, 0, unroll=8)
        wt = wt_ref[0]
        mix = wt[:, 0:1] * buf1[1 - slot] + wt[:, 1:2] * buf2[1 - slot]
        x = x_ref[0] + mod_ref[0][5:6] * mix
        o_ref[0] = _rms(x, x.shape[-1]) * gain_ref[...] if final else x


def _combine_call(dest_blocks, xc, mod, wts, y_rows, final_gain=None, s_len=None):
    b, t, d = xc.shape
    tm = TOKEN_TILE
    nt = t // tm
    final = final_gain is not None
    used = s_len // tm if final else nt
    n_tiles = b * used

    def issued(i):
        j = jnp.minimum(i, n_tiles - 1)
        return j // used, j % used

    def finished(i):
        j = jnp.maximum(i - 1, 0)
        return j // used, j % used

    tok = lambda w: pl.BlockSpec((1, tm, w), lambda i: (*finished(i), 0))
    gain = final_gain if final else jnp.ones((1, d), F32)
    return pl.pallas_call(
        functools.partial(_combine_kernel, n_tiles=n_tiles, final=final),
        out_shape=jax.ShapeDtypeStruct((b, used * tm, d), F32),
        grid=(n_tiles + 1,),
        in_specs=[
            pl.BlockSpec((1, 1, dest_blocks.shape[-1]), lambda i: (issued(i)[0] * nt + issued(i)[1], 0, 0),
                         memory_space=pltpu.SMEM),
            tok(d),
            pl.BlockSpec((1, 6, d), lambda i: (
                2 * finished(i)[0] + (finished(i)[1] == nt - 1).astype(jnp.int32), 0, 0)),
            tok(LANE), pl.BlockSpec((1, d), lambda i: (0, 0)),
            pl.BlockSpec(memory_space=pl.ANY),
        ],
        out_specs=tok(d),
        scratch_shapes=[pltpu.VMEM((2, tm, d), F32), pltpu.VMEM((2, tm, d), F32),
                        pltpu.SemaphoreType.DMA((2,))],
        input_output_aliases={} if final else {1: 0},
        compiler_params=_params(("arbitrary",)),
        name="moe_combine_final" if final else "moe_combine",
    )(dest_blocks, xc, mod, wts, gain, y_rows)


def _final_kernel(x_ref, g_ref, o_ref):
    x = x_ref[0]
    o_ref[0] = _rms(x, x.shape[-1]) * g_ref[...]


def _final_call(xc, gain, s_len):
    b, t, d = xc.shape
    tm = TOKEN_TILE
    tok = pl.BlockSpec((1, tm, d), lambda bi, ti: (bi, ti, 0))
    return pl.pallas_call(
        _final_kernel,
        out_shape=jax.ShapeDtypeStruct((b, s_len, d), F32),
        grid=(b, s_len // tm),
        in_specs=[tok, _resident((1, d))],
        out_specs=tok,
        compiler_params=_params(("arbitrary", "arbitrary")),
        name="final_norm",
    )(xc, gain)


def _pad_heads(w, n_heads, width):
    k = w.shape[0]
    w = w.reshape(k, n_heads, width)
    return jnp.pad(w, ((0, 0), (0, 0), (0, LANE - width))).reshape(k, n_heads * LANE)


def _layer_weights(w_in, w_uq, w_ukv):
    d = w_in.shape[0]
    nw = NA_HEADS * HEAD_DIM
    gk = GQA_KV_HEADS * HEAD_DIM
    o = 0
    k_na = w_in[:, o:o + nw]; o += nw
    v_na = w_in[:, o:o + nw]; o += nw
    k_g = w_in[:, o:o + gk]; o += gk
    v_g = w_in[:, o:o + gk]; o += gk
    c_kv = w_in[:, o:o + MLA_KV_RANK]; o += MLA_KV_RANK
    k_r = w_in[:, o:o + MLA_ROPE_DIM]; o += MLA_ROPE_DIM
    q_na = w_in[:, o:o + nw]; o += nw
    q_g = w_in[:, o:o + GQA_Q_HEADS * HEAD_DIM]; o += GQA_Q_HEADS * HEAD_DIM
    c_q = w_in[:, o:o + MLA_Q_RANK]; o += MLA_Q_RANK
    gate = w_in[:, o:]
    k_r_pad = jnp.pad(k_r, ((0, 0), (MLA_NOPE_DIM, LANE - MLA_NOPE_DIM - MLA_ROPE_DIM)))
    w_cat = jnp.concatenate([
        q_na, k_na, v_na, q_g, _pad_heads(k_g, GQA_KV_HEADS, HEAD_DIM), _pad_heads(v_g, GQA_KV_HEADS, HEAD_DIM), c_q, c_kv, k_r_pad, gate],
        axis=1).astype(BF16)
    assert w_cat.shape[1] == _C_GATE + 3 * d
    wuq = _pad_heads(w_uq, MLA_HEADS, MLA_NOPE_DIM + MLA_ROPE_DIM).astype(BF16)
    kv = w_ukv.reshape(MLA_KV_RANK, MLA_HEADS, MLA_NOPE_DIM + MLA_V_DIM)
    wuk = _pad_heads(kv[:, :, :MLA_NOPE_DIM].reshape(MLA_KV_RANK, -1), MLA_HEADS, MLA_NOPE_DIM)
    wuv = _pad_heads(kv[:, :, MLA_NOPE_DIM:].reshape(MLA_KV_RANK, -1), MLA_HEADS, MLA_V_DIM)
    return w_cat, wuq, jnp.concatenate([wuk, wuv], axis=1).astype(BF16)


def _rope_tables(s_len, c_len, rot_dim, lane_off):
    half = rot_dim // 2
    n_freq = half // 2
    pos = jnp.arange(s_len)
    rows = (pos // GRID_W).astype(F32)
    cols = (pos % GRID_W).astype(F32)
    inv = jnp.power(ROPE_THETA, -jnp.arange(n_freq, dtype=F32) / n_freq)
    ang = jnp.concatenate([rows[:, None] * inv, cols[:, None] * inv], axis=-1)
    cos, sin = jnp.cos(ang), jnp.sin(ang)
    zeros = jnp.zeros((s_len, half), F32)
    right = LANE - lane_off - rot_dim
    pad = lambda a, b: jnp.pad(jnp.concatenate([a, b], axis=-1), ((0, 0), (lane_off, right)))
    cos_t = pad(cos, cos) + jnp.pad(jnp.ones((s_len, lane_off), F32), ((0, 0), (0, LANE - lane_off)))
    sa_t = pad(-sin, zeros)
    sb_t = pad(zeros, sin)
    ident = jnp.pad(jnp.ones((c_len, lane_off + rot_dim), F32), ((0, 0), (0, right)))
    zc = jnp.zeros((c_len, LANE), F32)
    tables = jnp.stack([jnp.concatenate([cos_t, ident]), jnp.concatenate([sa_t, zc]),
                        jnp.concatenate([sb_t, zc])])
    if lane_off == 0 and LANE % rot_dim == 0:
        tables = jnp.tile(tables[..., :rot_dim], (1, 1, LANE // rot_dim))
    return tables


def _tile_lane(v):
    return jnp.tile(v, LANE // v.shape[0]).reshape(1, LANE)


def kernel(x, c, ctx, c_ctx, w_ada, b_ada, norm_mix, norm_ffn, w_in, q_norm_gqa, k_norm_gqa,
           q_lora_norm, kv_lora_norm, w_uq, w_ukv, rpb, w_o_na, w_o_gqa, w_o_mla, w_out,
           w_ffn_gu, w_ffn_dn, w_router, w_moe_gu, w_moe_dn, norm_final):
    b, s_len, d = x.shape
    c_len = ctx.shape[1]
    t_len = s_len + c_len
    depth = w_ada.shape[0]
    n_rows = s_len // GRID_W
    n_tok = b * t_len
    n_tiles = n_tok // TOKEN_TILE

    assert b + 1 <= ADA_ROWS
    c_rows = jnp.zeros((ADA_ROWS, d), F32).at[:b].set(c).at[b].set(c_ctx)
    mods = _ada_call(c_rows, w_ada, b_ada)
    rope_g = _rope_tables(s_len, c_len, HEAD_DIM, 0)
    rope_m = _rope_tables(s_len, c_len, MLA_ROPE_DIM, MLA_NOPE_DIM)
    na_bias = _na_bias_tables(rpb, n_rows)

    xc = jnp.concatenate([x, ctx], axis=1)
    for i in range(depth):
        lat = mods[i, :b].reshape(b, 1, 6, d)
        cm = jnp.broadcast_to(mods[i, b].reshape(1, 1, 6, d), (b, 1, 6, d))
        mod = jnp.concatenate([lat, cm], axis=1).reshape(2 * b, 6, d)

        w_cat, wuq, wukv = _layer_weights(w_in[i], w_uq[i], w_ukv[i])
        q_na, k_na, v_na, q_g, k_g, v_g, q_m, k_m, v_m, gate, norms = _mixin_call(
            xc, mod, norm_mix[i].reshape(1, d), w_cat, _tile_lane(q_norm_gqa[i]), _tile_lane(k_norm_gqa[i]),
            q_lora_norm[i].reshape(1, -1), kv_lora_norm[i].reshape(1, -1), wuq, wukv, rope_g, rope_m)

        nmax = jnp.max(norms, axis=1)
        y_na = _na_attention(q_na, k_na, v_na, na_bias[i], rpb[i], nmax[:, _N_QNA:_N_QNA + NA_HEADS],
                             nmax[:, _N_KNA:_N_KNA + NA_HEADS], s_len, c_len)
        u_g = (HEAD_DIM * GQA_SCALE * LOG2_E) * jnp.max(jnp.abs(q_norm_gqa[i])) * jnp.max(jnp.abs(k_norm_gqa[i]))
        y_g = _bounded_attention(
            q_g, k_g, v_g, jnp.full((b, GQA_Q_HEADS), u_g, F32), HEAD_DIM,
            n_kv=GQA_KV_HEADS, group=GQA_GROUP, tq=512, q_width=HEAD_DIM, s_len=s_len, c_len=c_len,
            name="gqa_attn")
        u_m = jnp.sqrt(nmax[:, _N_QM:_N_QM + MLA_HEADS] * nmax[:, _N_KM:_N_KM + MLA_HEADS])
        y_m = _bounded_attention(
            q_m, k_m, v_m, u_m, MLA_NOPE_DIM + MLA_ROPE_DIM, n_kv=MLA_HEADS, group=1, tq=512, q_width=LANE,
            s_len=s_len, c_len=c_len, name="mla_attn")
        j = i // 2
        gn = norm_ffn[i].reshape(1, d)
        merge_args = (xc, mod, y_na, y_g, y_m, gate, w_o_na[i].astype(BF16), w_o_gqa[i].astype(BF16),
                      w_o_mla[i].astype(BF16), w_out[i].astype(BF16), gn)
        if i % 2 == 0:
            f_dim = w_ffn_dn.shape[1]
            xc = _merge_call(*merge_args, ffn=(w_ffn_gu[j][:, :f_dim].astype(BF16),
                                               w_ffn_gu[j][:, f_dim:].astype(BF16), w_ffn_dn[j].astype(BF16)))
        else:
            n_exp = w_router.shape[-1]
            wr = jnp.pad(w_router[j], ((0, 0), (0, LANE - n_exp)))
            wr_hi = wr.astype(BF16)
            wr = jnp.stack([wr_hi, (wr - wr_hi.astype(F32)).astype(BF16)])
            xc, idx, wts, cnt = _merge_call(*merge_args, router=(wr, n_exp))
            counts = cnt[0, :n_exp].astype(jnp.int32)
            padded = (counts + MOE_ROW_BLOCK - 1) // MOE_ROW_BLOCK * MOE_ROW_BLOCK
            pad_end = jnp.cumsum(padded)
            pad_start = pad_end - padded
            idx = idx.reshape(n_tok, LANE)
            dest = pad_start[idx[:, 0:TOP_K]] + idx[:, TOP_K:2 * TOP_K]
            dest_blocks = dest.reshape(n_tiles, 1, TOP_K * TOKEN_TILE)
            n_blocks = -(-(n_tok * TOP_K + n_exp * (MOE_ROW_BLOCK - 1)) // MOE_ROW_BLOCK)
            starts = jnp.arange(n_blocks) * MOE_ROW_BLOCK
            block_expert = jnp.minimum(jnp.sum(starts[:, None] >= pad_end[None, :], axis=-1),
                                       n_exp - 1).astype(jnp.int32)
            n_used = (pad_end[-1] // MOE_ROW_BLOCK).astype(jnp.int32).reshape(1)
            n_rows_x = n_blocks * MOE_ROW_BLOCK
            zero_from = jnp.concatenate([pad_start + counts, pad_end[-1:]]).astype(jnp.int32)
            zero_cnt = jnp.concatenate([padded - counts, n_rows_x - pad_end[-1:]]).astype(jnp.int32)
            xs = _dispatch_call(zero_from, zero_cnt, dest_blocks, xc, mod, gn, n_rows_x)
            y_rows = _moe_call(block_expert, n_used, xs, w_moe_gu[j].astype(BF16), w_moe_dn[j].astype(BF16))
            if i == depth - 1:
                return _combine_call(dest_blocks, xc, mod, wts, y_rows, norm_final.reshape(1, d), s_len)
            xc = _combine_call(dest_blocks, xc, mod, wts, y_rows)
    return _final_call(xc, norm_final.reshape(1, d), s_len)
```

```python
import functools

import numpy as np
import jax
import jax.numpy as jnp
from jax import lax
from jax.experimental import pallas as pl
from jax.experimental.pallas import tpu as pltpu

GRID_W = 64
HEAD_DIM = 64
NA_HEADS = 4
WIN_H = 8
WIN_W = 16
GQA_Q_HEADS = 8
GQA_KV_HEADS = 2
GQA_GROUP = GQA_Q_HEADS // GQA_KV_HEADS
MLA_HEADS = 4
MLA_Q_RANK = 256
MLA_KV_RANK = 128
MLA_NOPE_DIM = 64
MLA_ROPE_DIM = 32
MLA_V_DIM = 64
ROPE_THETA = 10000.0
TOP_K = 2
NORM_EPS = 1e-6
NA_SCALE = HEAD_DIM ** -0.5
GQA_SCALE = HEAD_DIM ** -0.5
MLA_SCALE = (MLA_NOPE_DIM + MLA_ROPE_DIM) ** -0.5
LOG2_E = 1.4426950408889634

LANE = 128
MXU_WIDTH = 256
ADA_ROWS = 16
TOKEN_TILE = 256
NA_Q_ROWS = 4
NA_K_ROWS = NA_Q_ROWS + WIN_H
KV_CHUNK = 512
MOE_ROW_BLOCK = 512
VMEM_LIMIT = 56 * 1024 * 1024
NEG_BIG = -1e30
SAFE_LOG2_RANGE = 80.0
NORM_SLACK = 1.02
_N_QM, _N_KM = 0, MLA_HEADS
_N_QNA, _N_KNA = 2 * MLA_HEADS, 2 * MLA_HEADS + NA_HEADS

F32 = jnp.float32
BF16 = jnp.bfloat16


def _params(sem, vmem=VMEM_LIMIT):
    return pltpu.CompilerParams(dimension_semantics=sem, vmem_limit_bytes=vmem)


def _resident(shape):
    zeros = (0,) * len(shape)
    return pl.BlockSpec(shape, lambda *_: zeros, pipeline_mode=pl.Buffered(1))


def _rms(x, n):
    ss = jnp.sum(x * x, axis=-1, keepdims=True)
    return x * lax.rsqrt(ss * (1.0 / n) + NORM_EPS)


def _dot(a, b):
    return jnp.dot(a, b, preferred_element_type=F32)


def _dot_nt(a, b):
    return lax.dot_general(a, b, (((1,), (1,)), ((), ())), preferred_element_type=F32)


def _silu(x):
    return x * jax.nn.sigmoid(x)


def _ada_kernel(c_ref, w_ref, b_ref, o_ref):
    s = _silu(c_ref[...])
    o_ref[0] = jnp.dot(s, w_ref[0], precision=lax.Precision.HIGHEST,
                       preferred_element_type=F32) + b_ref[0]


def _ada_call(c_rows, w_ada, b_ada):
    depth, d, n = w_ada.shape
    rows = c_rows.shape[0]
    tn = 1536 if n % 1536 == 0 else n
    return pl.pallas_call(
        _ada_kernel,
        out_shape=jax.ShapeDtypeStruct((depth, rows, n), F32),
        grid=(depth, n // tn),
        in_specs=[
            pl.BlockSpec((rows, d), lambda i, j: (0, 0)),
            pl.BlockSpec((1, d, tn), lambda i, j: (i, 0, j)),
            pl.BlockSpec((1, 1, tn), lambda i, j: (i, 0, j)),
        ],
        out_specs=pl.BlockSpec((1, rows, tn), lambda i, j: (i, 0, j)),
        compiler_params=_params(("arbitrary", "arbitrary")),
        name="ada_mod",
    )(c_rows, w_ada, b_ada.reshape(depth, 1, n))


def _rope(xh, cos, sa, sb, shift):
    return xh * cos + pltpu.roll(xh, LANE - shift, 1) * sa + pltpu.roll(xh, shift, 1) * sb


_W_NA = 3 * NA_HEADS * HEAD_DIM
_W_QG = GQA_Q_HEADS * HEAD_DIM
_W_KVG = 2 * GQA_KV_HEADS * LANE
_W_LORA = MLA_Q_RANK + MLA_KV_RANK + LANE
_C_NA = 0
_C_QG = _C_NA + _W_NA
_C_KVG = _C_QG + _W_QG
_C_LORA = _C_KVG + _W_KVG
_C_GATE = _C_LORA + _W_LORA


def _mixin_kernel(x_ref, mod0_ref, mod1_ref, gmix_ref, w_ref, qn_ref, kn_ref, qln_ref, kvln_ref,
                  wuq_ref, wukv_ref, rg0_ref, rg1_ref, rm0_ref, rm1_ref, *out_refs):
    shared = (gmix_ref, w_ref, qn_ref, kn_ref, qln_ref, kvln_ref, wuq_ref, wukv_ref)
    half = x_ref.shape[0] // 2
    gates0 = _mixin_tile(slice(0, half), x_ref, mod0_ref, *shared, rg0_ref, rm0_ref, *out_refs)
    gates1 = _mixin_tile(slice(half, 2 * half), x_ref, mod1_ref, *shared, rg1_ref, rm1_ref, *out_refs)
    gates0()
    gates1()


def _mixin_tile(rows, x_ref, mod_ref, gmix_ref, w_ref, qn_ref, kn_ref, qln_ref, kvln_ref,
                wuq_ref, wukv_ref, rg_ref, rm_ref,
                qna_ref, kna_ref, vna_ref, qg_ref, kg_ref, vg_ref, qm_ref, km_ref, vm_ref,
                gate_ref, nrm_ref):
    x = x_ref[rows, :]
    d = x.shape[-1]
    tm = x.shape[0]
    mod = mod_ref[0]
    h = _rms(x, d) * gmix_ref[...]
    h = h * (1.0 + mod[1:2]) + mod[0:1]
    hb = h.astype(BF16)

    def proj(lo, width):
        return _dot(hb, w_ref[:, lo:lo + width])

    def block(a, j):
        return a[:, j * LANE:(j + 1) * LANE]

    lane = lax.broadcasted_iota(jnp.int32, (tm, LANE), 1)
    low = lane < HEAD_DIM
    ones_pad = jnp.where(low, 0.0, 1.0)
    norms = []

    def unit_at(pos):
        return jnp.where(lane == pos, 1.0, 0.0)

    cos_g, sa_g, sb_g = rg_ref[0], rg_ref[1], rg_ref[2]
    cos_m, sa_m, sb_m = rm_ref[0], rm_ref[1], rm_ref[2]
    g_half = HEAD_DIM // 2
    m_half = MLA_ROPE_DIM // 2

    p_lora = proj(_C_LORA, _W_LORA)
    cq = (_rms(p_lora[:, 0:MLA_Q_RANK], MLA_Q_RANK) * qln_ref[...]).astype(BF16)
    q_up = _dot(cq, wuq_ref[...])
    ckv = p_lora[:, MLA_Q_RANK:MLA_Q_RANK + MLA_KV_RANK]
    ckv = (_rms(ckv, MLA_KV_RANK) * kvln_ref[...]).astype(BF16)
    kv_up = _dot(ckv, wukv_ref[...])
    k_rope = _rope(p_lora[:, MLA_Q_RANK + MLA_KV_RANK:], cos_m, sa_m, sb_m, m_half)
    for hh in range(MLA_HEADS):
        sl = slice(hh * LANE, (hh + 1) * LANE)
        qh = _rope(block(q_up, hh), cos_m, sa_m, sb_m, m_half) * (MLA_SCALE * LOG2_E)
        qm_ref[rows, sl] = qh.astype(BF16)
        kh = block(kv_up, hh) + k_rope
        norms.append((_N_QM + hh, jnp.sum(qh * qh, axis=-1, keepdims=True)))
        norms.append((_N_KM + hh, jnp.sum(kh * kh, axis=-1, keepdims=True)))
        km_ref[rows, sl] = (kh + unit_at(MLA_NOPE_DIM + MLA_ROPE_DIM)).astype(BF16)
        vm_ref[rows, sl] = (block(kv_up, MLA_HEADS + hh) + ones_pad).astype(BF16)

    p_qg = proj(_C_QG, _W_QG)
    for j in range(GQA_Q_HEADS // 2):
        xp = block(p_qg, j)
        sq = xp * xp
        ss_lo = jnp.sum(jnp.where(low, sq, 0.0), axis=-1, keepdims=True)
        ss_hi = jnp.sum(jnp.where(low, 0.0, sq), axis=-1, keepdims=True)
        inv = jnp.where(low, lax.rsqrt(ss_lo * (1.0 / HEAD_DIM) + NORM_EPS),
                        lax.rsqrt(ss_hi * (1.0 / HEAD_DIM) + NORM_EPS))
        xp = _rope(xp * inv * qn_ref[...], cos_g, sa_g, sb_g, g_half) * (GQA_SCALE * LOG2_E)
        qg_ref[rows, j * LANE:(j + 1) * LANE] = xp.astype(BF16)
    p_kvg = proj(_C_KVG, _W_KVG)
    for hh in range(GQA_KV_HEADS):
        xh = _rope(_rms(block(p_kvg, hh), HEAD_DIM) * kn_ref[...], cos_g, sa_g, sb_g, g_half)
        kg_ref[rows, hh * LANE:(hh + 1) * LANE] = (xh + unit_at(HEAD_DIM)).astype(BF16)
        vh = block(p_kvg, GQA_KV_HEADS + hh) + ones_pad
        vg_ref[rows, hh * LANE:(hh + 1) * LANE] = vh.astype(BF16)

    nw = NA_HEADS * HEAD_DIM
    p_na = proj(_C_NA, _W_NA)
    q_na = p_na[:, 0:nw] * (NA_SCALE * LOG2_E)
    qna_ref[rows, :] = q_na.astype(BF16)
    kna_ref[rows, :] = p_na[:, nw:2 * nw].astype(BF16)
    vna_ref[rows, :] = p_na[:, 2 * nw:3 * nw].astype(BF16)
    for j in range(NA_HEADS // 2):
        for pos, blk in ((_N_QNA, block(q_na, j)), (_N_KNA, block(p_na, NA_HEADS // 2 + j))):
            sq = blk * blk
            norms.append((pos + 2 * j, jnp.sum(jnp.where(low, sq, 0.0), axis=-1, keepdims=True)))
            norms.append((pos + 2 * j + 1, jnp.sum(jnp.where(low, 0.0, sq), axis=-1, keepdims=True)))
    packed = jnp.zeros((tm, LANE), F32)
    for pos, val in norms:
        packed = jnp.where(lane == pos, val, packed)
    nrm_ref[rows, :] = packed

    def gates():
        for j in range(3):
            gate_ref[rows, j * d:(j + 1) * d] = jax.nn.sigmoid(proj(_C_GATE + j * d, d)).astype(BF16)

    return gates


def _mixin_call(xc, mod, gmix, w_cat, qn, kn, qln, kvln, wuq, wukv, rope_g, rope_m):
    b, t, d = xc.shape
    tm = TOKEN_TILE
    nt = t // tm
    n_tiles = b * nt
    assert n_tiles % 2 == 0
    tok = lambda w: pl.BlockSpec((2 * tm, w), lambda i: (i, 0))
    widths = [NA_HEADS * HEAD_DIM] * 3 + [GQA_Q_HEADS * HEAD_DIM, GQA_KV_HEADS * LANE,
                                          GQA_KV_HEADS * LANE, MLA_HEADS * LANE,
                                          MLA_HEADS * LANE, MLA_HEADS * LANE, 3 * d]

    def mod_spec(h):
        return pl.BlockSpec((1, 6, d), lambda i: (
            2 * ((2 * i + h) // nt) + ((2 * i + h) % nt == nt - 1).astype(jnp.int32), 0, 0))

    def rope_spec(h):
        return pl.BlockSpec((3, tm, LANE), lambda i: (0, (2 * i + h) % nt, 0))

    outs = pl.pallas_call(
        _mixin_kernel,
        out_shape=[jax.ShapeDtypeStruct((b * t, w), BF16) for w in widths]
        + [jax.ShapeDtypeStruct((b * t, LANE), F32)],
        grid=(n_tiles // 2,),
        in_specs=[
            tok(d), mod_spec(0), mod_spec(1),
            _resident((1, d)),
            _resident(w_cat.shape),
            _resident((1, LANE)), _resident((1, LANE)),
            _resident((1, MLA_Q_RANK)), _resident((1, MLA_KV_RANK)),
            _resident(wuq.shape), _resident(wukv.shape),
            rope_spec(0), rope_spec(1), rope_spec(0), rope_spec(1),
        ],
        out_specs=[tok(w) for w in widths] + [tok(LANE)],
        compiler_params=_params(("arbitrary",)),
        name="mixer_in",
    )(xc.reshape(b * t, d), mod, mod, gmix, w_cat, qn, kn, qln, kvln, wuq, wukv, rope_g, rope_g, rope_m, rope_m)
    return [o.reshape(b, t, o.shape[-1]) for o in outs]


def _na_kernel(q_ref, k_ref, v_ref, bias_ref, u_ref, o_ref, *, n_groups, n_rows, s_len, c_len, bounded):
    g = pl.program_id(1)
    qn = NA_Q_ROWS * GRID_W
    kn = NA_K_ROWS * GRID_W
    lane = lax.broadcasted_iota(jnp.int32, (qn, LANE), 1)
    low = lane < HEAD_DIM

    def attend_pairs(k_nb, v_nb):
        k_c = k_ref[0, s_len:s_len + c_len, :]
        v_c = v_ref[0, s_len:s_len + c_len, :]
        for pair in range(NA_HEADS // 2):
            sl = slice(pair * LANE, (pair + 1) * LANE)
            qp = q_ref[0, :, sl]
            outs = []
            for sub in range(2):
                head = 2 * pair + sub
                qm = jnp.where(low if sub == 0 else jnp.logical_not(low), qp, jnp.zeros_like(qp))
                s_c = _dot_nt(qm, k_c[:, sl])
                if k_nb is not None:
                    s_nb = _dot_nt(qm, k_nb[:, sl]) + bias_ref[0, head]
                if bounded:
                    p_c = jnp.exp2(s_c - u_ref[head:head + 1, :])
                else:
                    m = jnp.max(s_c, axis=-1, keepdims=True)
                    if k_nb is not None:
                        m = jnp.maximum(m, jnp.max(s_nb, axis=-1, keepdims=True))
                    p_c = jnp.exp2(s_c - m)
                den = jnp.sum(p_c, axis=-1, keepdims=True)
                o = _dot(p_c.astype(BF16), v_c[:, sl])
                if k_nb is not None:
                    p_nb = jnp.exp2(s_nb if bounded else s_nb - m)
                    den = den + jnp.sum(p_nb, axis=-1, keepdims=True)
                    o = o + _dot(p_nb.astype(BF16), v_nb[:, sl])
                outs.append(o / den)
            o_ref[0, :, sl] = jnp.where(low, outs[0], outs[1]).astype(BF16)

    @pl.when(g < n_groups)
    def _():
        first_row = jnp.clip(NA_Q_ROWS * g - WIN_H // 2, 0, n_rows - NA_K_ROWS)
        start = pl.multiple_of(first_row * GRID_W, GRID_W)
        attend_pairs(k_ref[0, pl.ds(start, kn), :], v_ref[0, pl.ds(start, kn), :])

    @pl.when(g == n_groups)
    def _():
        attend_pairs(None, None)


def _na_attention(q, k, v, bias, rpb_l, q_norm2, k_norm2, s_len, c_len):
    qk = jnp.max(jnp.sqrt(q_norm2 * k_norm2), axis=0) * NORM_SLACK
    b_hi = jnp.maximum(jnp.max(rpb_l, axis=(1, 2)) * LOG2_E, 0.0)
    b_lo = jnp.minimum(jnp.min(rpb_l, axis=(1, 2)) * LOG2_E, 0.0)
    u = qk + b_hi
    safe = jnp.max(2.0 * qk + b_hi - b_lo) < SAFE_LOG2_RANGE
    u_rows = jnp.broadcast_to(u[:, None], (NA_HEADS, c_len)).astype(F32)
    shift = jnp.where(safe, u, 0.0)
    table = jnp.where(bias > 0.5 * NEG_BIG, bias - shift[None, :, None, None], NEG_BIG)
    return lax.cond(safe,
                    lambda ops: _na_call(*ops, s_len, c_len, True),
                    lambda ops: _na_call(*ops, s_len, c_len, False),
                    (q, k, v, table, u_rows))


def _na_call(q, k, v, bias, u_rows, s_len, c_len, bounded):
    b, t, w = q.shape
    qn = NA_Q_ROWS * GRID_W
    kn = NA_K_ROWS * GRID_W
    n_rows = s_len // GRID_W
    n_groups = s_len // qn
    assert c_len == qn and n_rows >= NA_K_ROWS

    def variant(g):
        return jnp.where(g == 0, 0, jnp.where(g >= n_groups - 1, 2, 1))

    return pl.pallas_call(
        functools.partial(_na_kernel, n_groups=n_groups, n_rows=n_rows, s_len=s_len, c_len=c_len,
                          bounded=bounded),
        out_shape=jax.ShapeDtypeStruct((b, t, w), BF16),
        grid=(b, n_groups + 1),
        in_specs=[
            pl.BlockSpec((1, qn, w), lambda bi, g: (bi, g, 0)),
            pl.BlockSpec((1, t, w), lambda bi, g: (bi, 0, 0)),
            pl.BlockSpec((1, t, w), lambda bi, g: (bi, 0, 0)),
            pl.BlockSpec((1, NA_HEADS, qn, kn), lambda bi, g: (variant(g), 0, 0, 0)),
            pl.BlockSpec(u_rows.shape, lambda bi, g: (0, 0)),
        ],
        out_specs=pl.BlockSpec((1, qn, w), lambda bi, g: (bi, g, 0)),
        compiler_params=_params(("arbitrary", "arbitrary")),
        name="na_attn" if bounded else "na_attn_max",
    )(q, k, v, bias, u_rows)


def _na_bias_tables(rpb, n_rows):
    qi = np.arange(NA_Q_ROWS)[:, None]
    kj = np.arange(NA_K_ROWS)[None, :]
    qc = np.arange(GRID_W)[:, None]
    kc = np.arange(GRID_W)[None, :]
    col0 = np.clip(qc - WIN_W // 2, 0, GRID_W - WIN_W)
    col_ok = (kc >= col0) & (kc < col0 + WIN_W)
    dc = np.clip(kc - qc, -(WIN_W - 1), WIN_W - 1) + WIN_W - 1
    col_sel = np.eye(2 * WIN_W - 1, dtype=np.float32)[dc]
    row_sel, valid = [], []
    for r0, u0 in ((0, 0), (WIN_H // 2, 0), (n_rows - NA_Q_ROWS, n_rows - NA_K_ROWS)):
        qr = r0 + qi
        kr = u0 + kj
        row0 = np.clip(qr - WIN_H // 2, 0, n_rows - WIN_H)
        row_ok = (kr >= row0) & (kr < row0 + WIN_H)
        dr = np.clip(kr - qr, -(WIN_H - 1), WIN_H - 1) + WIN_H - 1
        row_sel.append(np.eye(2 * WIN_H - 1, dtype=np.float32)[dr])
        valid.append(row_ok[:, None, :, None] & col_ok[None, :, None, :])
    row_sel = jnp.asarray(np.stack(row_sel))
    valid = np.stack(valid).reshape(3, 1, NA_Q_ROWS * GRID_W, NA_K_ROWS * GRID_W)
    picked = jnp.einsum('vija,lhab,qkb->lvhiqjk', row_sel, rpb.astype(F32) * LOG2_E, jnp.asarray(col_sel),
                        precision=lax.Precision.HIGHEST)
    picked = picked.reshape(rpb.shape[0], 3, NA_HEADS, NA_Q_ROWS * GRID_W, NA_K_ROWS * GRID_W)
    return jnp.where(valid[None], picked, NEG_BIG)


def _row_max(s):
    parts = [s[:, j * LANE:(j + 1) * LANE] for j in range(s.shape[1] // LANE)]
    while len(parts) > 1:
        parts = [jnp.maximum(parts[j], parts[j + 1]) for j in range(0, len(parts) - 1, 2)] + parts[len(parts) & ~1:]
    return jnp.max(parts[0], axis=-1, keepdims=True)


def _flash_kernel(q_ref, k_ref, v_ref, u_ref, o_ref, *, n_kv, group, tq, q_width, s_len, c_len,
                  with_latent, online):
    low = lax.broadcasted_iota(jnp.int32, (tq, LANE), 1) < HEAD_DIM
    heads = range(n_kv)
    lanes = [slice(h * LANE, (h + 1) * LANE) for h in heads]

    def load_q(head):
        if q_width == LANE:
            qh = q_ref[0, :, head * LANE:(head + 1) * LANE]
        else:
            pair = q_ref[0, :, (head // 2) * LANE:(head // 2 + 1) * LANE]
            if head % 2:
                pair = pltpu.roll(pair.astype(F32), HEAD_DIM, 1).astype(BF16)
            qh = jnp.where(low, pair, jnp.zeros_like(pair))
        return qh if online else qh + u_ref[0, head:head + 1, :]

    qs = []
    for h in heads:
        parts = [load_q(h * group + g) for g in range(group)]
        qs.append(parts[0] if group == 1 else jnp.concatenate(parts, axis=0))

    if online:
        carry = []
        for h in heads:
            s = _dot_nt(qs[h], k_ref[0, s_len:s_len + c_len, lanes[h]])
            m = _row_max(s)
            carry += [m, _dot(jnp.exp2(s - m).astype(BF16), v_ref[0, s_len:s_len + c_len, lanes[h]])]
        if with_latent:
            def body(i, carry):
                st = pl.multiple_of(i * KV_CHUNK, KV_CHUNK)
                new = []
                for h in heads:
                    m_old, acc_old = carry[2 * h], carry[2 * h + 1]
                    sc = _dot_nt(qs[h], k_ref[0, pl.ds(st, KV_CHUNK), lanes[h]])
                    m_new = jnp.maximum(m_old, _row_max(sc))
                    p = jnp.exp2(sc - m_new).astype(BF16)
                    pv = _dot(p, v_ref[0, pl.ds(st, KV_CHUNK), lanes[h]])
                    new += [m_new, jnp.exp2(m_old - m_new) * acc_old + pv]
                return tuple(new)
            carry = lax.fori_loop(0, s_len // KV_CHUNK, body, tuple(carry), unroll=2)
        accs = [carry[2 * h + 1] for h in heads]
    else:
        chunks = [(s_len, c_len)]
        if with_latent:
            chunks += [(i * KV_CHUNK, KV_CHUNK) for i in range(s_len // KV_CHUNK)]
        accs = [None] * n_kv
        for st, size in chunks:
            for h in heads:
                p = jnp.exp2(_dot_nt(qs[h], k_ref[0, st:st + size, lanes[h]])).astype(BF16)
                pv = _dot(p, v_ref[0, st:st + size, lanes[h]])
                accs[h] = pv if accs[h] is None else accs[h] + pv
    outs = []
    for h in heads:
        acc = accs[h]
        o = acc / pltpu.roll(acc, HEAD_DIM, 1)
        for g in range(group):
            outs.append(o[g * tq:(g + 1) * tq])
    for pair in range(len(outs) // 2):
        packed = jnp.where(low, outs[2 * pair], pltpu.roll(outs[2 * pair + 1], HEAD_DIM, 1))
        o_ref[0, :, pair * LANE:(pair + 1) * LANE] = packed.astype(BF16)


def _flash_call(q, k, v, shift, *, n_kv, group, tq, q_width, s_len, c_len, online, name):
    b, t, _ = q.shape
    n_heads = n_kv * group
    tq_ctx = min(tq, c_len)
    assert s_len % tq == 0 and c_len % tq_ctx == 0 and s_len % tq_ctx == 0 and s_len % KV_CHUNK == 0
    kv_spec = pl.BlockSpec((1, t, n_kv * LANE), lambda bi, ti: (bi, 0, 0))
    u_spec = pl.BlockSpec((1, n_heads, LANE), lambda bi, ti: (bi, 0, 0))
    common = dict(n_kv=n_kv, group=group, q_width=q_width, s_len=s_len, c_len=c_len, online=online)
    width = n_heads * HEAD_DIM
    tag = "_online" if online else ""
    y_lat = pl.pallas_call(
        functools.partial(_flash_kernel, tq=tq, with_latent=True, **common),
        out_shape=jax.ShapeDtypeStruct((b, s_len, width), BF16),
        grid=(b, s_len // tq),
        in_specs=[pl.BlockSpec((1, tq, n_heads * q_width), lambda bi, ti: (bi, ti, 0)), kv_spec, kv_spec, u_spec],
        out_specs=pl.BlockSpec((1, tq, width), lambda bi, ti: (bi, ti, 0)),
        compiler_params=_params(("arbitrary", "arbitrary")),
        name=name + tag,
    )(q, k, v, shift)
    first = s_len // tq_ctx
    y_ctx = pl.pallas_call(
        functools.partial(_flash_kernel, tq=tq_ctx, with_latent=False, **common),
        out_shape=jax.ShapeDtypeStruct((b, c_len, width), BF16),
        grid=(b, c_len // tq_ctx),
        in_specs=[pl.BlockSpec((1, tq_ctx, n_heads * q_width), lambda bi, ti: (bi, first + ti, 0)), kv_spec, kv_spec,
                  u_spec],
        out_specs=pl.BlockSpec((1, tq_ctx, width), lambda bi, ti: (bi, ti, 0)),
        compiler_params=_params(("arbitrary", "arbitrary")),
        name=name + "_ctx" + tag,
    )(q, k, v, shift)
    return y_lat, y_ctx


def _bounded_attention(q, k, v, bound, unit_lane, **kw):
    bound = bound * NORM_SLACK
    lane = jnp.arange(LANE) == unit_lane
    shift = jnp.where(lane[None, None, :], -bound[:, :, None], 0.0).astype(BF16)
    safe = jnp.max(bound) * 2.0 < SAFE_LOG2_RANGE
    return lax.cond(safe,
                    lambda ops: _flash_call(*ops, online=False, **kw),
                    lambda ops: _flash_call(*ops, online=True, **kw),
                    (q, k, v, shift))


def _mod_spec(d, nt):
    return pl.BlockSpec((1, 6, d), lambda bi, ti: (2 * bi + (ti == nt - 1).astype(jnp.int32), 0, 0))


def _merged_residual(x_ref, mod_ref, yna_ref, yg_ref, ygc_ref, ym_ref, ymc_ref, gate_ref, wna_ref, wg_ref,
                     wm_ref, wout_ref, n_lat_tiles):
    d = x_ref.shape[-1]
    is_ctx = pl.program_id(1) >= n_lat_tiles
    y_g = jnp.where(is_ctx, ygc_ref[0], yg_ref[0])
    y_m = jnp.where(is_ctx, ymc_ref[0], ym_ref[0])
    m = gate_ref[0, :, 0:d].astype(F32) * _dot(yna_ref[0], wna_ref[...])
    m = m + gate_ref[0, :, d:2 * d].astype(F32) * _dot(y_g, wg_ref[...])
    m = m + gate_ref[0, :, 2 * d:3 * d].astype(F32) * _dot(y_m, wm_ref[...])
    r = _dot(m.astype(BF16), wout_ref[...])
    return x_ref[0] + mod_ref[0][2:3] * r


def _norm2(x, mod, gain):
    h = _rms(x, x.shape[-1]) * gain
    return h * (1.0 + mod[4:5]) + mod[3:4]


def _merge_ffn_kernel(*refs, n_lat_tiles):
    mix_refs, (gn_ref, wg_ref, wu_ref, wd_ref, o_ref) = refs[:12], refs[12:]
    x = _merged_residual(*mix_refs, n_lat_tiles)
    mod = mix_refs[1][0]
    hb = _norm2(x, mod, gn_ref[...]).astype(BF16)
    act = (_silu(_dot(hb, wg_ref[...])) * _dot(hb, wu_ref[...])).astype(BF16)
    o_ref[0] = x + mod[5:6] * _dot(act, wd_ref[...])


def _merge_router_kernel(*refs, n_lat_tiles, n_experts):
    mix_refs, (gn_ref, wr_ref, o_ref, idx_ref, wt_ref, cnt_ref, carry_ref) = refs[:12], refs[12:]

    @pl.when((pl.program_id(0) == 0) & (pl.program_id(1) == 0))
    def _():
        carry_ref[...] = jnp.zeros_like(carry_ref)

    x = _merged_residual(*mix_refs, n_lat_tiles)
    o_ref[0] = x
    tm = x.shape[0]
    h = _norm2(x, mix_refs[1][0], gn_ref[...])
    h_hi = h.astype(BF16)
    h_lo = (h - h_hi.astype(F32)).astype(BF16)
    logits = _dot(h_hi, wr_ref[0]) + (_dot(h_lo, wr_ref[0]) + _dot(h_hi, wr_ref[1]))
    lane = lax.broadcasted_iota(jnp.int32, (tm, LANE), 1).astype(F32)
    logits = jnp.where(lane < n_experts, logits, -jnp.inf)
    m1 = jnp.max(logits, axis=-1, keepdims=True)
    i1 = jnp.min(jnp.where(logits == m1, lane, float(LANE)), axis=-1, keepdims=True)
    rest = jnp.where(lane == i1, -jnp.inf, logits)
    m2 = jnp.max(rest, axis=-1, keepdims=True)
    i2 = jnp.min(jnp.where(rest == m2, lane, float(LANE)), axis=-1, keepdims=True)
    e2 = jnp.exp(m2 - m1)
    w1 = 1.0 / (1.0 + e2)
    w2 = e2 / (1.0 + e2)
    hot1 = lane == i1
    hot2 = lane == i2
    hot = jnp.where(hot1 | hot2, 1.0, 0.0)
    rows = lax.broadcasted_iota(jnp.int32, (tm, tm), 0)
    cols = lax.broadcasted_iota(jnp.int32, (tm, tm), 1)
    below = jnp.where(rows > cols, 1.0, 0.0).astype(BF16)
    before = _dot(below, hot.astype(BF16)) + carry_ref[0:1, :]
    r1 = jnp.sum(jnp.where(hot1, before, 0.0), axis=-1, keepdims=True)
    r2 = jnp.sum(jnp.where(hot2, before, 0.0), axis=-1, keepdims=True)
    total = carry_ref[0:1, :] + jnp.sum(hot, axis=0, keepdims=True)
    carry_ref[...] = jnp.broadcast_to(total, carry_ref.shape)
    cnt_ref[...] = jnp.broadcast_to(total, cnt_ref.shape)
    packed = jnp.where(lane == 0, i1, jnp.where(lane == 1, i2, jnp.where(lane == 2, r1, jnp.where(lane == 3, r2, 0.0))))
    idx_ref[0] = packed.astype(jnp.int32)
    wt_ref[0] = jnp.where(lane == 0, w1, jnp.where(lane == 1, w2, 0.0))


def _merge_call(xc, mod, y_na, y_g, y_m, gate, w_na, w_g, w_m, w_out, gn, *, ffn=None, router=None):
    b, t, d = xc.shape
    tm = TOKEN_TILE
    nt = t // tm
    n_lat = y_g[0].shape[1] // tm
    tok = lambda w: pl.BlockSpec((1, tm, w), lambda bi, ti: (bi, ti, 0))
    lat = lambda w: pl.BlockSpec((1, tm, w), lambda bi, ti: (bi, jnp.minimum(ti, n_lat - 1), 0))
    ctx = lambda w: pl.BlockSpec((1, tm, w), lambda bi, ti: (bi, jnp.maximum(ti - n_lat, 0), 0))
    wg_, wm_ = y_g[0].shape[-1], y_m[0].shape[-1]
    mix_specs = [tok(d), _mod_spec(d, nt), tok(y_na.shape[-1]), lat(wg_), ctx(wg_), lat(wm_), ctx(wm_),
                 tok(3 * d), _resident(w_na.shape), _resident(w_g.shape), _resident(w_m.shape),
                 _resident(w_out.shape), _resident((1, d))]
    mix_args = (xc, mod, y_na, y_g[0], y_g[1], y_m[0], y_m[1], gate, w_na, w_g, w_m, w_out, gn)
    x_shape = jax.ShapeDtypeStruct(xc.shape, F32)
    if ffn is not None:
        return pl.pallas_call(
            functools.partial(_merge_ffn_kernel, n_lat_tiles=n_lat),
            out_shape=x_shape,
            grid=(b, nt),
            in_specs=mix_specs + [_resident(w.shape) for w in ffn],
            out_specs=tok(d),
            input_output_aliases={0: 0},
            compiler_params=_params(("arbitrary", "arbitrary")),
            name="merge_ffn",
        )(*mix_args, *ffn)
    w_router_pad, n_experts = router
    return pl.pallas_call(
        functools.partial(_merge_router_kernel, n_lat_tiles=n_lat, n_experts=n_experts),
        out_shape=[x_shape, jax.ShapeDtypeStruct((b, t, LANE), jnp.int32),
                   jax.ShapeDtypeStruct((b, t, LANE), F32), jax.ShapeDtypeStruct((8, LANE), F32)],
        grid=(b, nt),
        in_specs=mix_specs + [_resident(w_router_pad.shape)],
        out_specs=[tok(d), tok(LANE), tok(LANE), pl.BlockSpec((8, LANE), lambda bi, ti: (0, 0))],
        scratch_shapes=[pltpu.VMEM((8, LANE), F32)],
        input_output_aliases={0: 0},
        compiler_params=_params(("arbitrary", "arbitrary")),
        name="merge_router",
    )(*mix_args, w_router_pad)


def _row_copy(src, src_row, dst, dst_row, sem):
    return pltpu.make_async_copy(src.at[pl.ds(src_row, 1)], dst.at[pl.ds(dst_row, 1)], sem)


def _dispatch_kernel(pad_from_ref, pad_cnt_ref, dest_ref, x_ref, mod_ref, gn_ref, xs_ref, h_buf, zrow, sem,
                     zsem, *, n_tiles, n_exp):
    i = pl.program_id(0)
    slot = i % 2
    tm = x_ref.shape[1]

    @pl.when(i == 0)
    def _():
        zrow[...] = jnp.zeros_like(zrow)
        for e in range(n_exp):
            def zero_issue(r, carry, e=e):
                _row_copy(zrow, 0, xs_ref, pad_from_ref[e] + r, zsem).start()
                return carry
            lax.fori_loop(0, pad_cnt_ref[e], zero_issue, 0)
        for e in range(n_exp):
            def zero_drain(r, carry):
                _row_copy(zrow, 0, xs_ref, 0, zsem).wait()
                return carry
            lax.fori_loop(0, pad_cnt_ref[e], zero_drain, 0)

    @pl.when(i < n_tiles)
    def _():
        h_buf[slot] = _norm2(x_ref[0], mod_ref[0], gn_ref[...])

        def issue(r, carry):
            for k in range(TOP_K):
                _row_copy(h_buf.at[slot], r, xs_ref, dest_ref[0, 0, TOP_K * r + k], sem.at[slot]).start()
            return carry

        lax.fori_loop(0, tm, issue, 0, unroll=8)

    @pl.when(i > 0)
    def _():
        def drain(r, carry):
            for k in range(TOP_K):
                _row_copy(h_buf.at[1 - slot], 0, xs_ref, 0, sem.at[1 - slot]).wait()
            return carry

        lax.fori_loop(0, tm, drain, 0, unroll=8)


def _dispatch_call(pad_from, pad_cnt, dest_blocks, xc, mod, gn, n_rows):
    b, t, d = xc.shape
    tm = TOKEN_TILE
    nt = t // tm
    n_tiles = b * nt

    def tile(i):
        j = jnp.minimum(i, n_tiles - 1)
        return j // nt, j % nt

    grid_spec = pltpu.PrefetchScalarGridSpec(
        num_scalar_prefetch=2,
        grid=(n_tiles + 1,),
        in_specs=[
            pl.BlockSpec((1, 1, dest_blocks.shape[-1]), lambda i, pf, pc: (jnp.minimum(i, n_tiles - 1), 0, 0),
                         memory_space=pltpu.SMEM),
            pl.BlockSpec((1, tm, d), lambda i, pf, pc: (*tile(i), 0)),
            pl.BlockSpec((1, 6, d), lambda i, pf, pc: (
                2 * tile(i)[0] + (tile(i)[1] == nt - 1).astype(jnp.int32), 0, 0)),
            pl.BlockSpec((1, d), lambda i, pf, pc: (0, 0)),
        ],
        out_specs=pl.BlockSpec(memory_space=pl.ANY),
        scratch_shapes=[pltpu.VMEM((2, tm, d), F32), pltpu.VMEM((8, d), F32),
                        pltpu.SemaphoreType.DMA((2,)), pltpu.SemaphoreType.DMA],
    )
    return pl.pallas_call(
        functools.partial(_dispatch_kernel, n_tiles=n_tiles, n_exp=pad_from.shape[0]),
        out_shape=jax.ShapeDtypeStruct((n_rows, d), F32),
        grid_spec=grid_spec,
        compiler_params=_params(("arbitrary",)),
        name="moe_dispatch",
    )(pad_from, pad_cnt, dest_blocks, xc, mod, gn)


def _moe_kernel(be_ref, nu_ref, xs_ref, wgu_ref, wd_ref, y_ref, *, f_chunk):
    f_dim = wd_ref.shape[1]

    @pl.when(pl.program_id(0) < nu_ref[0])
    def _():
        xb = xs_ref[...].astype(BF16)
        acc = None
        for c in range(0, f_dim, f_chunk):
            gate = _dot(xb, wgu_ref[0, :, c:c + f_chunk])
            up = _dot(xb, wgu_ref[0, :, f_dim + c:f_dim + c + f_chunk])
            part = _dot((_silu(gate) * up).astype(BF16), wd_ref[0, c:c + f_chunk, :])
            acc = part if acc is None else acc + part
        y_ref[...] = acc

    @pl.when(pl.program_id(0) >= nu_ref[0])
    def _():
        y_ref[...] = jnp.zeros_like(y_ref)


def _moe_call(block_expert, n_used, xs, w_gu, w_dn):
    rows, d = xs.shape
    n_exp, _, two_f = w_gu.shape
    f_dim = two_f // 2
    f_chunk = MXU_WIDTH if f_dim % MXU_WIDTH == 0 else f_dim
    tmr = MOE_ROW_BLOCK
    grid_spec = pltpu.PrefetchScalarGridSpec(
        num_scalar_prefetch=2,
        grid=(rows // tmr,),
        in_specs=[
            pl.BlockSpec((tmr, d), lambda i, be, nu: (jnp.minimum(i, nu[0] - 1), 0)),
            pl.BlockSpec((1, d, two_f), lambda i, be, nu: (be[i], 0, 0)),
            pl.BlockSpec((1, f_dim, d), lambda i, be, nu: (be[i], 0, 0)),
        ],
        out_specs=pl.BlockSpec((tmr, d), lambda i, be, nu: (i, 0)),
    )
    return pl.pallas_call(
        functools.partial(_moe_kernel, f_chunk=f_chunk),
        out_shape=jax.ShapeDtypeStruct((rows, d), F32),
        grid_spec=grid_spec,
        compiler_params=_params(("arbitrary",)),
        name="moe_experts",
    )(block_expert, n_used, xs, w_gu, w_dn)


def _combine_kernel(dest_ref, x_ref, mod_ref, wt_ref, gain_ref, y_ref, o_ref, buf1, buf2, sem, *, n_tiles,
                    final):
    i = pl.program_id(0)
    slot = i % 2
    tm = x_ref.shape[1]

    @pl.when(i < n_tiles)
    def _():
        def issue(r, carry):
            _row_copy(y_ref, dest_ref[0, 0, TOP_K * r], buf1.at[slot], r, sem.at[slot]).start()
            _row_copy(y_ref, dest_ref[0, 0, TOP_K * r + 1], buf2.at[slot], r, sem.at[slot]).start()
            return carry

        lax.fori_loop(0, tm, issue, 0, unroll=8)

    @pl.when(i > 0)
    def _():
        def drain(r, carry):
            _row_copy(y_ref, 0, buf1.at[1 - slot], 0, sem.at[1 - slot]).wait()
            _row_copy(y_ref, 0, buf2.at[1 - slot], 0, sem.at[1 - slot]).wait()
            return carry

        lax.fori_loop(0, tm, drain, 0, unroll=8)
        wt = wt_ref[0]
        mix = wt[:, 0:1] * buf1[1 - slot] + wt[:, 1:2] * buf2[1 - slot]
        x = x_ref[0] + mod_ref[0][5:6] * mix
        o_ref[0] = _rms(x, x.shape[-1]) * gain_ref[...] if final else x


def _combine_call(dest_blocks, xc, mod, wts, y_rows, final_gain=None, s_len=None):
    b, t, d = xc.shape
    tm = TOKEN_TILE
    nt = t // tm
    final = final_gain is not None
    used = s_len // tm if final else nt
    n_tiles = b * used

    def issued(i):
        j = jnp.minimum(i, n_tiles - 1)
        return j // used, j % used

    def finished(i):
        j = jnp.maximum(i - 1, 0)
        return j // used, j % used

    tok = lambda w: pl.BlockSpec((1, tm, w), lambda i: (*finished(i), 0))
    gain = final_gain if final else jnp.ones((1, d), F32)
    return pl.pallas_call(
        functools.partial(_combine_kernel, n_tiles=n_tiles, final=final),
        out_shape=jax.ShapeDtypeStruct((b, used * tm, d), F32),
        grid=(n_tiles + 1,),
        in_specs=[
            pl.BlockSpec((1, 1, dest_blocks.shape[-1]), lambda i: (issued(i)[0] * nt + issued(i)[1], 0, 0),
                         memory_space=pltpu.SMEM),
            tok(d),
            pl.BlockSpec((1, 6, d), lambda i: (
                2 * finished(i)[0] + (finished(i)[1] == nt - 1).astype(jnp.int32), 0, 0)),
            tok(LANE), pl.BlockSpec((1, d), lambda i: (0, 0)),
            pl.BlockSpec(memory_space=pl.ANY),
        ],
        out_specs=tok(d),
        scratch_shapes=[pltpu.VMEM((2, tm, d), F32), pltpu.VMEM((2, tm, d), F32),
                        pltpu.SemaphoreType.DMA((2,))],
        input_output_aliases={} if final else {1: 0},
        compiler_params=_params(("arbitrary",)),
        name="moe_combine_final" if final else "moe_combine",
    )(dest_blocks, xc, mod, wts, gain, y_rows)


def _final_kernel(x_ref, g_ref, o_ref):
    x = x_ref[0]
    o_ref[0] = _rms(x, x.shape[-1]) * g_ref[...]


def _final_call(xc, gain, s_len):
    b, t, d = xc.shape
    tm = TOKEN_TILE
    tok = pl.BlockSpec((1, tm, d), lambda bi, ti: (bi, ti, 0))
    return pl.pallas_call(
        _final_kernel,
        out_shape=jax.ShapeDtypeStruct((b, s_len, d), F32),
        grid=(b, s_len // tm),
        in_specs=[tok, _resident((1, d))],
        out_specs=tok,
        compiler_params=_params(("arbitrary", "arbitrary")),
        name="final_norm",
    )(xc, gain)


def _pad_heads(w, n_heads, width):
    k = w.shape[0]
    w = w.reshape(k, n_heads, width)
    return jnp.pad(w, ((0, 0), (0, 0), (0, LANE - width))).reshape(k, n_heads * LANE)


def _layer_weights(w_in, w_uq, w_ukv):
    d = w_in.shape[0]
    nw = NA_HEADS * HEAD_DIM
    gk = GQA_KV_HEADS * HEAD_DIM
    o = 0
    k_na = w_in[:, o:o + nw]; o += nw
    v_na = w_in[:, o:o + nw]; o += nw
    k_g = w_in[:, o:o + gk]; o += gk
    v_g = w_in[:, o:o + gk]; o += gk
    c_kv = w_in[:, o:o + MLA_KV_RANK]; o += MLA_KV_RANK
    k_r = w_in[:, o:o + MLA_ROPE_DIM]; o += MLA_ROPE_DIM
    q_na = w_in[:, o:o + nw]; o += nw
    q_g = w_in[:, o:o + GQA_Q_HEADS * HEAD_DIM]; o += GQA_Q_HEADS * HEAD_DIM
    c_q = w_in[:, o:o + MLA_Q_RANK]; o += MLA_Q_RANK
    gate = w_in[:, o:]
    k_r_pad = jnp.pad(k_r, ((0, 0), (MLA_NOPE_DIM, LANE - MLA_NOPE_DIM - MLA_ROPE_DIM)))
    w_cat = jnp.concatenate([
        q_na, k_na, v_na, q_g, _pad_heads(k_g, GQA_KV_HEADS, HEAD_DIM), _pad_heads(v_g, GQA_KV_HEADS, HEAD_DIM), c_q, c_kv, k_r_pad, gate],
        axis=1).astype(BF16)
    assert w_cat.shape[1] == _C_GATE + 3 * d
    wuq = _pad_heads(w_uq, MLA_HEADS, MLA_NOPE_DIM + MLA_ROPE_DIM).astype(BF16)
    kv = w_ukv.reshape(MLA_KV_RANK, MLA_HEADS, MLA_NOPE_DIM + MLA_V_DIM)
    wuk = _pad_heads(kv[:, :, :MLA_NOPE_DIM].reshape(MLA_KV_RANK, -1), MLA_HEADS, MLA_NOPE_DIM)
    wuv = _pad_heads(kv[:, :, MLA_NOPE_DIM:].reshape(MLA_KV_RANK, -1), MLA_HEADS, MLA_V_DIM)
    return w_cat, wuq, jnp.concatenate([wuk, wuv], axis=1).astype(BF16)


def _rope_tables(s_len, c_len, rot_dim, lane_off):
    half = rot_dim // 2
    n_freq = half // 2
    pos = jnp.arange(s_len)
    rows = (pos // GRID_W).astype(F32)
    cols = (pos % GRID_W).astype(F32)
    inv = jnp.power(ROPE_THETA, -jnp.arange(n_freq, dtype=F32) / n_freq)
    ang = jnp.concatenate([rows[:, None] * inv, cols[:, None] * inv], axis=-1)
    cos, sin = jnp.cos(ang), jnp.sin(ang)
    zeros = jnp.zeros((s_len, half), F32)
    right = LANE - lane_off - rot_dim
    pad = lambda a, b: jnp.pad(jnp.concatenate([a, b], axis=-1), ((0, 0), (lane_off, right)))
    cos_t = pad(cos, cos) + jnp.pad(jnp.ones((s_len, lane_off), F32), ((0, 0), (0, LANE - lane_off)))
    sa_t = pad(-sin, zeros)
    sb_t = pad(zeros, sin)
    ident = jnp.pad(jnp.ones((c_len, lane_off + rot_dim), F32), ((0, 0), (0, right)))
    zc = jnp.zeros((c_len, LANE), F32)
    tables = jnp.stack([jnp.concatenate([cos_t, ident]), jnp.concatenate([sa_t, zc]),
                        jnp.concatenate([sb_t, zc])])
    if lane_off == 0 and LANE % rot_dim == 0:
        tables = jnp.tile(tables[..., :rot_dim], (1, 1, LANE // rot_dim))
    return tables


def _tile_lane(v):
    return jnp.tile(v, LANE // v.shape[0]).reshape(1, LANE)


def kernel(x, c, ctx, c_ctx, w_ada, b_ada, norm_mix, norm_ffn, w_in, q_norm_gqa, k_norm_gqa,
           q_lora_norm, kv_lora_norm, w_uq, w_ukv, rpb, w_o_na, w_o_gqa, w_o_mla, w_out,
           w_ffn_gu, w_ffn_dn, w_router, w_moe_gu, w_moe_dn, norm_final):
    b, s_len, d = x.shape
    c_len = ctx.shape[1]
    t_len = s_len + c_len
    depth = w_ada.shape[0]
    n_rows = s_len // GRID_W
    n_tok = b * t_len
    n_tiles = n_tok // TOKEN_TILE

    assert b + 1 <= ADA_ROWS
    c_rows = jnp.zeros((ADA_ROWS, d), F32).at[:b].set(c).at[b].set(c_ctx)
    mods = _ada_call(c_rows, w_ada, b_ada)
    rope_g = _rope_tables(s_len, c_len, HEAD_DIM, 0)
    rope_m = _rope_tables(s_len, c_len, MLA_ROPE_DIM, MLA_NOPE_DIM)
    na_bias = _na_bias_tables(rpb, n_rows)

    xc = jnp.concatenate([x, ctx], axis=1)
    for i in range(depth):
        lat = mods[i, :b].reshape(b, 1, 6, d)
        cm = jnp.broadcast_to(mods[i, b].reshape(1, 1, 6, d), (b, 1, 6, d))
        mod = jnp.concatenate([lat, cm], axis=1).reshape(2 * b, 6, d)

        w_cat, wuq, wukv = _layer_weights(w_in[i], w_uq[i], w_ukv[i])
        q_na, k_na, v_na, q_g, k_g, v_g, q_m, k_m, v_m, gate, norms = _mixin_call(
            xc, mod, norm_mix[i].reshape(1, d), w_cat, _tile_lane(q_norm_gqa[i]), _tile_lane(k_norm_gqa[i]),
            q_lora_norm[i].reshape(1, -1), kv_lora_norm[i].reshape(1, -1), wuq, wukv, rope_g, rope_m)

        nmax = jnp.max(norms, axis=1)
        y_na = _na_attention(q_na, k_na, v_na, na_bias[i], rpb[i], nmax[:, _N_QNA:_N_QNA + NA_HEADS],
                             nmax[:, _N_KNA:_N_KNA + NA_HEADS], s_len, c_len)
        u_g = (HEAD_DIM * GQA_SCALE * LOG2_E) * jnp.max(jnp.abs(q_norm_gqa[i])) * jnp.max(jnp.abs(k_norm_gqa[i]))
        y_g = _bounded_attention(
            q_g, k_g, v_g, jnp.full((b, GQA_Q_HEADS), u_g, F32), HEAD_DIM,
            n_kv=GQA_KV_HEADS, group=GQA_GROUP, tq=512, q_width=HEAD_DIM, s_len=s_len, c_len=c_len,
            name="gqa_attn")
        u_m = jnp.sqrt(nmax[:, _N_QM:_N_QM + MLA_HEADS] * nmax[:, _N_KM:_N_KM + MLA_HEADS])
        y_m = _bounded_attention(
            q_m, k_m, v_m, u_m, MLA_NOPE_DIM + MLA_ROPE_DIM, n_kv=MLA_HEADS, group=1, tq=512, q_width=LANE,
            s_len=s_len, c_len=c_len, name="mla_attn")
        j = i // 2
        gn = norm_ffn[i].reshape(1, d)
        merge_args = (xc, mod, y_na, y_g, y_m, gate, w_o_na[i].astype(BF16), w_o_gqa[i].astype(BF16),
                      w_o_mla[i].astype(BF16), w_out[i].astype(BF16), gn)
        if i % 2 == 0:
            f_dim = w_ffn_dn.shape[1]
            xc = _merge_call(*merge_args, ffn=(w_ffn_gu[j][:, :f_dim].astype(BF16),
                                               w_ffn_gu[j][:, f_dim:].astype(BF16), w_ffn_dn[j].astype(BF16)))
        else:
            n_exp = w_router.shape[-1]
            wr = jnp.pad(w_router[j], ((0, 0), (0, LANE - n_exp)))
            wr_hi = wr.astype(BF16)
            wr = jnp.stack([wr_hi, (wr - wr_hi.astype(F32)).astype(BF16)])
            xc, idx, wts, cnt = _merge_call(*merge_args, router=(wr, n_exp))
            counts = cnt[0, :n_exp].astype(jnp.int32)
            padded = (counts + MOE_ROW_BLOCK - 1) // MOE_ROW_BLOCK * MOE_ROW_BLOCK
            pad_end = jnp.cumsum(padded)
            pad_start = pad_end - padded
            idx = idx.reshape(n_tok, LANE)
            dest = pad_start[idx[:, 0:TOP_K]] + idx[:, TOP_K:2 * TOP_K]
            dest_blocks = dest.reshape(n_tiles, 1, TOP_K * TOKEN_TILE)
            n_blocks = -(-(n_tok * TOP_K + n_exp * (MOE_ROW_BLOCK - 1)) // MOE_ROW_BLOCK)
            starts = jnp.arange(n_blocks) * MOE_ROW_BLOCK
            block_expert = jnp.minimum(jnp.sum(starts[:, None] >= pad_end[None, :], axis=-1),
                                       n_exp - 1).astype(jnp.int32)
            n_used = (pad_end[-1] // MOE_ROW_BLOCK).astype(jnp.int32).reshape(1)
            n_rows_x = n_blocks * MOE_ROW_BLOCK
            zero_from = jnp.concatenate([pad_start + counts, pad_end[-1:]]).astype(jnp.int32)
            zero_cnt = jnp.concatenate([padded - counts, n_rows_x - pad_end[-1:]]).astype(jnp.int32)
            xs = _dispatch_call(zero_from, zero_cnt, dest_blocks, xc, mod, gn, n_rows_x)
            y_rows = _moe_call(block_expert, n_used, xs, w_moe_gu[j].astype(BF16), w_moe_dn[j].astype(BF16))
            if i == depth - 1:
                return _combine_call(dest_blocks, xc, mod, wts, y_rows, norm_final.reshape(1, d), s_len)
            xc = _combine_call(dest_blocks, xc, mod, wts, y_rows)
    return _final_call(xc, norm_final.reshape(1, d), s_len)
```
